```python
import jax
import jax.numpy as jnp
from jax import lax
import numpy as np

D_MODEL = 2048
BATCH = 2
SEQ = 16384
DEPTH = 2

GRID_W = 64
CTX_LEN = 256
HEAD_DIM = 128
A_Q_HEADS = 6
A_KV_HEADS = 2
A_WINDOW = 128
A_BLOCK = 128
B_HEADS = 6
B_WIN_H = 8
B_WIN_W = 16
C_CHANNELS = D_MODEL - (A_Q_HEADS + B_HEADS) * HEAD_DIM
C_CONV_WIDTH = 31
D_FF = 5632
ROPE_THETA = 10000.0
NORM_EPS = 1e-6
NEG_INF = -1e30
N_MOD = 9
A_Q_COLS = A_Q_HEADS * HEAD_DIM
A_KV_COLS = A_KV_HEADS * HEAD_DIM
B_COLS = B_HEADS * HEAD_DIM
IN_COLS = A_Q_COLS + 2 * A_KV_COLS + 3 * B_COLS + 2 * C_CHANNELS
MIX_WIDTH = A_Q_COLS + B_COLS + C_CHANNELS

kernel_name = "hymba_style_dit_window_natten_conformer"


def rms_norm(x, g):
    xf = x.astype(jnp.float32)
    y = xf * lax.rsqrt(jnp.mean(xf * xf, axis=-1, keepdims=True) + NORM_EPS)
    return (y * g.astype(jnp.float32)).astype(x.dtype)


def layer_norm(x, g, b):
    xf = x.astype(jnp.float32)
    mu = jnp.mean(xf, axis=-1, keepdims=True)
    xc = xf - mu
    y = xc * lax.rsqrt(jnp.mean(xc * xc, axis=-1, keepdims=True) + NORM_EPS)
    return (y * g.astype(jnp.float32) + b.astype(jnp.float32)).astype(x.dtype)


def modulate(x, g, shift, scale):
    return rms_norm(x, g) * (1 + scale) + shift


def swiglu(h, w_gate, w_up, w_down):
    return (jax.nn.silu(h @ w_gate) * (h @ w_up)) @ w_down


def axial_rope_tables(n_tokens):
    t = jnp.arange(n_tokens, dtype=jnp.int32)
    row = (t // GRID_W).astype(jnp.float32)
    col = (t % GRID_W).astype(jnp.float32)
    n_freq = HEAD_DIM // 4
    inv_freq = ROPE_THETA ** (-jnp.arange(n_freq, dtype=jnp.float32) / n_freq)
    ang_r = row[:, None] * inv_freq[None, :]
    ang_c = col[:, None] * inv_freq[None, :]
    return (jnp.cos(ang_r), jnp.sin(ang_r), jnp.cos(ang_c), jnp.sin(ang_c))


def _rotate(x, cos, sin):
    x1, x2 = jnp.split(x, 2, axis=-1)
    return jnp.concatenate([x1 * cos - x2 * sin, x2 * cos + x1 * sin], axis=-1)


def apply_axial_rope(x, tables):
    cos_r, sin_r, cos_c, sin_c = (t[None, :, None, :] for t in tables)
    xf = x.astype(jnp.float32)
    half = HEAD_DIM // 2
    out = jnp.concatenate([_rotate(xf[..., :half], cos_r, sin_r),
                           _rotate(xf[..., half:], cos_c, sin_c)], axis=-1)
    return out.astype(x.dtype)


def split_projection(p):
    b, n, _ = p.shape
    offs = list(np.cumsum([A_Q_COLS, A_KV_COLS, A_KV_COLS, B_COLS, B_COLS, B_COLS]))
    aq, ak, av, bq, bk, bv, cu = jnp.split(p, offs, axis=-1)
    heads = lambda t: t.reshape(b, n, -1, HEAD_DIM)
    return heads(aq), heads(ak), heads(av), heads(bq), heads(bk), heads(bv), cu


def context_attention(q, k, v, sink):
    b, n, hq, d = q.shape
    hkv = k.shape[2]
    g = hq // hkv
    qg = q.reshape(b, n, hkv, g, d)
    s = jnp.einsum('bqhgd,bkhd->bhgqk', qg, k).astype(jnp.float32) * (d ** -0.5)
    if sink is not None:
        s_sink = jnp.broadcast_to(sink.astype(jnp.float32).reshape(1, hkv, g, 1, 1), s.shape[:-1] + (1,))
        p = jax.nn.softmax(jnp.concatenate([s_sink, s], axis=-1), axis=-1)[..., 1:]
    else:
        p = jax.nn.softmax(s, axis=-1)
    o = jnp.einsum('bhgqk,bkhd->bqhgd', p.astype(v.dtype), v)
    return o.reshape(b, n, hq * d)


def windowed_gqa(q, k, v, kc, vc, sink):
    b, L, hq, d = q.shape
    hkv = k.shape[2]
    g = hq // hkv
    nb = L // A_BLOCK
    scale = d ** -0.5
    qb = q.reshape(b, nb, A_BLOCK, hkv, g, d)
    pad = ((0, 0), (A_BLOCK, A_BLOCK), (0, 0), (0, 0))
    kp = jnp.pad(k, pad).reshape(b, nb + 2, A_BLOCK, hkv, d)
    vp = jnp.pad(v, pad).reshape(b, nb + 2, A_BLOCK, hkv, d)
    kw = jnp.concatenate([kp[:, :-2], kp[:, 1:-1], kp[:, 2:]], axis=2)
    vw = jnp.concatenate([vp[:, :-2], vp[:, 1:-1], vp[:, 2:]], axis=2)
    rel = np.arange(3 * A_BLOCK)[None, :] - A_BLOCK - np.arange(A_BLOCK)[:, None]
    band = np.abs(rel) <= A_WINDOW
    kpos = (np.arange(nb)[:, None] - 1) * A_BLOCK + np.arange(3 * A_BLOCK)[None, :]
    inside = (kpos >= 0) & (kpos < L)
    mask = band[None, :, :] & inside[:, None, :]
    s_loc = jnp.einsum('bnqhgd,bnkhd->bhgnqk', qb, kw).astype(jnp.float32) * scale
    s_loc = jnp.where(mask, s_loc, NEG_INF)
    s_ctx = jnp.einsum('bnqhgd,bchd->bhgnqc', qb, kc).astype(jnp.float32) * scale
    s_sink = jnp.broadcast_to(sink.astype(jnp.float32).reshape(1, hkv, g, 1, 1, 1), s_ctx.shape[:-1] + (1,))
    p = jax.nn.softmax(jnp.concatenate([s_sink, s_loc, s_ctx], axis=-1), axis=-1)
    n_loc = 3 * A_BLOCK
    p_loc = p[..., 1:1 + n_loc].astype(v.dtype)
    p_ctx = p[..., 1 + n_loc:].astype(v.dtype)
    o = (jnp.einsum('bhgnqk,bnkhd->bnqhgd', p_loc, vw)
         + jnp.einsum('bhgnqc,bchd->bnqhgd', p_ctx, vc))
    return o.reshape(b, L, hq * d)


def neighbourhood_attention(q, k, v, kc, vc, rpb):
    b, L, h, d = q.shape
    rows = L // GRID_W
    kh = min(B_WIN_H, rows)
    kw = B_WIN_W
    scale = d ** -0.5
    col_start = np.clip(np.arange(GRID_W) - kw // 2, 0, GRID_W - kw)
    col_idx = col_start[:, None] + np.arange(kw)[None, :]
    col_bias_idx = col_idx - np.arange(GRID_W)[:, None] + (B_WIN_W - 1)
    qg = q.reshape(b, rows, GRID_W, h, d).transpose(1, 0, 3, 2, 4)
    kg = k.reshape(b, rows, GRID_W, h, d).transpose(0, 3, 1, 2, 4)
    vg = v.reshape(b, rows, GRID_W, h, d).transpose(0, 3, 1, 2, 4)
    kct = kc.transpose(0, 2, 1, 3)
    vct = vc.transpose(0, 2, 1, 3)
    n_loc = kh * kw

    def row_block(args):
        r, q_r = args
        rs = jnp.clip(r - kh // 2, 0, rows - kh)
        k_rows = lax.dynamic_slice_in_dim(kg, rs, kh, axis=2)
        v_rows = lax.dynamic_slice_in_dim(vg, rs, kh, axis=2)
        k_win = k_rows[:, :, :, col_idx, :]
        v_win = v_rows[:, :, :, col_idx, :]
        s_loc = jnp.einsum('bhjd,bhajcd->bhjac', q_r, k_win).astype(jnp.float32) * scale
        row_bias_idx = rs + jnp.arange(kh, dtype=jnp.int32) - r + (B_WIN_H - 1)
        bias = rpb[:, row_bias_idx][:, :, col_bias_idx]
        s_loc = s_loc + bias.transpose(0, 2, 1, 3).astype(jnp.float32)[None]
        s_loc = s_loc.reshape(b, h, GRID_W, n_loc)
        s_ctx = jnp.einsum('bhjd,bhcd->bhjc', q_r, kct).astype(jnp.float32) * scale
        p = jax.nn.softmax(jnp.concatenate([s_loc, s_ctx], axis=-1), axis=-1)
        p_loc = p[..., :n_loc].reshape(b, h, GRID_W, kh, kw).astype(v.dtype)
        p_ctx = p[..., n_loc:].astype(v.dtype)
        return (jnp.einsum('bhjac,bhajcd->bhjd', p_loc, v_win)
                + jnp.einsum('bhjc,bhcd->bhjd', p_ctx, vct))

    out = lax.map(row_block, (jnp.arange(rows, dtype=jnp.int32), qg))
    return out.transpose(1, 0, 3, 2, 4).reshape(b, L, h * d)


def conv_module(u, dw_w, dw_b, ln_g, ln_b):
    a, gt = jnp.split(u, 2, axis=-1)
    hh = a * jax.nn.sigmoid(gt)
    pad = (C_CONV_WIDTH - 1) // 2
    hh = lax.conv_general_dilated(hh, dw_w[:, None, :], window_strides=(1,), padding=[(pad, pad)],
                                  dimension_numbers=('NWC', 'WIO', 'NWC'),
                                  feature_group_count=C_CHANNELS) + dw_b
    return jax.nn.silu(layer_norm(hh, ln_g, ln_b))


def mixer(h, hc, w_in, w_out, a_q_norm, a_k_norm, a_sink, b_q_norm, b_k_norm, b_rpb,
          c_dw_w, c_dw_b, c_ln_g, c_ln_b, rope, with_ctx_out):
    aq, ak, av, bq, bk, bv, cu = split_projection(h @ w_in)
    caq, cak, cav, cbq, cbk, cbv, ccu = split_projection(hc @ w_in)
    aq = apply_axial_rope(rms_norm(aq, a_q_norm), rope)
    ak = apply_axial_rope(rms_norm(ak, a_k_norm), rope)
    bq = rms_norm(bq, b_q_norm)
    bk = rms_norm(bk, b_k_norm)
    cak = rms_norm(cak, a_k_norm)
    cbk = rms_norm(cbk, b_k_norm)
    o_a = windowed_gqa(aq, ak, av, cak, cav, a_sink)
    o_b = neighbourhood_attention(bq, bk, bv, cbk, cbv, b_rpb)
    o_c = conv_module(cu, c_dw_w, c_dw_b, c_ln_g, c_ln_b)
    y = jnp.concatenate([o_a, o_b, o_c], axis=-1) @ w_out
    if not with_ctx_out:
        return y, None
    co_a = context_attention(rms_norm(caq, a_q_norm), cak, cav, a_sink)
    co_b = context_attention(rms_norm(cbq, b_q_norm), cbk, cbv, None)
    co_c = conv_module(ccu, c_dw_w, c_dw_b, c_ln_g, c_ln_b)
    y_ctx = jnp.concatenate([co_a, co_b, co_c], axis=-1) @ w_out
    return y, y_ctx


def setup_inputs(seed: int = 0) -> dict:
    key = jax.random.key(seed)
    ks = jax.random.split(key, 32)
    f32 = jnp.float32
    D = D_MODEL

    def normal(k, shape, s):
        return jax.random.normal(k, shape, f32) * s

    def gain(k, shape):
        return 1.0 + 0.05 * jax.random.normal(k, shape, f32)

    return {
        "x": normal(ks[0], (BATCH, SEQ, D), 1.0),
        "c": normal(ks[1], (BATCH, D), 1.0),
        "ctx": normal(ks[2], (BATCH, CTX_LEN, D), 1.0),
        "c_ctx": normal(ks[3], (D,), 1.0),
        "w_mod": normal(ks[4], (DEPTH, D, N_MOD * D), 0.5 * D ** -0.5),
        "b_mod": normal(ks[5], (DEPTH, N_MOD * D), 0.02),
        "norm_ffn1": gain(ks[6], (DEPTH, D)),
        "norm_mix": gain(ks[7], (DEPTH, D)),
        "norm_ffn2": gain(ks[8], (DEPTH, D)),
        "ffn1_w_gate": normal(ks[9], (DEPTH, D, D_FF), D ** -0.5),
        "ffn1_w_up": normal(ks[10], (DEPTH, D, D_FF), D ** -0.5),
        "ffn1_w_down": normal(ks[11], (DEPTH, D_FF, D), D_FF ** -0.5),
        "ffn2_w_gate": normal(ks[12], (DEPTH, D, D_FF), D ** -0.5),
        "ffn2_w_up": normal(ks[13], (DEPTH, D, D_FF), D ** -0.5),
        "ffn2_w_down": normal(ks[14], (DEPTH, D_FF, D), D_FF ** -0.5),
        "w_in": normal(ks[15], (DEPTH, D, IN_COLS), D ** -0.5),
        "w_out": normal(ks[16], (DEPTH, MIX_WIDTH, D), MIX_WIDTH ** -0.5),
        "a_q_norm": gain(ks[17], (DEPTH, HEAD_DIM)),
        "a_k_norm": gain(ks[18], (DEPTH, HEAD_DIM)),
        "a_sink": normal(ks[19], (DEPTH, A_Q_HEADS), 0.5),
        "b_q_norm": gain(ks[20], (DEPTH, HEAD_DIM)),
        "b_k_norm": gain(ks[21], (DEPTH, HEAD_DIM)),
        "b_rpb": normal(ks[22], (DEPTH, B_HEADS, 2 * B_WIN_H - 1, 2 * B_WIN_W - 1), 0.1),
        "c_dw_w": normal(ks[23], (DEPTH, C_CONV_WIDTH, C_CHANNELS), C_CONV_WIDTH ** -0.5),
        "c_dw_b": normal(ks[24], (DEPTH, C_CHANNELS), 0.02),
        "c_ln_g": gain(ks[25], (DEPTH, C_CHANNELS)),
        "c_ln_b": normal(ks[26], (DEPTH, C_CHANNELS), 0.02),
    }


def reference(x, c, ctx, c_ctx, w_mod, b_mod, norm_ffn1, norm_mix, norm_ffn2,
              ffn1_w_gate, ffn1_w_up, ffn1_w_down, ffn2_w_gate, ffn2_w_up, ffn2_w_down,
              w_in, w_out, a_q_norm, a_k_norm, a_sink, b_q_norm, b_k_norm, b_rpb,
              c_dw_w, c_dw_b, c_ln_g, c_ln_b):
    rope = axial_rope_tables(x.shape[1])
    for l in range(DEPTH):
        last = l == DEPTH - 1
        m = jnp.split((jax.nn.silu(c) @ w_mod[l] + b_mod[l])[:, None, :], N_MOD, axis=-1)
        mc = jnp.split(jax.nn.silu(c_ctx) @ w_mod[l] + b_mod[l], N_MOD, axis=-1)
        x = x + 0.5 * m[2] * swiglu(modulate(x, norm_ffn1[l], m[0], m[1]),
                                    ffn1_w_gate[l], ffn1_w_up[l], ffn1_w_down[l])
        ctx = ctx + 0.5 * mc[2] * swiglu(modulate(ctx, norm_ffn1[l], mc[0], mc[1]),
                                         ffn1_w_gate[l], ffn1_w_up[l], ffn1_w_down[l])
        h = modulate(x, norm_mix[l], m[3], m[4])
        hc = modulate(ctx, norm_mix[l], mc[3], mc[4])
        y, y_ctx = mixer(h, hc, w_in[l], w_out[l], a_q_norm[l], a_k_norm[l], a_sink[l],
                         b_q_norm[l], b_k_norm[l], b_rpb[l], c_dw_w[l], c_dw_b[l],
                         c_ln_g[l], c_ln_b[l], rope, not last)
        x = x + m[5] * y
        x = x + 0.5 * m[8] * swiglu(modulate(x, norm_ffn2[l], m[6], m[7]),
                                    ffn2_w_gate[l], ffn2_w_up[l], ffn2_w_down[l])
        if not last:
            ctx = ctx + mc[5] * y_ctx
            ctx = ctx + 0.5 * mc[8] * swiglu(modulate(ctx, norm_ffn2[l], mc[6], mc[7]),
                                             ffn2_w_gate[l], ffn2_w_up[l], ffn2_w_down[l])
    return x
```

```python
import functools

import numpy as np
import jax
import jax.numpy as jnp
from jax import lax
from jax.experimental import pallas as pl
from jax.experimental.pallas import tpu as pltpu

D_MODEL = 2048
BATCH = 2
SEQ = 16384
DEPTH = 2
GRID_W = 64
GRID_ROWS = SEQ // GRID_W
CTX_LEN = 256
HEAD_DIM = 128
A_Q_HEADS = 6
A_KV_HEADS = 2
A_GROUP = A_Q_HEADS // A_KV_HEADS
A_WINDOW = 128
B_HEADS = 6
B_WIN_H = 8
B_WIN_W = 16
C_CHANNELS = 512
C_CONV_WIDTH = 31
C_PAD = (C_CONV_WIDTH - 1) // 2
D_FF = 5632
ROPE_THETA = 10000.0
NORM_EPS = 1e-6
NEG_INF = -1e30
N_MOD = 9
IN_COLS = 4608
MIX_WIDTH = 2048

HEAD_AQ = 0
HEAD_AK = HEAD_AQ + A_Q_HEADS
HEAD_AV = HEAD_AK + A_KV_HEADS
HEAD_BQ = HEAD_AV + A_KV_HEADS
HEAD_BK = HEAD_BQ + B_HEADS
HEAD_BV = HEAD_BK + B_HEADS
N_QKV_HEADS = HEAD_BV + B_HEADS
C_COL0 = N_QKV_HEADS * HEAD_DIM

N_LAT = BATCH * SEQ
N_CTX = BATCH * CTX_LEN
NT = N_LAT + N_CTX

V7X_VMEM_BYTES = 64 * 1024 * 1024
LANES = 128
MXU_DIM = 256

TM = 512
LAT_TILES = N_LAT // TM
TILES_PER_BATCH = SEQ // TM
ALL_TILES = NT // TM
TF = 512
TC = 256
A_TQ = 256
A_TK = A_TQ + 2 * A_WINDOW
CONV_HALO = 16

F32 = jnp.float32
BF16 = jnp.bfloat16


def _vmem_limit(block_bytes):
    return int(min(V7X_VMEM_BYTES - 4 * 1024 * 1024, block_bytes + 12 * 1024 * 1024))


def _group_of_tile(i):
    return i // TILES_PER_BATCH


def _modulated(x, gain, shift, scale):
    ms = jnp.mean(x * x, axis=-1, keepdims=True)
    return (x * lax.rsqrt(ms + NORM_EPS) * gain) * (1.0 + scale) + shift


MOD_TN = 1024


def _mod_kernel(c_ref, w_ref, b_ref, o_ref):
    c = c_ref[...]
    a = c * jax.nn.sigmoid(c)
    o_ref[...] = jnp.dot(a, w_ref[...], preferred_element_type=F32,
                         precision=lax.Precision.HIGHEST) + b_ref[...]


def _mod_vectors(c, c_ctx, w_mod, b_mod):
    rows = jnp.concatenate([c, c_ctx[None, :], jnp.zeros((8 - BATCH - 1, D_MODEL), F32)], axis=0)
    n = N_MOD * D_MODEL
    out = pl.pallas_call(
        _mod_kernel,
        grid=(DEPTH, n // MOD_TN),
        in_specs=[
            pl.BlockSpec((8, D_MODEL), lambda l, j: (0, 0)),
            pl.BlockSpec((None, D_MODEL, MOD_TN), lambda l, j: (l, 0, j)),
            pl.BlockSpec((None, 1, MOD_TN), lambda l, j: (l, 0, j)),
        ],
        out_specs=pl.BlockSpec((None, 8, MOD_TN), lambda l, j: (l, 0, j)),
        out_shape=jax.ShapeDtypeStruct((DEPTH, 8, n), F32),
        compiler_params=pltpu.CompilerParams(
            dimension_semantics=("arbitrary", "arbitrary"),
            vmem_limit_bytes=_vmem_limit(2 * D_MODEL * MOD_TN * 4)),
        name="mod_vectors",
    )(rows, w_mod, b_mod.reshape(DEPTH, 1, n))
    return out[:, :3].reshape(DEPTH, 3, 3, 3, D_MODEL)


def _ffn_kernel(x_ref, mod_ref, gain_ref, wg_ref, wu_ref, wd_ref, o_ref, h_ref, acc_ref):
    j = pl.program_id(1)

    @pl.when(j == 0)
    def _():
        h = _modulated(x_ref[...], gain_ref[...], mod_ref[0:1, :], mod_ref[1:2, :])
        h_ref[...] = h.astype(BF16)
        acc_ref[...] = jnp.zeros_like(acc_ref)

    h = h_ref[...]
    g = jnp.dot(h, wg_ref[...], preferred_element_type=F32)
    u = jnp.dot(h, wu_ref[...], preferred_element_type=F32)
    a = (g * jax.nn.sigmoid(g)) * u
    acc_ref[...] += jnp.dot(a.astype(BF16), wd_ref[...], preferred_element_type=F32)

    @pl.when(j == pl.num_programs(1) - 1)
    def _():
        o_ref[...] = x_ref[...] + (0.5 * mod_ref[2:3, :]) * acc_ref[...]


def _ffn(xs, mod_l, sub, gain, wg, wu, wd, n_tiles):
    blocks = (4 * TM * D_MODEL * 4 + 6 * D_MODEL * TF * 2 + TM * D_MODEL * (2 + 4))
    return pl.pallas_call(
        _ffn_kernel,
        grid=(n_tiles, D_FF // TF),
        in_specs=[
            pl.BlockSpec((TM, D_MODEL), lambda i, j: (i, 0)),
            pl.BlockSpec((None, None, 3, D_MODEL), lambda i, j: (_group_of_tile(i), sub, 0, 0)),
            pl.BlockSpec((1, D_MODEL), lambda i, j: (0, 0)),
            pl.BlockSpec((D_MODEL, TF), lambda i, j: (0, j)),
            pl.BlockSpec((D_MODEL, TF), lambda i, j: (0, j)),
            pl.BlockSpec((TF, D_MODEL), lambda i, j: (j, 0)),
        ],
        out_specs=pl.BlockSpec((TM, D_MODEL), lambda i, j: (i, 0)),
        out_shape=jax.ShapeDtypeStruct((NT, D_MODEL), F32),
        scratch_shapes=[pltpu.VMEM((TM, D_MODEL), BF16), pltpu.VMEM((TM, D_MODEL), F32)],
        input_output_aliases={0: 0},
        compiler_params=pltpu.CompilerParams(
            dimension_semantics=("arbitrary", "arbitrary"),
            vmem_limit_bytes=_vmem_limit(blocks)),
        name="ffn",
    )(xs, mod_l, gain.reshape(1, D_MODEL), wg, wu, wd)


GAIN_AQ, GAIN_AK, GAIN_BQ, GAIN_BK = 0, 1, 2, 3


def _head_kind(h):
    if h < HEAD_AK:
        return GAIN_AQ, True
    if h < HEAD_AV:
        return GAIN_AK, True
    if h < HEAD_BQ:
        return None, False
    if h < HEAD_BK:
        return GAIN_BQ, False
    if h < HEAD_BV:
        return GAIN_BK, False
    return None, False


def _inproj_kernel(x_ref, mod_ref, gain_ref, w_ref, hg_ref, cos_ref, sin_ref, qkv_ref, cu_ref, h_ref):
    h = _modulated(x_ref[...], gain_ref[...], mod_ref[0:1, :], mod_ref[1:2, :])
    h_ref[...] = h.astype(BF16)
    lane = lax.broadcasted_iota(jnp.int32, (TM, HEAD_DIM), 1)
    even_quarter = ((lane // (HEAD_DIM // 4)) % 2) == 0
    for pair in range(N_QKV_HEADS // 2):
        c0 = pair * 2 * HEAD_DIM
        y2 = jnp.dot(h_ref[...], w_ref[:, c0:c0 + 2 * HEAD_DIM], preferred_element_type=F32)
        for half in range(2):
            hd = 2 * pair + half
            y = y2[:, half * HEAD_DIM:(half + 1) * HEAD_DIM]
            gain_row, rotary = _head_kind(hd)
            if gain_row is not None:
                ms = jnp.mean(y * y, axis=-1, keepdims=True)
                y = y * lax.rsqrt(ms + NORM_EPS) * hg_ref[gain_row:gain_row + 1, :]
            if rotary:
                swapped = jnp.where(even_quarter,
                                    pltpu.roll(y, HEAD_DIM - HEAD_DIM // 4, 1),
                                    pltpu.roll(y, HEAD_DIM // 4, 1))
                y = y * cos_ref[...] + swapped * sin_ref[...]
            qkv_ref[hd] = y.astype(BF16)
    for blk in range((IN_COLS - C_COL0) // (2 * HEAD_DIM)):
        c0 = C_COL0 + blk * 2 * HEAD_DIM
        cu_ref[:, blk * 2 * HEAD_DIM:(blk + 1) * 2 * HEAD_DIM] = jnp.dot(
            h_ref[...], w_ref[:, c0:c0 + 2 * HEAD_DIM], preferred_element_type=F32)


def _inproj(xs, mod_l, gain, w_in, head_gains, cos_t, sin_t):
    n_cu = IN_COLS - C_COL0
    blocks = (2 * TM * D_MODEL * 4 + D_MODEL * IN_COLS * 2 + 2 * N_QKV_HEADS * TM * HEAD_DIM * 2
              + 2 * TM * n_cu * 4 + 4 * TM * HEAD_DIM * 4 + TM * D_MODEL * 2)
    rope_block = lambda i: (jnp.where(i < LAT_TILES, i % TILES_PER_BATCH, TILES_PER_BATCH), 0)
    return pl.pallas_call(
        _inproj_kernel,
        grid=(ALL_TILES,),
        in_specs=[
            pl.BlockSpec((TM, D_MODEL), lambda i: (i, 0)),
            pl.BlockSpec((None, None, 3, D_MODEL), lambda i: (_group_of_tile(i), 1, 0, 0)),
            pl.BlockSpec((1, D_MODEL), lambda i: (0, 0)),
            pl.BlockSpec((D_MODEL, IN_COLS), lambda i: (0, 0), pipeline_mode=pl.Buffered(1)),
            pl.BlockSpec((8, HEAD_DIM), lambda i: (0, 0)),
            pl.BlockSpec((TM, HEAD_DIM), rope_block),
            pl.BlockSpec((TM, HEAD_DIM), rope_block),
        ],
        out_specs=[
            pl.BlockSpec((N_QKV_HEADS, TM, HEAD_DIM), lambda i: (0, i, 0)),
            pl.BlockSpec((TM, n_cu), lambda i: (i, 0)),
        ],
        out_shape=[
            jax.ShapeDtypeStruct((N_QKV_HEADS, NT, HEAD_DIM), BF16),
            jax.ShapeDtypeStruct((NT, n_cu), F32),
        ],
        scratch_shapes=[pltpu.VMEM((TM, D_MODEL), BF16)],
        compiler_params=pltpu.CompilerParams(
            dimension_semantics=("arbitrary",),
            vmem_limit_bytes=_vmem_limit(blocks)),
        name="inproj",
    )(xs, mod_l, gain.reshape(1, D_MODEL), w_in, head_gains, cos_t, sin_t)


def _rope_tables():
    t = jnp.arange(SEQ, dtype=jnp.int32)
    row = (t // GRID_W).astype(F32)
    col = (t % GRID_W).astype(F32)
    n_freq = HEAD_DIM // 4
    inv_freq = ROPE_THETA ** (-jnp.arange(n_freq, dtype=F32) / n_freq)
    ang_r = row[:, None] * inv_freq[None, :]
    ang_c = col[:, None] * inv_freq[None, :]
    cos_t = jnp.concatenate([jnp.cos(ang_r), jnp.cos(ang_r), jnp.cos(ang_c), jnp.cos(ang_c)], axis=-1)
    sin_t = jnp.concatenate([-jnp.sin(ang_r), jnp.sin(ang_r), -jnp.sin(ang_c), jnp.sin(ang_c)], axis=-1)
    cos_t = jnp.concatenate([cos_t, jnp.ones((TM, HEAD_DIM), F32)], axis=0)
    sin_t = jnp.concatenate([sin_t, jnp.zeros((TM, HEAD_DIM), F32)], axis=0)
    return cos_t, sin_t


def _dot_nt(a, b):
    return lax.dot_general(a, b, (((1,), (1,)), ((), ())), preferred_element_type=F32)


def _attn_a_kernel(sink_ref, q_ref, k_ref, v_ref, kc_ref, vc_ref, o_ref):
    head = pl.program_id(1) * A_GROUP + pl.program_id(2)
    sink = sink_ref[head]
    kc = kc_ref[...]
    vc = vc_ref[...]
    rel = (lax.broadcasted_iota(jnp.int32, (A_TQ, A_TK), 1)
           - lax.broadcasted_iota(jnp.int32, (A_TQ, A_TK), 0))

    def body(i, carry):
        q0 = pl.multiple_of(i * A_TQ, A_TQ)
        k0 = pl.multiple_of(jnp.clip(q0 - A_WINDOW, 0, SEQ - A_TK), A_WINDOW)
        q = q_ref[pl.ds(q0, A_TQ), :]
        k = k_ref[pl.ds(k0, A_TK), :]
        v = v_ref[pl.ds(k0, A_TK), :]
        s = _dot_nt(q, k)
        s = jnp.where(jnp.abs(rel + (k0 - q0)) <= A_WINDOW, s, NEG_INF)
        sc = _dot_nt(q, kc)
        m = jnp.maximum(jnp.maximum(jnp.max(s, axis=-1, keepdims=True),
                                    jnp.max(sc, axis=-1, keepdims=True)), sink)
        p = jnp.exp(s - m)
        pc = jnp.exp(sc - m)
        den = (jnp.sum(p, axis=-1, keepdims=True) + jnp.sum(pc, axis=-1, keepdims=True)
               + jnp.exp(sink - m))
        o = (jnp.dot(p.astype(BF16), v, preferred_element_type=F32)
             + jnp.dot(pc.astype(BF16), vc, preferred_element_type=F32))
        o_ref[pl.ds(q0, A_TQ), :] = (o / den).astype(BF16)
        return carry

    lax.fori_loop(0, SEQ // A_TQ, body, 0)


def _attn_a(qkv, sink):
    seq_block = (None, SEQ, HEAD_DIM)
    ctx_block = (None, CTX_LEN, HEAD_DIM)
    ctx_row0 = N_LAT // CTX_LEN
    blocks = 2 * 4 * SEQ * HEAD_DIM * 2 + 4 * CTX_LEN * HEAD_DIM * 2
    return pl.pallas_call(
        _attn_a_kernel,
        grid=(BATCH, A_KV_HEADS, A_GROUP),
        in_specs=[
            pl.BlockSpec(memory_space=pltpu.SMEM),
            pl.BlockSpec(seq_block, lambda b, kv, g: (HEAD_AQ + kv * A_GROUP + g, b, 0)),
            pl.BlockSpec(seq_block, lambda b, kv, g: (HEAD_AK + kv, b, 0)),
            pl.BlockSpec(seq_block, lambda b, kv, g: (HEAD_AV + kv, b, 0)),
            pl.BlockSpec(ctx_block, lambda b, kv, g: (HEAD_AK + kv, ctx_row0 + b, 0)),
            pl.BlockSpec(ctx_block, lambda b, kv, g: (HEAD_AV + kv, ctx_row0 + b, 0)),
        ],
        out_specs=pl.BlockSpec((SEQ, HEAD_DIM), lambda b, kv, g: (b, kv * A_GROUP + g)),
        out_shape=jax.ShapeDtypeStruct((N_LAT, A_Q_HEADS * HEAD_DIM), BF16),
        compiler_params=pltpu.CompilerParams(
            dimension_semantics=("arbitrary", "arbitrary", "arbitrary"),
            vmem_limit_bytes=_vmem_limit(blocks)),
        name="attn_window",
    )(sink, qkv, qkv, qkv, qkv, qkv)


B_KEYS = B_WIN_H * GRID_W
B_SHIFTS = 8


def _attn_b_kernel(q_ref, k_ref, v_ref, kc_ref, vc_ref, bias_ref, o_ref):
    kc = kc_ref[...]
    vc = vc_ref[...]

    def body(r, carry):
        rs = jnp.clip(r - B_WIN_H // 2, 0, GRID_ROWS - B_WIN_H)
        shift = rs - r + (B_WIN_H - 1)
        q0 = pl.multiple_of(r * GRID_W, GRID_W)
        k0 = pl.multiple_of(rs * GRID_W, GRID_W)
        q = q_ref[pl.ds(q0, GRID_W), :]
        k = k_ref[pl.ds(k0, B_KEYS), :]
        v = v_ref[pl.ds(k0, B_KEYS), :]
        s = _dot_nt(q, k) + bias_ref[shift]
        sc = _dot_nt(q, kc)
        m = jnp.maximum(jnp.max(s, axis=-1, keepdims=True), jnp.max(sc, axis=-1, keepdims=True))
        p = jnp.exp(s - m)
        pc = jnp.exp(sc - m)
        den = jnp.sum(p, axis=-1, keepdims=True) + jnp.sum(pc, axis=-1, keepdims=True)
        o = (jnp.dot(p.astype(BF16), v, preferred_element_type=F32)
             + jnp.dot(pc.astype(BF16), vc, preferred_element_type=F32))
        o_ref[pl.ds(q0, GRID_W), :] = (o / den).astype(BF16)
        return carry

    lax.fori_loop(0, GRID_ROWS, body, 0)


def _b_bias_table(rpb):
    j = np.arange(GRID_W)[:, None]
    jk = np.arange(GRID_W)[None, :]
    col_start = np.clip(j - B_WIN_W // 2, 0, GRID_W - B_WIN_W)
    inside = (jk >= col_start) & (jk < col_start + B_WIN_W)
    col_idx = np.clip(jk - j + (B_WIN_W - 1), 0, 2 * B_WIN_W - 2)
    d = np.arange(B_SHIFTS)[:, None]
    a = np.arange(B_WIN_H)[None, :]
    row_idx = np.clip(d + a, 0, 2 * B_WIN_H - 2)
    t = rpb[:, row_idx][:, :, :, col_idx]
    t = jnp.where(inside[None, None, None], t, NEG_INF)
    return t.transpose(0, 1, 3, 2, 4).reshape(B_HEADS, B_SHIFTS, GRID_W, B_KEYS)


def _attn_b(qkv, bias):
    seq_block = (None, SEQ, HEAD_DIM)
    ctx_block = (None, CTX_LEN, HEAD_DIM)
    ctx_row0 = N_LAT // CTX_LEN
    blocks = 2 * 4 * SEQ * HEAD_DIM * 2 + 4 * CTX_LEN * HEAD_DIM * 2 + 2 * B_SHIFTS * GRID_W * B_KEYS * 4
    return pl.pallas_call(
        _attn_b_kernel,
        grid=(BATCH, B_HEADS),
        in_specs=[
            pl.BlockSpec(seq_block, lambda b, h: (HEAD_BQ + h, b, 0)),
            pl.BlockSpec(seq_block, lambda b, h: (HEAD_BK + h, b, 0)),
            pl.BlockSpec(seq_block, lambda b, h: (HEAD_BV + h, b, 0)),
            pl.BlockSpec(ctx_block, lambda b, h: (HEAD_BK + h, ctx_row0 + b, 0)),
            pl.BlockSpec(ctx_block, lambda b, h: (HEAD_BV + h, ctx_row0 + b, 0)),
            pl.BlockSpec((None, B_SHIFTS, GRID_W, B_KEYS), lambda b, h: (h, 0, 0, 0)),
        ],
        out_specs=pl.BlockSpec((SEQ, HEAD_DIM), lambda b, h: (b, h)),
        out_shape=jax.ShapeDtypeStruct((N_LAT, B_HEADS * HEAD_DIM), BF16),
        compiler_params=pltpu.CompilerParams(
            dimension_semantics=("arbitrary", "arbitrary"),
            vmem_limit_bytes=_vmem_limit(blocks)),
        name="attn_neighbourhood",
    )(qkv, qkv, qkv, qkv, qkv, bias)


N_CTX_HEADS = A_Q_HEADS + B_HEADS


def _attn_ctx_kernel(sink_ref, q_ref, k_ref, v_ref, o_ref):
    sink = sink_ref[pl.program_id(1)]
    q = q_ref[...]
    s = _dot_nt(q, k_ref[...])
    m = jnp.maximum(jnp.max(s, axis=-1, keepdims=True), sink)
    p = jnp.exp(s - m)
    den = jnp.sum(p, axis=-1, keepdims=True) + jnp.exp(sink - m)
    o = jnp.dot(p.astype(BF16), v_ref[...], preferred_element_type=F32)
    o_ref[...] = (o / den).astype(BF16)


def _attn_ctx(qkv, sinks):
    ctx_block = (None, CTX_LEN, HEAD_DIM)
    ctx_row0 = N_LAT // CTX_LEN
    is_b = lambda h: h >= A_Q_HEADS
    q_head = lambda h: jnp.where(is_b(h), HEAD_BQ + h - A_Q_HEADS, HEAD_AQ + h)
    k_head = lambda h: jnp.where(is_b(h), HEAD_BK + h - A_Q_HEADS, HEAD_AK + h // A_GROUP)
    v_head = lambda h: jnp.where(is_b(h), HEAD_BV + h - A_Q_HEADS, HEAD_AV + h // A_GROUP)
    return pl.pallas_call(
        _attn_ctx_kernel,
        grid=(BATCH, N_CTX_HEADS),
        in_specs=[
            pl.BlockSpec(memory_space=pltpu.SMEM),
            pl.BlockSpec(ctx_block, lambda b, h: (q_head(h), ctx_row0 + b, 0)),
            pl.BlockSpec(ctx_block, lambda b, h: (k_head(h), ctx_row0 + b, 0)),
            pl.BlockSpec(ctx_block, lambda b, h: (v_head(h), ctx_row0 + b, 0)),
        ],
        out_specs=pl.BlockSpec((CTX_LEN, HEAD_DIM), lambda b, h: (b, h)),
        out_shape=jax.ShapeDtypeStruct((N_CTX, N_CTX_HEADS * HEAD_DIM), BF16),
        compiler_params=pltpu.CompilerParams(dimension_semantics=("arbitrary", "arbitrary")),
        name="attn_context",
    )(sinks, qkv, qkv, qkv)


CONV_ROWS = 32
SEQ_TILES = SEQ // TC


def _glu(u):
    return u[:, :C_CHANNELS] * jax.nn.sigmoid(u[:, C_CHANNELS:])


def _conv_kernel(prev_ref, cur_ref, next_ref, w_ref, b_ref, g_ref, beta_ref, o_ref, ext_ref):
    i = pl.program_id(0)
    n_lat_tiles = BATCH * SEQ_TILES
    first = jnp.logical_or(i % SEQ_TILES == 0, i >= n_lat_tiles)
    last = jnp.logical_or(i % SEQ_TILES == SEQ_TILES - 1, i >= n_lat_tiles)
    ext_ref[0:CONV_HALO, :] = jnp.where(first, 0.0, _glu(prev_ref[...]))
    ext_ref[CONV_HALO:CONV_HALO + TC, :] = _glu(cur_ref[...])
    ext_ref[CONV_HALO + TC:, :] = jnp.where(last, 0.0, _glu(next_ref[...]))
    for c in range(TC // CONV_ROWS):
        r0 = c * CONV_ROWS
        acc = jnp.zeros((CONV_ROWS, C_CHANNELS), F32)
        for k in range(C_CONV_WIDTH):
            lo = r0 + CONV_HALO - C_PAD + k
            acc = acc + ext_ref[lo:lo + CONV_ROWS, :] * w_ref[k:k + 1, :]
        acc = acc + b_ref[...]
        mu = jnp.mean(acc, axis=-1, keepdims=True)
        xc = acc - mu
        y = xc * lax.rsqrt(jnp.mean(xc * xc, axis=-1, keepdims=True) + NORM_EPS)
        y = y * g_ref[...] + beta_ref[...]
        o_ref[r0:r0 + CONV_ROWS, :] = (y * jax.nn.sigmoid(y)).astype(BF16)


def _conv(cu, dw_w, dw_b, ln_g, ln_b):
    halo_per_tile = TC // CONV_HALO
    n_halo = NT // CONV_HALO
    vec = pl.BlockSpec((1, C_CHANNELS), lambda i: (0, 0))
    return pl.pallas_call(
        _conv_kernel,
        grid=(NT // TC,),
        in_specs=[
            pl.BlockSpec((CONV_HALO, 2 * C_CHANNELS), lambda i: (jnp.maximum(i * halo_per_tile - 1, 0), 0)),
            pl.BlockSpec((TC, 2 * C_CHANNELS), lambda i: (i, 0)),
            pl.BlockSpec((CONV_HALO, 2 * C_CHANNELS),
                         lambda i: (jnp.minimum((i + 1) * halo_per_tile, n_halo - 1), 0)),
            pl.BlockSpec((C_CONV_WIDTH, C_CHANNELS), lambda i: (0, 0)),
            vec, vec, vec,
        ],
        out_specs=pl.BlockSpec((TC, C_CHANNELS), lambda i: (i, 0)),
        out_shape=jax.ShapeDtypeStruct((NT, C_CHANNELS), BF16),
        scratch_shapes=[pltpu.VMEM((TC + 2 * CONV_HALO, C_CHANNELS), F32)],
        compiler_params=pltpu.CompilerParams(dimension_semantics=("arbitrary",)),
        name="conv_module",
    )(cu, cu, cu, dw_w, dw_b.reshape(1, -1), ln_g.reshape(1, -1), ln_b.reshape(1, -1))


def _outproj_kernel(widths, x_ref, mod_ref, w_ref, *refs):
    pieces, o_ref = refs[:-1], refs[-1]
    y = None
    k0 = 0
    for piece, width in zip(pieces, widths):
        t = jnp.dot(piece[...], w_ref[k0:k0 + width, :], preferred_element_type=F32)
        y = t if y is None else y + t
        k0 += width
    o_ref[...] = x_ref[...] + mod_ref[2:3, :] * y


def _outproj(xs, mod_l, w_out, pieces, tile0, n_tiles):
    widths = tuple(int(p.shape[1]) for p, _ in pieces)
    assert sum(widths) == MIX_WIDTH
    blocks = 4 * TM * D_MODEL * 4 + 2 * MIX_WIDTH * D_MODEL * 2 + 2 * TM * MIX_WIDTH * 2
    piece_specs = [pl.BlockSpec((TM, w), functools.partial(lambda i, r0: (r0 + i, 0), r0=r0))
                   for w, (_, r0) in zip(widths, pieces)]
    return pl.pallas_call(
        functools.partial(_outproj_kernel, widths),
        grid=(n_tiles,),
        in_specs=[
            pl.BlockSpec((TM, D_MODEL), lambda i: (tile0 + i, 0)),
            pl.BlockSpec((None, None, 3, D_MODEL), lambda i: (_group_of_tile(tile0 + i), 1, 0, 0)),
            pl.BlockSpec((MIX_WIDTH, D_MODEL), lambda i: (0, 0)),
        ] + piece_specs,
        out_specs=pl.BlockSpec((TM, D_MODEL), lambda i: (tile0 + i, 0)),
        out_shape=jax.ShapeDtypeStruct((NT, D_MODEL), F32),
        input_output_aliases={0: 0},
        compiler_params=pltpu.CompilerParams(
            dimension_semantics=("arbitrary",),
            vmem_limit_bytes=_vmem_limit(blocks)),
        name="outproj",
    )(xs, mod_l, w_out, *[p for p, _ in pieces])


def kernel(x, c, ctx, c_ctx, w_mod, b_mod, norm_ffn1, norm_mix, norm_ffn2, ffn1_w_gate, ffn1_w_up,
           ffn1_w_down, ffn2_w_gate, ffn2_w_up, ffn2_w_down, w_in, w_out, a_q_norm, a_k_norm, a_sink,
           b_q_norm, b_k_norm, b_rpb, c_dw_w, c_dw_b, c_ln_g, c_ln_b):
    xs = jnp.concatenate([x.reshape(N_LAT, D_MODEL), ctx.reshape(N_CTX, D_MODEL)], axis=0)
    mods = _mod_vectors(c, c_ctx, w_mod, b_mod)
    cos_t, sin_t = _rope_tables()
    q_scale = HEAD_DIM ** -0.5
    for l in range(DEPTH):
        last = l == DEPTH - 1
        mod_l = mods[l]
        xs = _ffn(xs, mod_l, 0, norm_ffn1[l], ffn1_w_gate[l].astype(BF16), ffn1_w_up[l].astype(BF16),
                  ffn1_w_down[l].astype(BF16), ALL_TILES)
        head_gains = jnp.concatenate([
            (a_q_norm[l] * q_scale)[None], a_k_norm[l][None], (b_q_norm[l] * q_scale)[None],
            b_k_norm[l][None], jnp.zeros((4, HEAD_DIM), F32)], axis=0)
        qkv, cu = _inproj(xs, mod_l, norm_mix[l], w_in[l].astype(BF16), head_gains, cos_t, sin_t)
        o_a = _attn_a(qkv, a_sink[l])
        o_b = _attn_b(qkv, _b_bias_table(b_rpb[l]))
        o_c = _conv(cu, c_dw_w[l], c_dw_b[l], c_ln_g[l], c_ln_b[l])
        w_out_l = w_out[l].astype(BF16)
        xs = _outproj(xs, mod_l, w_out_l, [(o_a, 0), (o_b, 0), (o_c, 0)], 0, LAT_TILES)
        if not last:
            sinks = jnp.concatenate([a_sink[l], jnp.full((B_HEADS,), NEG_INF, F32)])
            o_ctx = _attn_ctx(qkv, sinks)
            xs = _outproj(xs, mod_l, w_out_l, [(o_ctx, 0), (o_c, LAT_TILES)], LAT_TILES, 1)
        xs = _ffn(xs, mod_l, 2, norm_ffn2[l], ffn2_w_gate[l].astype(BF16), ffn2_w_up[l].astype(BF16),
                  ffn2_w_down[l].astype(BF16), LAT_TILES if last else ALL_TILES)
    return xs[:N_LAT].reshape(BATCH, SEQ, D_MODEL)
```

```python
import functools

import numpy as np
import jax
import jax.numpy as jnp
from jax import lax
from jax.experimental import pallas as pl
from jax.experimental.pallas import tpu as pltpu

D_MODEL = 2048
BATCH = 2
SEQ = 16384
DEPTH = 2
GRID_W = 64
GRID_ROWS = SEQ // GRID_W
CTX_LEN = 256
HEAD_DIM = 128
A_Q_HEADS = 6
A_KV_HEADS = 2
A_GROUP = A_Q_HEADS // A_KV_HEADS
A_WINDOW = 128
B_HEADS = 6
B_WIN_H = 8
B_WIN_W = 16
C_CHANNELS = 512
C_CONV_WIDTH = 31
C_PAD = (C_CONV_WIDTH - 1) // 2
D_FF = 5632
ROPE_THETA = 10000.0
NORM_EPS = 1e-6
NEG_INF = -1e30
N_MOD = 9
IN_COLS = 4608
MIX_WIDTH = 2048

HEAD_AQ = 0
HEAD_AK = HEAD_AQ + A_Q_HEADS
HEAD_AV = HEAD_AK + A_KV_HEADS
HEAD_BQ = HEAD_AV + A_KV_HEADS
HEAD_BK = HEAD_BQ + B_HEADS
HEAD_BV = HEAD_BK + B_HEADS
N_QKV_HEADS = HEAD_BV + B_HEADS
C_COL0 = N_QKV_HEADS * HEAD_DIM

N_LAT = BATCH * SEQ
N_CTX = BATCH * CTX_LEN
NT = N_LAT + N_CTX

V7X_VMEM_BYTES = 64 * 1024 * 1024
SUBLANES = 8

TM = 512
LAT_TILES = N_LAT // TM
TILES_PER_BATCH = SEQ // TM
ALL_TILES = NT // TM
TF = 512
TC = 256
A_TQ = 256
A_TK = A_TQ + 2 * A_WINDOW
CONV_HALO = 16

F32 = jnp.float32
BF16 = jnp.bfloat16


def _vmem_limit(block_bytes):
    return int(min(V7X_VMEM_BYTES - 4 * 1024 * 1024, block_bytes + 12 * 1024 * 1024))


def _group_of_tile(i):
    return i // TILES_PER_BATCH


def _modulated(x, gain, shift, scale):
    ms = jnp.mean(x * x, axis=-1, keepdims=True)
    return (x * lax.rsqrt(ms + NORM_EPS) * gain) * (1.0 + scale) + shift


MOD_TN = 1024


def _mod_kernel(c_ref, w_ref, b_ref, o_ref):
    c = c_ref[...]
    a = c * jax.nn.sigmoid(c)
    o_ref[...] = jnp.dot(a, w_ref[...], preferred_element_type=F32,
                         precision=lax.Precision.HIGHEST) + b_ref[...]


def _mod_vectors(c, c_ctx, w_mod, b_mod):
    rows = jnp.concatenate([c, c_ctx[None, :], jnp.zeros((SUBLANES - BATCH - 1, D_MODEL), F32)], axis=0)
    n = N_MOD * D_MODEL
    out = pl.pallas_call(
        _mod_kernel,
        grid=(DEPTH, n // MOD_TN),
        in_specs=[
            pl.BlockSpec((SUBLANES, D_MODEL), lambda l, j: (0, 0)),
            pl.BlockSpec((None, D_MODEL, MOD_TN), lambda l, j: (l, 0, j)),
            pl.BlockSpec((None, 1, MOD_TN), lambda l, j: (l, 0, j)),
        ],
        out_specs=pl.BlockSpec((None, SUBLANES, MOD_TN), lambda l, j: (l, 0, j)),
        out_shape=jax.ShapeDtypeStruct((DEPTH, SUBLANES, n), F32),
        compiler_params=pltpu.CompilerParams(
            dimension_semantics=("arbitrary", "arbitrary"),
            vmem_limit_bytes=_vmem_limit(2 * D_MODEL * MOD_TN * 4)),
        name="mod_vectors",
    )(rows, w_mod, b_mod.reshape(DEPTH, 1, n))
    return out[:, :3].reshape(DEPTH, 3, 3, 3, D_MODEL)


def _ffn_kernel(n_x, *refs):
    x_refs = refs[:n_x]
    mod_ref, gain_ref, wg_ref, wu_ref, wd_ref, o_ref, h_ref, acc_ref = refs[n_x:]
    i = pl.program_id(0)
    j = pl.program_id(1)
    if n_x == 1:
        sources = [(x_refs[0], None)]
    else:
        sources = [(x_refs[0], i < LAT_TILES), (x_refs[1], i >= LAT_TILES)]

    def when(cond, extra):
        return pl.when(cond if extra is None else jnp.logical_and(cond, extra))

    for x_ref, is_source in sources:
        @when(j == 0, is_source)
        def _(x_ref=x_ref):
            h = _modulated(x_ref[...], gain_ref[...], mod_ref[0:1, :], mod_ref[1:2, :])
            h_ref[...] = h.astype(BF16)
            acc_ref[...] = jnp.zeros_like(acc_ref)

    h = h_ref[...]
    g = jnp.dot(h, wg_ref[...], preferred_element_type=F32)
    u = jnp.dot(h, wu_ref[...], preferred_element_type=F32)
    a = (g * jax.nn.sigmoid(g)) * u
    acc_ref[...] += jnp.dot(a.astype(BF16), wd_ref[...], preferred_element_type=F32)

    for x_ref, is_source in sources:
        @when(j == pl.num_programs(1) - 1, is_source)
        def _(x_ref=x_ref):
            o_ref[...] = x_ref[...] + (0.5 * mod_ref[2:3, :]) * acc_ref[...]


def _ffn(x_parts, mod_l, sub, gain, wg, wu, wd, layer, n_tiles):
    blocks = 4 * TM * D_MODEL * 4 + 6 * D_MODEL * TF * 2 + TM * D_MODEL * (2 + 4)
    if len(x_parts) == 1:
        x_specs = [pl.BlockSpec((TM, D_MODEL), lambda i, j: (i, 0))]
    else:
        blocks += 2 * TM * D_MODEL * 4
        x_specs = [pl.BlockSpec((TM, D_MODEL), lambda i, j: (jnp.minimum(i, LAT_TILES - 1), 0)),
                   pl.BlockSpec((TM, D_MODEL), lambda i, j: (0, 0))]
    in_place = len(x_parts) == 1 and x_parts[0].shape[0] == n_tiles * TM
    return pl.pallas_call(
        functools.partial(_ffn_kernel, len(x_parts)),
        grid=(n_tiles, D_FF // TF),
        in_specs=x_specs + [
            pl.BlockSpec((None, None, 3, D_MODEL), lambda i, j: (_group_of_tile(i), sub, 0, 0)),
            pl.BlockSpec((1, D_MODEL), lambda i, j: (0, 0)),
            pl.BlockSpec((None, D_MODEL, TF), lambda i, j: (layer, 0, j)),
            pl.BlockSpec((None, D_MODEL, TF), lambda i, j: (layer, 0, j)),
            pl.BlockSpec((None, TF, D_MODEL), lambda i, j: (layer, j, 0)),
        ],
        out_specs=pl.BlockSpec((TM, D_MODEL), lambda i, j: (i, 0)),
        out_shape=jax.ShapeDtypeStruct((n_tiles * TM, D_MODEL), F32),
        scratch_shapes=[pltpu.VMEM((TM, D_MODEL), BF16), pltpu.VMEM((TM, D_MODEL), F32)],
        input_output_aliases={0: 0} if in_place else {},
        compiler_params=pltpu.CompilerParams(
            dimension_semantics=("arbitrary", "arbitrary"),
            vmem_limit_bytes=_vmem_limit(blocks)),
        name="ffn",
    )(*x_parts, mod_l, gain.reshape(1, D_MODEL), wg, wu, wd)


GAIN_AQ, GAIN_AK, GAIN_BQ, GAIN_BK = 0, 1, 2, 3


def _head_kind(h):
    if h < HEAD_AK:
        return GAIN_AQ, True
    if h < HEAD_AV:
        return GAIN_AK, True
    if h < HEAD_BQ:
        return None, False
    if h < HEAD_BK:
        return GAIN_BQ, False
    if h < HEAD_BV:
        return GAIN_BK, False
    return None, False


def _inproj_kernel(x_ref, mod_ref, gain_ref, w_ref, hg_ref, cos_ref, sin_ref, qkv_ref, cu_ref, h_ref):
    h = _modulated(x_ref[...], gain_ref[...], mod_ref[0:1, :], mod_ref[1:2, :])
    h_ref[...] = h.astype(BF16)
    lane = lax.broadcasted_iota(jnp.int32, (TM, HEAD_DIM), 1)
    even_quarter = ((lane // (HEAD_DIM // 4)) % 2) == 0
    for pair in range(N_QKV_HEADS // 2):
        c0 = pair * 2 * HEAD_DIM
        y2 = jnp.dot(h_ref[...], w_ref[:, c0:c0 + 2 * HEAD_DIM], preferred_element_type=F32)
        for half in range(2):
            hd = 2 * pair + half
            y = y2[:, half * HEAD_DIM:(half + 1) * HEAD_DIM]
            gain_row, rotary = _head_kind(hd)
            if gain_row is not None:
                ms = jnp.mean(y * y, axis=-1, keepdims=True)
                y = y * lax.rsqrt(ms + NORM_EPS) * hg_ref[gain_row:gain_row + 1, :]
            if rotary:
                swapped = jnp.where(even_quarter,
                                    pltpu.roll(y, HEAD_DIM - HEAD_DIM // 4, 1),
                                    pltpu.roll(y, HEAD_DIM // 4, 1))
                y = y * cos_ref[...] + swapped * sin_ref[...]
            qkv_ref[hd] = y.astype(BF16)
    for blk in range((IN_COLS - C_COL0) // (2 * HEAD_DIM)):
        c0 = C_COL0 + blk * 2 * HEAD_DIM
        cu_ref[:, blk * 2 * HEAD_DIM:(blk + 1) * 2 * HEAD_DIM] = jnp.dot(
            h_ref[...], w_ref[:, c0:c0 + 2 * HEAD_DIM], preferred_element_type=F32)


def _inproj(xs, mod_l, gain, w_in, layer, head_gains, cos_t, sin_t):
    n_cu = IN_COLS - C_COL0
    blocks = (2 * TM * D_MODEL * 4 + D_MODEL * IN_COLS * 2 + 2 * N_QKV_HEADS * TM * HEAD_DIM * 2
              + 2 * TM * n_cu * 4 + 4 * TM * HEAD_DIM * 4 + TM * D_MODEL * 2)
    rope_block = lambda i: (jnp.where(i < LAT_TILES, i % TILES_PER_BATCH, TILES_PER_BATCH), 0)
    return pl.pallas_call(
        _inproj_kernel,
        grid=(ALL_TILES,),
        in_specs=[
            pl.BlockSpec((TM, D_MODEL), lambda i: (i, 0)),
            pl.BlockSpec((None, None, 3, D_MODEL), lambda i: (_group_of_tile(i), 1, 0, 0)),
            pl.BlockSpec((1, D_MODEL), lambda i: (0, 0)),
            pl.BlockSpec((None, D_MODEL, IN_COLS), lambda i: (layer, 0, 0), pipeline_mode=pl.Buffered(1)),
            pl.BlockSpec((SUBLANES, HEAD_DIM), lambda i: (0, 0)),
            pl.BlockSpec((TM, HEAD_DIM), rope_block),
            pl.BlockSpec((TM, HEAD_DIM), rope_block),
        ],
        out_specs=[
            pl.BlockSpec((N_QKV_HEADS, TM, HEAD_DIM), lambda i: (0, i, 0)),
            pl.BlockSpec((TM, n_cu), lambda i: (i, 0)),
        ],
        out_shape=[
            jax.ShapeDtypeStruct((N_QKV_HEADS, NT, HEAD_DIM), BF16),
            jax.ShapeDtypeStruct((NT, n_cu), F32),
        ],
        scratch_shapes=[pltpu.VMEM((TM, D_MODEL), BF16)],
        compiler_params=pltpu.CompilerParams(
            dimension_semantics=("arbitrary",),
            vmem_limit_bytes=_vmem_limit(blocks)),
        name="inproj",
    )(xs, mod_l, gain.reshape(1, D_MODEL), w_in, head_gains, cos_t, sin_t)


def _rope_tables():
    t = np.arange(SEQ)
    n_freq = HEAD_DIM // 4
    inv_freq = ROPE_THETA ** (-np.arange(n_freq, dtype=np.float64) / n_freq)
    ang_r = (t // GRID_W)[:, None] * inv_freq[None, :]
    ang_c = (t % GRID_W)[:, None] * inv_freq[None, :]
    cos_t = np.concatenate([np.cos(ang_r), np.cos(ang_r), np.cos(ang_c), np.cos(ang_c)], axis=-1)
    sin_t = np.concatenate([-np.sin(ang_r), np.sin(ang_r), -np.sin(ang_c), np.sin(ang_c)], axis=-1)
    cos_t = np.concatenate([cos_t, np.ones((TM, HEAD_DIM))], axis=0)
    sin_t = np.concatenate([sin_t, np.zeros((TM, HEAD_DIM))], axis=0)
    return jnp.asarray(cos_t, F32), jnp.asarray(sin_t, F32)


A_BLOCKS = SEQ // A_TQ
A_BAND_OFFSETS = (-A_WINDOW, 0, -2 * A_WINDOW)


def _dot_nt(a, b):
    return lax.dot_general(a, b, (((1,), (1,)), ((), ())), preferred_element_type=F32)


def _a_band_table():
    rel = np.arange(A_TK)[None, :] - np.arange(A_TQ)[:, None]
    return jnp.asarray(np.stack([np.where(np.abs(rel + off) <= A_WINDOW, 0.0, NEG_INF)
                                 for off in A_BAND_OFFSETS]), F32)


def _attn_a_kernel(sink_ref, q_ref, k_ref, v_ref, kc_ref, vc_ref, band_ref, o_ref):
    head = pl.program_id(1) * A_GROUP + pl.program_id(2)
    sink = sink_ref[head]
    kc = kc_ref[...]
    vc = vc_ref[...]

    def body(i, carry):
        q0 = pl.multiple_of(i * A_TQ, A_TQ)
        k0 = pl.multiple_of(jnp.clip(q0 - A_WINDOW, 0, SEQ - A_TK), A_WINDOW)
        placement = jnp.where(i == 0, 1, jnp.where(i == A_BLOCKS - 1, 2, 0))
        q = q_ref[pl.ds(q0, A_TQ), :]
        k = k_ref[pl.ds(k0, A_TK), :]
        v = v_ref[pl.ds(k0, A_TK), :]
        s = _dot_nt(q, k) + band_ref[placement]
        sc = _dot_nt(q, kc)
        m = jnp.maximum(jnp.maximum(jnp.max(s, axis=-1, keepdims=True),
                                    jnp.max(sc, axis=-1, keepdims=True)), sink)
        p = jnp.exp(s - m)
        pc = jnp.exp(sc - m)
        den = (jnp.sum(p, axis=-1, keepdims=True) + jnp.sum(pc, axis=-1, keepdims=True)
               + jnp.exp(sink - m))
        o = (jnp.dot(p.astype(BF16), v, preferred_element_type=F32)
             + jnp.dot(pc.astype(BF16), vc, preferred_element_type=F32))
        o_ref[pl.ds(q0, A_TQ), :] = (o / den).astype(BF16)
        return carry

    lax.fori_loop(0, A_BLOCKS, body, 0)


def _attn_a(qkv, sink):
    assert A_BLOCKS >= 3 and A_TK <= SEQ
    seq_block = (None, SEQ, HEAD_DIM)
    ctx_block = (None, CTX_LEN, HEAD_DIM)
    ctx_row0 = N_LAT // CTX_LEN
    n_band = len(A_BAND_OFFSETS)
    blocks = 2 * 4 * SEQ * HEAD_DIM * 2 + 4 * CTX_LEN * HEAD_DIM * 2 + 2 * n_band * A_TQ * A_TK * 4
    return pl.pallas_call(
        _attn_a_kernel,
        grid=(BATCH, A_KV_HEADS, A_GROUP),
        in_specs=[
            pl.BlockSpec(memory_space=pltpu.SMEM),
            pl.BlockSpec(seq_block, lambda b, kv, g: (HEAD_AQ + kv * A_GROUP + g, b, 0)),
            pl.BlockSpec(seq_block, lambda b, kv, g: (HEAD_AK + kv, b, 0)),
            pl.BlockSpec(seq_block, lambda b, kv, g: (HEAD_AV + kv, b, 0)),
            pl.BlockSpec(ctx_block, lambda b, kv, g: (HEAD_AK + kv, ctx_row0 + b, 0)),
            pl.BlockSpec(ctx_block, lambda b, kv, g: (HEAD_AV + kv, ctx_row0 + b, 0)),
            pl.BlockSpec((n_band, A_TQ, A_TK), lambda b, kv, g: (0, 0, 0)),
        ],
        out_specs=pl.BlockSpec((SEQ, HEAD_DIM), lambda b, kv, g: (b, kv * A_GROUP + g)),
        out_shape=jax.ShapeDtypeStruct((N_LAT, A_Q_HEADS * HEAD_DIM), BF16),
        compiler_params=pltpu.CompilerParams(
            dimension_semantics=("arbitrary", "arbitrary", "arbitrary"),
            vmem_limit_bytes=_vmem_limit(blocks)),
        name="attn_window",
    )(sink, qkv, qkv, qkv, qkv, qkv, _a_band_table())


B_KEYS = B_WIN_H * GRID_W
B_SHIFTS = B_WIN_H
B_BLOCK_ROWS = 8
B_BLOCK = B_BLOCK_ROWS * GRID_W


def _attn_b_kernel(q_ref, k_ref, v_ref, kc_ref, vc_ref, bias_ref, o_ref):
    kc = kc_ref[...]
    vc = vc_ref[...]

    def body(blk, carry):
        q0 = pl.multiple_of(blk * B_BLOCK, B_BLOCK)
        qb = q_ref[pl.ds(q0, B_BLOCK), :]
        sc_all = _dot_nt(qb, kc)
        o_loc, pcs, dens = [], [], []
        for t in range(B_BLOCK_ROWS):
            r = blk * B_BLOCK_ROWS + t
            rs = jnp.clip(r - B_WIN_H // 2, 0, GRID_ROWS - B_WIN_H)
            shift = rs - r + (B_WIN_H - 1)
            k0 = pl.multiple_of(rs * GRID_W, GRID_W)
            rows = slice(t * GRID_W, (t + 1) * GRID_W)
            s = _dot_nt(qb[rows], k_ref[pl.ds(k0, B_KEYS), :]) + bias_ref[shift]
            sc = sc_all[rows]
            m = jnp.maximum(jnp.max(s, axis=-1, keepdims=True), jnp.max(sc, axis=-1, keepdims=True))
            p = jnp.exp(s - m)
            pc = jnp.exp(sc - m)
            dens.append(jnp.sum(p, axis=-1, keepdims=True) + jnp.sum(pc, axis=-1, keepdims=True))
            pcs.append(pc.astype(BF16))
            o_loc.append(jnp.dot(p.astype(BF16), v_ref[pl.ds(k0, B_KEYS), :], preferred_element_type=F32))
        o = (jnp.concatenate(o_loc, axis=0)
             + jnp.dot(jnp.concatenate(pcs, axis=0), vc, preferred_element_type=F32))
        o_ref[pl.ds(q0, B_BLOCK), :] = (o / jnp.concatenate(dens, axis=0)).astype(BF16)
        return carry

    lax.fori_loop(0, GRID_ROWS // B_BLOCK_ROWS, body, 0)


def _b_bias_tables(rpb):
    j = np.arange(GRID_W)[:, None]
    jk = np.arange(GRID_W)[None, :]
    col_start = np.clip(j - B_WIN_W // 2, 0, GRID_W - B_WIN_W)
    inside = (jk >= col_start) & (jk < col_start + B_WIN_W)
    n_col = 2 * B_WIN_W - 1
    onehot = (inside[:, :, None] & ((jk - j + (B_WIN_W - 1))[:, :, None] == np.arange(n_col))).astype(np.float32)
    t = jnp.einsum('lhac,jkc->lhajk', rpb, jnp.asarray(onehot), precision=lax.Precision.HIGHEST)
    t = t + jnp.asarray(np.where(inside, 0.0, NEG_INF), F32)
    t = jnp.stack([t[:, :, d:d + B_WIN_H] for d in range(B_SHIFTS)], axis=2)
    return t.transpose(0, 1, 2, 4, 3, 5).reshape(DEPTH, B_HEADS, B_SHIFTS, GRID_W, B_KEYS)


def _attn_b(qkv, bias, layer):
    seq_block = (None, SEQ, HEAD_DIM)
    ctx_block = (None, CTX_LEN, HEAD_DIM)
    ctx_row0 = N_LAT // CTX_LEN
    blocks = 2 * 4 * SEQ * HEAD_DIM * 2 + 4 * CTX_LEN * HEAD_DIM * 2 + 2 * B_SHIFTS * GRID_W * B_KEYS * 4
    return pl.pallas_call(
        _attn_b_kernel,
        grid=(BATCH, B_HEADS),
        in_specs=[
            pl.BlockSpec(seq_block, lambda b, h: (HEAD_BQ + h, b, 0)),
            pl.BlockSpec(seq_block, lambda b, h: (HEAD_BK + h, b, 0)),
            pl.BlockSpec(seq_block, lambda b, h: (HEAD_BV + h, b, 0)),
            pl.BlockSpec(ctx_block, lambda b, h: (HEAD_BK + h, ctx_row0 + b, 0)),
            pl.BlockSpec(ctx_block, lambda b, h: (HEAD_BV + h, ctx_row0 + b, 0)),
            pl.BlockSpec((None, None, B_SHIFTS, GRID_W, B_KEYS), lambda b, h: (layer, h, 0, 0, 0)),
        ],
        out_specs=pl.BlockSpec((SEQ, HEAD_DIM), lambda b, h: (b, h)),
        out_shape=jax.ShapeDtypeStruct((N_LAT, B_HEADS * HEAD_DIM), BF16),
        compiler_params=pltpu.CompilerParams(
            dimension_semantics=("arbitrary", "arbitrary"),
            vmem_limit_bytes=_vmem_limit(blocks)),
        name="attn_neighbourhood",
    )(qkv, qkv, qkv, qkv, qkv, bias)


N_CTX_HEADS = A_Q_HEADS + B_HEADS


def _attn_ctx_kernel(sink_ref, q_ref, k_ref, v_ref, o_ref):
    sink = sink_ref[pl.program_id(1)]
    q = q_ref[...]
    s = _dot_nt(q, k_ref[...])
    m = jnp.maximum(jnp.max(s, axis=-1, keepdims=True), sink)
    p = jnp.exp(s - m)
    den = jnp.sum(p, axis=-1, keepdims=True) + jnp.exp(sink - m)
    o = jnp.dot(p.astype(BF16), v_ref[...], preferred_element_type=F32)
    o_ref[...] = (o / den).astype(BF16)


def _attn_ctx(qkv, sinks):
    ctx_block = (None, CTX_LEN, HEAD_DIM)
    ctx_row0 = N_LAT // CTX_LEN
    is_b = lambda h: h >= A_Q_HEADS
    q_head = lambda h: jnp.where(is_b(h), HEAD_BQ + h - A_Q_HEADS, HEAD_AQ + h)
    k_head = lambda h: jnp.where(is_b(h), HEAD_BK + h - A_Q_HEADS, HEAD_AK + h // A_GROUP)
    v_head = lambda h: jnp.where(is_b(h), HEAD_BV + h - A_Q_HEADS, HEAD_AV + h // A_GROUP)
    return pl.pallas_call(
        _attn_ctx_kernel,
        grid=(BATCH, N_CTX_HEADS),
        in_specs=[
            pl.BlockSpec(memory_space=pltpu.SMEM),
            pl.BlockSpec(ctx_block, lambda b, h: (q_head(h), ctx_row0 + b, 0)),
            pl.BlockSpec(ctx_block, lambda b, h: (k_head(h), ctx_row0 + b, 0)),
            pl.BlockSpec(ctx_block, lambda b, h: (v_head(h), ctx_row0 + b, 0)),
        ],
        out_specs=pl.BlockSpec((CTX_LEN, HEAD_DIM), lambda b, h: (b, h)),
        out_shape=jax.ShapeDtypeStruct((N_CTX, N_CTX_HEADS * HEAD_DIM), BF16),
        compiler_params=pltpu.CompilerParams(dimension_semantics=("arbitrary", "arbitrary")),
        name="attn_context",
    )(sinks, qkv, qkv, qkv)


CONV_ROWS = 32
SEQ_TILES = SEQ // TC
CONV_SH_ROWS = TC + SUBLANES * ((C_CONV_WIDTH - 1) // SUBLANES)
assert SUBLANES - 1 + CONV_HALO - C_PAD + CONV_SH_ROWS <= TC + 2 * CONV_HALO


def _glu(u):
    return u[:, :C_CHANNELS] * jax.nn.sigmoid(u[:, C_CHANNELS:])


def _conv_kernel(prev_ref, cur_ref, next_ref, w_ref, b_ref, g_ref, beta_ref, o_ref, ext_ref, sh_ref):
    i = pl.program_id(0)
    n_lat_tiles = BATCH * SEQ_TILES
    first = jnp.logical_or(i % SEQ_TILES == 0, i >= n_lat_tiles)
    last = jnp.logical_or(i % SEQ_TILES == SEQ_TILES - 1, i >= n_lat_tiles)
    ext_ref[0:CONV_HALO, :] = jnp.where(first, 0.0, _glu(prev_ref[...]))
    ext_ref[CONV_HALO:CONV_HALO + TC, :] = _glu(cur_ref[...])
    ext_ref[CONV_HALO + TC:, :] = jnp.where(last, 0.0, _glu(next_ref[...]))
    for b in range(SUBLANES):
        lo = b + CONV_HALO - C_PAD
        sh_ref[b] = ext_ref[lo:lo + CONV_SH_ROWS, :]
    groups = CONV_ROWS // SUBLANES
    for c in range(TC // CONV_ROWS):
        r0 = c * CONV_ROWS
        accs = [jnp.zeros((SUBLANES, C_CHANNELS), F32) for _ in range(groups)]
        for k in range(C_CONV_WIDTH):
            a, b = divmod(k, SUBLANES)
            w_k = w_ref[k]
            for g in range(groups):
                lo = r0 + SUBLANES * (a + g)
                accs[g] = accs[g] + sh_ref[b, lo:lo + SUBLANES, :] * w_k
        acc = jnp.concatenate(accs, axis=0) + b_ref[...]
        mu = jnp.mean(acc, axis=-1, keepdims=True)
        xc = acc - mu
        y = xc * lax.rsqrt(jnp.mean(xc * xc, axis=-1, keepdims=True) + NORM_EPS)
        y = y * g_ref[...] + beta_ref[...]
        o_ref[r0:r0 + CONV_ROWS, :] = (y * jax.nn.sigmoid(y)).astype(BF16)


def _conv(cu, dw_w, dw_b, ln_g, ln_b):
    halo_per_tile = TC // CONV_HALO
    n_halo = NT // CONV_HALO
    vec = pl.BlockSpec((1, C_CHANNELS), lambda i: (0, 0))
    return pl.pallas_call(
        _conv_kernel,
        grid=(NT // TC,),
        in_specs=[
            pl.BlockSpec((CONV_HALO, 2 * C_CHANNELS), lambda i: (jnp.maximum(i * halo_per_tile - 1, 0), 0)),
            pl.BlockSpec((TC, 2 * C_CHANNELS), lambda i: (i, 0)),
            pl.BlockSpec((CONV_HALO, 2 * C_CHANNELS),
                         lambda i: (jnp.minimum((i + 1) * halo_per_tile, n_halo - 1), 0)),
            pl.BlockSpec((C_CONV_WIDTH, SUBLANES, C_CHANNELS), lambda i: (0, 0, 0)),
            vec, vec, vec,
        ],
        out_specs=pl.BlockSpec((TC, C_CHANNELS), lambda i: (i, 0)),
        out_shape=jax.ShapeDtypeStruct((NT, C_CHANNELS), BF16),
        scratch_shapes=[pltpu.VMEM((TC + 2 * CONV_HALO, C_CHANNELS), F32),
                        pltpu.VMEM((SUBLANES, CONV_SH_ROWS, C_CHANNELS), F32)],
        compiler_params=pltpu.CompilerParams(dimension_semantics=("arbitrary",)),
        name="conv_module",
    )(cu, cu, cu, jnp.broadcast_to(dw_w[:, None, :], (C_CONV_WIDTH, SUBLANES, C_CHANNELS)),
      dw_b.reshape(1, -1), ln_g.reshape(1, -1), ln_b.reshape(1, -1))


def _outproj_kernel(widths, x_ref, mod_ref, w_ref, *refs):
    pieces, o_ref = refs[:-1], refs[-1]
    y = None
    k0 = 0
    for piece, width in zip(pieces, widths):
        t = jnp.dot(piece[...], w_ref[k0:k0 + width, :], preferred_element_type=F32)
        y = t if y is None else y + t
        k0 += width
    o_ref[...] = x_ref[...] + mod_ref[2:3, :] * y


def _outproj(xs, mod_l, w_out, layer, pieces, tile0, n_tiles):
    widths = tuple(int(p.shape[1]) for p, _ in pieces)
    assert sum(widths) == MIX_WIDTH
    blocks = 4 * TM * D_MODEL * 4 + 2 * MIX_WIDTH * D_MODEL * 2 + 2 * TM * MIX_WIDTH * 2
    piece_specs = [pl.BlockSpec((TM, w), functools.partial(lambda i, r0: (r0 + i, 0), r0=r0))
                   for w, (_, r0) in zip(widths, pieces)]
    return pl.pallas_call(
        functools.partial(_outproj_kernel, widths),
        grid=(n_tiles,),
        in_specs=[
            pl.BlockSpec((TM, D_MODEL), lambda i: (tile0 + i, 0)),
            pl.BlockSpec((None, None, 3, D_MODEL), lambda i: (_group_of_tile(tile0 + i), 1, 0, 0)),
            pl.BlockSpec((None, MIX_WIDTH, D_MODEL), lambda i: (layer, 0, 0)),
        ] + piece_specs,
        out_specs=pl.BlockSpec((TM, D_MODEL), lambda i: (tile0 + i, 0)),
        out_shape=jax.ShapeDtypeStruct((NT, D_MODEL), F32),
        input_output_aliases={0: 0},
        compiler_params=pltpu.CompilerParams(
            dimension_semantics=("arbitrary",),
            vmem_limit_bytes=_vmem_limit(blocks)),
        name="outproj",
    )(xs, mod_l, w_out, *[p for p, _ in pieces])


def kernel(x, c, ctx, c_ctx, w_mod, b_mod, norm_ffn1, norm_mix, norm_ffn2, ffn1_w_gate, ffn1_w_up,
           ffn1_w_down, ffn2_w_gate, ffn2_w_up, ffn2_w_down, w_in, w_out, a_q_norm, a_k_norm, a_sink,
           b_q_norm, b_k_norm, b_rpb, c_dw_w, c_dw_b, c_ln_g, c_ln_b):
    mods = _mod_vectors(c, c_ctx, w_mod, b_mod)
    cos_t, sin_t = _rope_tables()
    b_bias = _b_bias_tables(b_rpb)
    ffn1_w = [w.astype(BF16) for w in (ffn1_w_gate, ffn1_w_up, ffn1_w_down)]
    ffn2_w = [w.astype(BF16) for w in (ffn2_w_gate, ffn2_w_up, ffn2_w_down)]
    w_in_bf = w_in.astype(BF16)
    w_out_bf = w_out.astype(BF16)
    q_scale = HEAD_DIM ** -0.5
    x_parts = [x.reshape(N_LAT, D_MODEL), ctx.reshape(N_CTX, D_MODEL)]
    for l in range(DEPTH):
        last = l == DEPTH - 1
        mod_l = mods[l]
        xs = _ffn(x_parts, mod_l, 0, norm_ffn1[l], *ffn1_w, l, ALL_TILES)
        head_gains = jnp.concatenate([
            (a_q_norm[l] * q_scale)[None], a_k_norm[l][None], (b_q_norm[l] * q_scale)[None],
            b_k_norm[l][None], jnp.zeros((SUBLANES - 4, HEAD_DIM), F32)], axis=0)
        qkv, cu = _inproj(xs, mod_l, norm_mix[l], w_in_bf, l, head_gains, cos_t, sin_t)
        o_a = _attn_a(qkv, a_sink[l])
        o_b = _attn_b(qkv, b_bias, l)
        o_c = _conv(cu, c_dw_w[l], c_dw_b[l], c_ln_g[l], c_ln_b[l])
        xs = _outproj(xs, mod_l, w_out_bf, l, [(o_a, 0), (o_b, 0), (o_c, 0)], 0, LAT_TILES)
        if not last:
            sinks = jnp.concatenate([a_sink[l], jnp.full((B_HEADS,), NEG_INF, F32)])
            o_ctx = _attn_ctx(qkv, sinks)
            xs = _outproj(xs, mod_l, w_out_bf, l, [(o_ctx, 0), (o_c, LAT_TILES)], LAT_TILES, 1)
        xs = _ffn([xs], mod_l, 2, norm_ffn2[l], *ffn2_w, l, LAT_TILES if last else ALL_TILES)
        x_parts = [xs]
    return xs.reshape(BATCH, SEQ, D_MODEL)
```

```python
import functools

import numpy as np
import jax
import jax.numpy as jnp
from jax import lax
from jax.experimental import pallas as pl
from jax.experimental.pallas import tpu as pltpu

D_MODEL = 2048
BATCH = 2
SEQ = 16384
DEPTH = 2
GRID_W = 64
GRID_ROWS = SEQ // GRID_W
CTX_LEN = 256
HEAD_DIM = 128
A_Q_HEADS = 6
A_KV_HEADS = 2
A_GROUP = A_Q_HEADS // A_KV_HEADS
A_WINDOW = 128
B_HEADS = 6
B_WIN_H = 8
B_WIN_W = 16
C_CHANNELS = 512
C_CONV_WIDTH = 31
C_PAD = (C_CONV_WIDTH - 1) // 2
D_FF = 5632
ROPE_THETA = 10000.0
NORM_EPS = 1e-6
NEG_INF = -1e30
N_MOD = 9
IN_COLS = 4608
MIX_WIDTH = 2048

HEAD_AQ = 0
HEAD_AK = HEAD_AQ + A_Q_HEADS
HEAD_AV = HEAD_AK + A_KV_HEADS
HEAD_BQ = HEAD_AV + A_KV_HEADS
HEAD_BK = HEAD_BQ + B_HEADS
HEAD_BV = HEAD_BK + B_HEADS
N_QKV_HEADS = HEAD_BV + B_HEADS
C_COL0 = N_QKV_HEADS * HEAD_DIM

N_LAT = BATCH * SEQ
N_CTX = BATCH * CTX_LEN
NT = N_LAT + N_CTX

V7X_VMEM_BYTES = 64 * 1024 * 1024
SUBLANES = 8

TM = 512
LAT_TILES = N_LAT // TM
TILES_PER_BATCH = SEQ // TM
ALL_TILES = NT // TM
TF = 512
TC = 256
CONV_HALO = 16

F32 = jnp.float32
BF16 = jnp.bfloat16


def _vmem_limit(block_bytes):
    return int(min(V7X_VMEM_BYTES - 4 * 1024 * 1024, block_bytes + 12 * 1024 * 1024))


def _group_of_tile(i):
    return i // TILES_PER_BATCH


def _modulated(x, gain, shift, scale):
    ms = jnp.mean(x * x, axis=-1, keepdims=True)
    return (x * lax.rsqrt(ms + NORM_EPS) * gain) * (1.0 + scale) + shift


MOD_TN = 1024


def _mod_kernel(c_ref, w_ref, b_ref, o_ref):
    c = c_ref[...]
    a = c * jax.nn.sigmoid(c)
    o_ref[...] = jnp.dot(a, w_ref[...], preferred_element_type=F32,
                         precision=lax.Precision.HIGHEST) + b_ref[...]


def _mod_vectors(c, c_ctx, w_mod, b_mod):
    rows = jnp.concatenate([c, c_ctx[None, :], jnp.zeros((SUBLANES - BATCH - 1, D_MODEL), F32)], axis=0)
    n = N_MOD * D_MODEL
    out = pl.pallas_call(
        _mod_kernel,
        grid=(DEPTH, n // MOD_TN),
        in_specs=[
            pl.BlockSpec((SUBLANES, D_MODEL), lambda l, j: (0, 0)),
            pl.BlockSpec((None, D_MODEL, MOD_TN), lambda l, j: (l, 0, j)),
            pl.BlockSpec((None, 1, MOD_TN), lambda l, j: (l, 0, j)),
        ],
        out_specs=pl.BlockSpec((None, SUBLANES, MOD_TN), lambda l, j: (l, 0, j)),
        out_shape=jax.ShapeDtypeStruct((DEPTH, SUBLANES, n), F32),
        compiler_params=pltpu.CompilerParams(
            dimension_semantics=("arbitrary", "arbitrary"),
            vmem_limit_bytes=_vmem_limit(2 * D_MODEL * MOD_TN * 4)),
        name="mod_vectors",
    )(rows, w_mod, b_mod.reshape(DEPTH, 1, n))
    return out[:, :3].reshape(DEPTH, 3, 3, 3, D_MODEL)


def _ffn_kernel(n_x, *refs):
    x_refs = refs[:n_x]
    mod_ref, gain_ref, wg_ref, wu_ref, wd_ref, o_ref, h_ref, acc_ref = refs[n_x:]
    i = pl.program_id(0)
    j = pl.program_id(1)
    if n_x == 1:
        sources = [(x_refs[0], None)]
    else:
        sources = [(x_refs[0], i < LAT_TILES), (x_refs[1], i >= LAT_TILES)]

    def when(cond, extra):
        return pl.when(cond if extra is None else jnp.logical_and(cond, extra))

    for x_ref, is_source in sources:
        @when(j == 0, is_source)
        def _(x_ref=x_ref):
            h = _modulated(x_ref[...], gain_ref[...], mod_ref[0:1, :], mod_ref[1:2, :])
            h_ref[...] = h.astype(BF16)
            acc_ref[...] = jnp.zeros_like(acc_ref)

    h = h_ref[...]
    g = jnp.dot(h, wg_ref[...], preferred_element_type=F32)
    u = jnp.dot(h, wu_ref[...], preferred_element_type=F32)
    a = (g * jax.nn.sigmoid(g)) * u
    acc_ref[...] += jnp.dot(a.astype(BF16), wd_ref[...], preferred_element_type=F32)

    for x_ref, is_source in sources:
        @when(j == pl.num_programs(1) - 1, is_source)
        def _(x_ref=x_ref):
            o_ref[...] = x_ref[...] + (0.5 * mod_ref[2:3, :]) * acc_ref[...]


def _ffn(x_parts, mod_l, sub, gain, wg, wu, wd, layer, n_tiles):
    blocks = 4 * TM * D_MODEL * 4 + 6 * D_MODEL * TF * 2 + TM * D_MODEL * (2 + 4)
    if len(x_parts) == 1:
        x_specs = [pl.BlockSpec((TM, D_MODEL), lambda i, j: (i, 0))]
    else:
        blocks += 2 * TM * D_MODEL * 4
        x_specs = [pl.BlockSpec((TM, D_MODEL), lambda i, j: (jnp.minimum(i, LAT_TILES - 1), 0)),
                   pl.BlockSpec((TM, D_MODEL), lambda i, j: (0, 0))]
    in_place = len(x_parts) == 1 and x_parts[0].shape[0] == n_tiles * TM
    return pl.pallas_call(
        functools.partial(_ffn_kernel, len(x_parts)),
        grid=(n_tiles, D_FF // TF),
        in_specs=x_specs + [
            pl.BlockSpec((None, None, 3, D_MODEL), lambda i, j: (_group_of_tile(i), sub, 0, 0)),
            pl.BlockSpec((1, D_MODEL), lambda i, j: (0, 0)),
            pl.BlockSpec((None, D_MODEL, TF), lambda i, j: (layer, 0, j)),
            pl.BlockSpec((None, D_MODEL, TF), lambda i, j: (layer, 0, j)),
            pl.BlockSpec((None, TF, D_MODEL), lambda i, j: (layer, j, 0)),
        ],
        out_specs=pl.BlockSpec((TM, D_MODEL), lambda i, j: (i, 0)),
        out_shape=jax.ShapeDtypeStruct((n_tiles * TM, D_MODEL), F32),
        scratch_shapes=[pltpu.VMEM((TM, D_MODEL), BF16), pltpu.VMEM((TM, D_MODEL), F32)],
        input_output_aliases={0: 0} if in_place else {},
        compiler_params=pltpu.CompilerParams(
            dimension_semantics=("arbitrary", "arbitrary"),
            vmem_limit_bytes=_vmem_limit(blocks)),
        name="ffn",
    )(*x_parts, mod_l, gain.reshape(1, D_MODEL), wg, wu, wd)


GAIN_AQ, GAIN_AK, GAIN_BQ, GAIN_BK = 0, 1, 2, 3


def _head_kind(h):
    if h < HEAD_AK:
        return GAIN_AQ, True
    if h < HEAD_AV:
        return GAIN_AK, True
    if h < HEAD_BQ:
        return None, False
    if h < HEAD_BK:
        return GAIN_BQ, False
    if h < HEAD_BV:
        return GAIN_BK, False
    return None, False


def _inproj_kernel(x_ref, mod_ref, gain_ref, w_ref, hg_ref, cos_ref, sin_ref, qkv_ref, cu_ref, h_ref):
    h = _modulated(x_ref[...], gain_ref[...], mod_ref[0:1, :], mod_ref[1:2, :])
    h_ref[...] = h.astype(BF16)
    lane = lax.broadcasted_iota(jnp.int32, (TM, HEAD_DIM), 1)
    even_quarter = ((lane // (HEAD_DIM // 4)) % 2) == 0
    for pair in range(N_QKV_HEADS // 2):
        c0 = pair * 2 * HEAD_DIM
        y2 = jnp.dot(h_ref[...], w_ref[:, c0:c0 + 2 * HEAD_DIM], preferred_element_type=F32)
        for half in range(2):
            hd = 2 * pair + half
            y = y2[:, half * HEAD_DIM:(half + 1) * HEAD_DIM]
            gain_row, rotary = _head_kind(hd)
            if gain_row is not None:
                ms = jnp.mean(y * y, axis=-1, keepdims=True)
                y = y * lax.rsqrt(ms + NORM_EPS) * hg_ref[gain_row:gain_row + 1, :]
            if rotary:
                swapped = jnp.where(even_quarter,
                                    pltpu.roll(y, HEAD_DIM - HEAD_DIM // 4, 1),
                                    pltpu.roll(y, HEAD_DIM // 4, 1))
                y = y * cos_ref[...] + swapped * sin_ref[...]
            qkv_ref[hd] = y.astype(BF16)
    for blk in range((IN_COLS - C_COL0) // (2 * HEAD_DIM)):
        c0 = C_COL0 + blk * 2 * HEAD_DIM
        cu_ref[:, blk * 2 * HEAD_DIM:(blk + 1) * 2 * HEAD_DIM] = jnp.dot(
            h_ref[...], w_ref[:, c0:c0 + 2 * HEAD_DIM], preferred_element_type=F32)


def _inproj(xs, mod_l, gain, w_in, layer, head_gains, cos_t, sin_t):
    n_cu = IN_COLS - C_COL0
    blocks = (2 * TM * D_MODEL * 4 + D_MODEL * IN_COLS * 2 + 2 * N_QKV_HEADS * TM * HEAD_DIM * 2
              + 2 * TM * n_cu * 4 + 4 * TM * HEAD_DIM * 4 + TM * D_MODEL * 2)
    rope_block = lambda i: (jnp.where(i < LAT_TILES, i % TILES_PER_BATCH, TILES_PER_BATCH), 0)
    return pl.pallas_call(
        _inproj_kernel,
        grid=(ALL_TILES,),
        in_specs=[
            pl.BlockSpec((TM, D_MODEL), lambda i: (i, 0)),
            pl.BlockSpec((None, None, 3, D_MODEL), lambda i: (_group_of_tile(i), 1, 0, 0)),
            pl.BlockSpec((1, D_MODEL), lambda i: (0, 0)),
            pl.BlockSpec((None, D_MODEL, IN_COLS), lambda i: (layer, 0, 0), pipeline_mode=pl.Buffered(1)),
            pl.BlockSpec((SUBLANES, HEAD_DIM), lambda i: (0, 0)),
            pl.BlockSpec((TM, HEAD_DIM), rope_block),
            pl.BlockSpec((TM, HEAD_DIM), rope_block),
        ],
        out_specs=[
            pl.BlockSpec((N_QKV_HEADS, TM, HEAD_DIM), lambda i: (0, i, 0)),
            pl.BlockSpec((TM, n_cu), lambda i: (i, 0)),
        ],
        out_shape=[
            jax.ShapeDtypeStruct((N_QKV_HEADS, NT, HEAD_DIM), BF16),
            jax.ShapeDtypeStruct((NT, n_cu), F32),
        ],
        scratch_shapes=[pltpu.VMEM((TM, D_MODEL), BF16)],
        compiler_params=pltpu.CompilerParams(
            dimension_semantics=("arbitrary",),
            vmem_limit_bytes=_vmem_limit(blocks)),
        name="inproj",
    )(xs, mod_l, gain.reshape(1, D_MODEL), w_in, head_gains, cos_t, sin_t)


def _rope_tables():
    t = np.arange(SEQ)
    n_freq = HEAD_DIM // 4
    inv_freq = ROPE_THETA ** (-np.arange(n_freq, dtype=np.float64) / n_freq)
    ang_r = (t // GRID_W)[:, None] * inv_freq[None, :]
    ang_c = (t % GRID_W)[:, None] * inv_freq[None, :]
    cos_t = np.concatenate([np.cos(ang_r), np.cos(ang_r), np.cos(ang_c), np.cos(ang_c)], axis=-1)
    sin_t = np.concatenate([-np.sin(ang_r), np.sin(ang_r), -np.sin(ang_c), np.sin(ang_c)], axis=-1)
    cos_t = np.concatenate([cos_t, np.ones((TM, HEAD_DIM))], axis=0)
    sin_t = np.concatenate([sin_t, np.zeros((TM, HEAD_DIM))], axis=0)
    return jnp.asarray(cos_t, F32), jnp.asarray(sin_t, F32)


ATTN_TQ = 256
ATTN_BLOCKS = SEQ // ATTN_TQ
ATTN_UNROLL = 8
PLACE_INTERIOR, PLACE_FIRST, PLACE_LAST = 0, 1, 2
N_PLACEMENTS = 3
PLACEMENT_BLOCKS = (1, 0, ATTN_BLOCKS - 1)

A_LEAD = A_WINDOW
A_KEYS = ATTN_TQ + 2 * A_WINDOW
B_ROWS_PER_BLOCK = ATTN_TQ // GRID_W
B_UNION_ROWS = B_ROWS_PER_BLOCK + B_WIN_H
B_LEAD = (B_WIN_H // 2) * GRID_W
B_KEYS = B_UNION_ROWS * GRID_W
LANES = 128
assert ATTN_BLOCKS % ATTN_UNROLL == 0 and ATTN_BLOCKS >= 3
assert A_LEAD % LANES == 0 and B_LEAD % LANES == 0 and A_KEYS % LANES == 0 and B_KEYS % LANES == 0


def _dot_nt(a, b):
    return lax.dot_general(a, b, (((1,), (1,)), ((), ())), preferred_element_type=F32)


def _window_start(first_query, lead, n_keys):
    return int(np.clip(first_query - lead, 0, SEQ - n_keys))


def _local_attn_kernel(lead, n_keys, has_sink, *refs):
    if has_sink:
        sink_ref, refs = refs[0], refs[1:]
    q_ref, k_ref, v_ref, kc_ref, vc_ref, bias_ref, o_ref = refs
    sink = sink_ref[pl.program_id(1) * A_GROUP + pl.program_id(2)] if has_sink else None
    kc = kc_ref[...]
    vc = vc_ref[...]

    def scores(i):
        q0 = pl.multiple_of(i * ATTN_TQ, ATTN_TQ)
        k0 = pl.multiple_of(jnp.clip(q0 - lead, 0, SEQ - n_keys), LANES)
        placement = jnp.where(i == 0, PLACE_FIRST, jnp.where(i == ATTN_BLOCKS - 1, PLACE_LAST, PLACE_INTERIOR))
        q = q_ref[pl.ds(q0, ATTN_TQ), :]
        s = _dot_nt(q, k_ref[pl.ds(k0, n_keys), :]) + bias_ref[placement]
        return q0, k0, s, _dot_nt(q, kc)

    def softmax(q0, k0, s, sc):
        m = jnp.maximum(jnp.max(s, axis=-1, keepdims=True), jnp.max(sc, axis=-1, keepdims=True))
        if has_sink:
            m = jnp.maximum(m, sink)
        p = jnp.exp(s - m)
        pc = jnp.exp(sc - m)
        den = jnp.sum(p, axis=-1, keepdims=True) + jnp.sum(pc, axis=-1, keepdims=True)
        if has_sink:
            den = den + jnp.exp(sink - m)
        return q0, k0, p.astype(BF16), pc.astype(BF16), den

    def output(q0, k0, p, pc, den):
        o = (jnp.dot(p, v_ref[pl.ds(k0, n_keys), :], preferred_element_type=F32)
             + jnp.dot(pc, vc, preferred_element_type=F32))
        o_ref[pl.ds(q0, ATTN_TQ), :] = (o / den).astype(BF16)

    def body(it, carry):
        scored, weighted = {}, {}
        for step in range(ATTN_UNROLL + 2):
            if step < ATTN_UNROLL:
                scored[step] = scores(it * ATTN_UNROLL + step)
            if 0 <= step - 2 < ATTN_UNROLL:
                output(*weighted.pop(step - 2))
            if 0 <= step - 1 < ATTN_UNROLL:
                weighted[step - 1] = softmax(*scored.pop(step - 1))
        return carry

    lax.fori_loop(0, ATTN_BLOCKS // ATTN_UNROLL, body, 0)


def _local_attn_vmem(n_keys):
    return 2 * 4 * SEQ * HEAD_DIM * 2 + 4 * CTX_LEN * HEAD_DIM * 2 + 2 * N_PLACEMENTS * ATTN_TQ * n_keys * 4


def _a_band_table():
    tables = []
    for blk in PLACEMENT_BLOCKS:
        q0 = blk * ATTN_TQ
        kpos = _window_start(q0, A_LEAD, A_KEYS) + np.arange(A_KEYS)[None, :]
        qpos = q0 + np.arange(ATTN_TQ)[:, None]
        tables.append(np.where(np.abs(kpos - qpos) <= A_WINDOW, 0.0, NEG_INF))
    return jnp.asarray(np.stack(tables), F32)


def _attn_a(qkv, sink):
    seq_block = (None, SEQ, HEAD_DIM)
    ctx_block = (None, CTX_LEN, HEAD_DIM)
    ctx_row0 = N_LAT // CTX_LEN
    return pl.pallas_call(
        functools.partial(_local_attn_kernel, A_LEAD, A_KEYS, True),
        grid=(BATCH, A_KV_HEADS, A_GROUP),
        in_specs=[
            pl.BlockSpec(memory_space=pltpu.SMEM),
            pl.BlockSpec(seq_block, lambda b, kv, g: (HEAD_AQ + kv * A_GROUP + g, b, 0)),
            pl.BlockSpec(seq_block, lambda b, kv, g: (HEAD_AK + kv, b, 0)),
            pl.BlockSpec(seq_block, lambda b, kv, g: (HEAD_AV + kv, b, 0)),
            pl.BlockSpec(ctx_block, lambda b, kv, g: (HEAD_AK + kv, ctx_row0 + b, 0)),
            pl.BlockSpec(ctx_block, lambda b, kv, g: (HEAD_AV + kv, ctx_row0 + b, 0)),
            pl.BlockSpec((N_PLACEMENTS, ATTN_TQ, A_KEYS), lambda b, kv, g: (0, 0, 0)),
        ],
        out_specs=pl.BlockSpec((SEQ, HEAD_DIM), lambda b, kv, g: (b, kv * A_GROUP + g)),
        out_shape=jax.ShapeDtypeStruct((N_LAT, A_Q_HEADS * HEAD_DIM), BF16),
        compiler_params=pltpu.CompilerParams(
            dimension_semantics=("arbitrary", "arbitrary", "arbitrary"),
            vmem_limit_bytes=_vmem_limit(_local_attn_vmem(A_KEYS))),
        name="attn_window",
    )(sink, qkv, qkv, qkv, qkv, qkv, _a_band_table())


def _b_bias_tables(rpb):
    j = np.arange(GRID_W)[:, None]
    jk = np.arange(GRID_W)[None, :]
    col_start = np.clip(j - B_WIN_W // 2, 0, GRID_W - B_WIN_W)
    inside = (jk >= col_start) & (jk < col_start + B_WIN_W)
    n_col = 2 * B_WIN_W - 1
    onehot = (inside[:, :, None] & ((jk - j + (B_WIN_W - 1))[:, :, None] == np.arange(n_col))).astype(np.float32)
    by_row = jnp.einsum('lhac,jkc->lhajk', rpb, jnp.asarray(onehot), precision=lax.Precision.HIGHEST)
    by_row = by_row + jnp.asarray(np.where(inside, 0.0, NEG_INF), F32)
    placements = []
    for blk in PLACEMENT_BLOCKS:
        r0 = blk * B_ROWS_PER_BLOCK
        u0 = _window_start(r0 * GRID_W, B_LEAD, B_KEYS) // GRID_W
        per_row = []
        for t in range(B_ROWS_PER_BLOCK):
            r = r0 + t
            rs = int(np.clip(r - B_WIN_H // 2, 0, GRID_ROWS - B_WIN_H))
            before = rs - u0
            after = B_UNION_ROWS - B_WIN_H - before
            assert before >= 0 and after >= 0
            bias_row0 = rs - r + (B_WIN_H - 1)
            piece = by_row[:, :, bias_row0:bias_row0 + B_WIN_H]
            per_row.append(jnp.pad(piece, ((0, 0), (0, 0), (before, after), (0, 0), (0, 0)),
                                   constant_values=NEG_INF))
        placements.append(jnp.stack(per_row, axis=2))
    t = jnp.stack(placements, axis=2)
    return t.transpose(0, 1, 2, 3, 5, 4, 6).reshape(DEPTH, B_HEADS, N_PLACEMENTS, ATTN_TQ, B_KEYS)


def _attn_b(qkv, bias, layer):
    seq_block = (None, SEQ, HEAD_DIM)
    ctx_block = (None, CTX_LEN, HEAD_DIM)
    ctx_row0 = N_LAT // CTX_LEN
    return pl.pallas_call(
        functools.partial(_local_attn_kernel, B_LEAD, B_KEYS, False),
        grid=(BATCH, B_HEADS),
        in_specs=[
            pl.BlockSpec(seq_block, lambda b, h: (HEAD_BQ + h, b, 0)),
            pl.BlockSpec(seq_block, lambda b, h: (HEAD_BK + h, b, 0)),
            pl.BlockSpec(seq_block, lambda b, h: (HEAD_BV + h, b, 0)),
            pl.BlockSpec(ctx_block, lambda b, h: (HEAD_BK + h, ctx_row0 + b, 0)),
            pl.BlockSpec(ctx_block, lambda b, h: (HEAD_BV + h, ctx_row0 + b, 0)),
            pl.BlockSpec((None, None, N_PLACEMENTS, ATTN_TQ, B_KEYS), lambda b, h: (layer, h, 0, 0, 0)),
        ],
        out_specs=pl.BlockSpec((SEQ, HEAD_DIM), lambda b, h: (b, h)),
        out_shape=jax.ShapeDtypeStruct((N_LAT, B_HEADS * HEAD_DIM), BF16),
        compiler_params=pltpu.CompilerParams(
            dimension_semantics=("arbitrary", "arbitrary"),
            vmem_limit_bytes=_vmem_limit(_local_attn_vmem(B_KEYS))),
        name="attn_neighbourhood",
    )(qkv, qkv, qkv, qkv, qkv, bias)


N_CTX_HEADS = A_Q_HEADS + B_HEADS


def _attn_ctx_kernel(sink_ref, q_ref, k_ref, v_ref, o_ref):
    sink = sink_ref[pl.program_id(1)]
    q = q_ref[...]
    s = _dot_nt(q, k_ref[...])
    m = jnp.maximum(jnp.max(s, axis=-1, keepdims=True), sink)
    p = jnp.exp(s - m)
    den = jnp.sum(p, axis=-1, keepdims=True) + jnp.exp(sink - m)
    o = jnp.dot(p.astype(BF16), v_ref[...], preferred_element_type=F32)
    o_ref[...] = (o / den).astype(BF16)


def _attn_ctx(qkv, sinks):
    ctx_block = (None, CTX_LEN, HEAD_DIM)
    ctx_row0 = N_LAT // CTX_LEN
    is_b = lambda h: h >= A_Q_HEADS
    q_head = lambda h: jnp.where(is_b(h), HEAD_BQ + h - A_Q_HEADS, HEAD_AQ + h)
    k_head = lambda h: jnp.where(is_b(h), HEAD_BK + h - A_Q_HEADS, HEAD_AK + h // A_GROUP)
    v_head = lambda h: jnp.where(is_b(h), HEAD_BV + h - A_Q_HEADS, HEAD_AV + h // A_GROUP)
    return pl.pallas_call(
        _attn_ctx_kernel,
        grid=(BATCH, N_CTX_HEADS),
        in_specs=[
            pl.BlockSpec(memory_space=pltpu.SMEM),
            pl.BlockSpec(ctx_block, lambda b, h: (q_head(h), ctx_row0 + b, 0)),
            pl.BlockSpec(ctx_block, lambda b, h: (k_head(h), ctx_row0 + b, 0)),
            pl.BlockSpec(ctx_block, lambda b, h: (v_head(h), ctx_row0 + b, 0)),
        ],
        out_specs=pl.BlockSpec((CTX_LEN, HEAD_DIM), lambda b, h: (b, h)),
        out_shape=jax.ShapeDtypeStruct((N_CTX, N_CTX_HEADS * HEAD_DIM), BF16),
        compiler_params=pltpu.CompilerParams(dimension_semantics=("arbitrary", "arbitrary")),
        name="attn_context",
    )(sinks, qkv, qkv, qkv)


CONV_ROWS = 32
SEQ_TILES = SEQ // TC
CONV_SH_ROWS = TC + SUBLANES * ((C_CONV_WIDTH - 1) // SUBLANES)
assert SUBLANES - 1 + CONV_HALO - C_PAD + CONV_SH_ROWS <= TC + 2 * CONV_HALO


def _glu(u):
    return u[:, :C_CHANNELS] * jax.nn.sigmoid(u[:, C_CHANNELS:])


def _conv_kernel(prev_ref, cur_ref, next_ref, w_ref, b_ref, g_ref, beta_ref, o_ref, ext_ref, sh_ref):
    i = pl.program_id(0)
    n_lat_tiles = BATCH * SEQ_TILES
    first = jnp.logical_or(i % SEQ_TILES == 0, i >= n_lat_tiles)
    last = jnp.logical_or(i % SEQ_TILES == SEQ_TILES - 1, i >= n_lat_tiles)
    ext_ref[0:CONV_HALO, :] = jnp.where(first, 0.0, _glu(prev_ref[...]))
    ext_ref[CONV_HALO:CONV_HALO + TC, :] = _glu(cur_ref[...])
    ext_ref[CONV_HALO + TC:, :] = jnp.where(last, 0.0, _glu(next_ref[...]))
    for b in range(SUBLANES):
        lo = b + CONV_HALO - C_PAD
        sh_ref[b] = ext_ref[lo:lo + CONV_SH_ROWS, :]
    groups = CONV_ROWS // SUBLANES
    for c in range(TC // CONV_ROWS):
        r0 = c * CONV_ROWS
        accs = [jnp.zeros((SUBLANES, C_CHANNELS), F32) for _ in range(groups)]
        for k in range(C_CONV_WIDTH):
            a, b = divmod(k, SUBLANES)
            w_k = w_ref[k]
            for g in range(groups):
                lo = r0 + SUBLANES * (a + g)
                accs[g] = accs[g] + sh_ref[b, lo:lo + SUBLANES, :] * w_k
        acc = jnp.concatenate(accs, axis=0) + b_ref[...]
        mu = jnp.mean(acc, axis=-1, keepdims=True)
        xc = acc - mu
        y = xc * lax.rsqrt(jnp.mean(xc * xc, axis=-1, keepdims=True) + NORM_EPS)
        y = y * g_ref[...] + beta_ref[...]
        o_ref[r0:r0 + CONV_ROWS, :] = (y * jax.nn.sigmoid(y)).astype(BF16)


def _conv(cu, dw_w, dw_b, ln_g, ln_b):
    halo_per_tile = TC // CONV_HALO
    n_halo = NT // CONV_HALO
    vec = pl.BlockSpec((1, C_CHANNELS), lambda i: (0, 0))
    return pl.pallas_call(
        _conv_kernel,
        grid=(NT // TC,),
        in_specs=[
            pl.BlockSpec((CONV_HALO, 2 * C_CHANNELS), lambda i: (jnp.maximum(i * halo_per_tile - 1, 0), 0)),
            pl.BlockSpec((TC, 2 * C_CHANNELS), lambda i: (i, 0)),
            pl.BlockSpec((CONV_HALO, 2 * C_CHANNELS),
                         lambda i: (jnp.minimum((i + 1) * halo_per_tile, n_halo - 1), 0)),
            pl.BlockSpec((C_CONV_WIDTH, SUBLANES, C_CHANNELS), lambda i: (0, 0, 0)),
            vec, vec, vec,
        ],
        out_specs=pl.BlockSpec((TC, C_CHANNELS), lambda i: (i, 0)),
        out_shape=jax.ShapeDtypeStruct((NT, C_CHANNELS), BF16),
        scratch_shapes=[pltpu.VMEM((TC + 2 * CONV_HALO, C_CHANNELS), F32),
                        pltpu.VMEM((SUBLANES, CONV_SH_ROWS, C_CHANNELS), F32)],
        compiler_params=pltpu.CompilerParams(dimension_semantics=("arbitrary",)),
        name="conv_module",
    )(cu, cu, cu, jnp.broadcast_to(dw_w[:, None, :], (C_CONV_WIDTH, SUBLANES, C_CHANNELS)),
      dw_b.reshape(1, -1), ln_g.reshape(1, -1), ln_b.reshape(1, -1))


def _outproj_kernel(widths, x_ref, mod_ref, w_ref, *refs):
    pieces, o_ref = refs[:-1], refs[-1]
    y = None
    k0 = 0
    for piece, width in zip(pieces, widths):
        t = jnp.dot(piece[...], w_ref[k0:k0 + width, :], preferred_element_type=F32)
        y = t if y is None else y + t
        k0 += width
    o_ref[...] = x_ref[...] + mod_ref[2:3, :] * y


def _outproj(xs, mod_l, w_out, layer, pieces, tile0, n_tiles):
    widths = tuple(int(p.shape[1]) for p, _ in pieces)
    assert sum(widths) == MIX_WIDTH
    blocks = 4 * TM * D_MODEL * 4 + 2 * MIX_WIDTH * D_MODEL * 2 + 2 * TM * MIX_WIDTH * 2
    piece_specs = [pl.BlockSpec((TM, w), functools.partial(lambda i, r0: (r0 + i, 0), r0=r0))
                   for w, (_, r0) in zip(widths, pieces)]
    return pl.pallas_call(
        functools.partial(_outproj_kernel, widths),
        grid=(n_tiles,),
        in_specs=[
            pl.BlockSpec((TM, D_MODEL), lambda i: (tile0 + i, 0)),
            pl.BlockSpec((None, None, 3, D_MODEL), lambda i: (_group_of_tile(tile0 + i), 1, 0, 0)),
            pl.BlockSpec((None, MIX_WIDTH, D_MODEL), lambda i: (layer, 0, 0)),
        ] + piece_specs,
        out_specs=pl.BlockSpec((TM, D_MODEL), lambda i: (tile0 + i, 0)),
        out_shape=jax.ShapeDtypeStruct((NT, D_MODEL), F32),
        input_output_aliases={0: 0},
        compiler_params=pltpu.CompilerParams(
            dimension_semantics=("arbitrary",),
            vmem_limit_bytes=_vmem_limit(blocks)),
        name="outproj",
    )(xs, mod_l, w_out, *[p for p, _ in pieces])


def kernel(x, c, ctx, c_ctx, w_mod, b_mod, norm_ffn1, norm_mix, norm_ffn2, ffn1_w_gate, ffn1_w_up,
           ffn1_w_down, ffn2_w_gate, ffn2_w_up, ffn2_w_down, w_in, w_out, a_q_norm, a_k_norm, a_sink,
           b_q_norm, b_k_norm, b_rpb, c_dw_w, c_dw_b, c_ln_g, c_ln_b):
    mods = _mod_vectors(c, c_ctx, w_mod, b_mod)
    cos_t, sin_t = _rope_tables()
    b_bias = _b_bias_tables(b_rpb)
    ffn1_w = [w.astype(BF16) for w in (ffn1_w_gate, ffn1_w_up, ffn1_w_down)]
    ffn2_w = [w.astype(BF16) for w in (ffn2_w_gate, ffn2_w_up, ffn2_w_down)]
    w_in_bf = w_in.astype(BF16)
    w_out_bf = w_out.astype(BF16)
    q_scale = HEAD_DIM ** -0.5
    x_parts = [x.reshape(N_LAT, D_MODEL), ctx.reshape(N_CTX, D_MODEL)]
    for l in range(DEPTH):
        last = l == DEPTH - 1
        mod_l = mods[l]
        xs = _ffn(x_parts, mod_l, 0, norm_ffn1[l], *ffn1_w, l, ALL_TILES)
        head_gains = jnp.concatenate([
            (a_q_norm[l] * q_scale)[None], a_k_norm[l][None], (b_q_norm[l] * q_scale)[None],
            b_k_norm[l][None], jnp.zeros((SUBLANES - 4, HEAD_DIM), F32)], axis=0)
        qkv, cu = _inproj(xs, mod_l, norm_mix[l], w_in_bf, l, head_gains, cos_t, sin_t)
        o_a = _attn_a(qkv, a_sink[l])
        o_b = _attn_b(qkv, b_bias, l)
        o_c = _conv(cu, c_dw_w[l], c_dw_b[l], c_ln_g[l], c_ln_b[l])
        xs = _outproj(xs, mod_l, w_out_bf, l, [(o_a, 0), (o_b, 0), (o_c, 0)], 0, LAT_TILES)
        if not last:
            sinks = jnp.concatenate([a_sink[l], jnp.full((B_HEADS,), NEG_INF, F32)])
            o_ctx = _attn_ctx(qkv, sinks)
            xs = _outproj(xs, mod_l, w_out_bf, l, [(o_ctx, 0), (o_c, LAT_TILES)], LAT_TILES, 1)
        xs = _ffn([xs], mod_l, 2, norm_ffn2[l], *ffn2_w, l, LAT_TILES if last else ALL_TILES)
        x_parts = [xs]
    return xs.reshape(BATCH, SEQ, D_MODEL)
```

```python
import functools

import numpy as np
import jax
import jax.numpy as jnp
from jax import lax
from jax.experimental import pallas as pl
from jax.experimental.pallas import tpu as pltpu

D_MODEL = 2048
BATCH = 2
SEQ = 16384
DEPTH = 2
GRID_W = 64
GRID_ROWS = SEQ // GRID_W
CTX_LEN = 256
HEAD_DIM = 128
A_Q_HEADS = 6
A_KV_HEADS = 2
A_GROUP = A_Q_HEADS // A_KV_HEADS
A_WINDOW = 128
B_HEADS = 6
B_WIN_H = 8
B_WIN_W = 16
C_CHANNELS = 512
C_CONV_WIDTH = 31
C_PAD = (C_CONV_WIDTH - 1) // 2
D_FF = 5632
ROPE_THETA = 10000.0
NORM_EPS = 1e-6
NEG_INF = -1e30
N_MOD = 9
IN_COLS = 4608
MIX_WIDTH = 2048

HEAD_AQ = 0
HEAD_AK = HEAD_AQ + A_Q_HEADS
HEAD_AV = HEAD_AK + A_KV_HEADS
HEAD_BQ = HEAD_AV + A_KV_HEADS
HEAD_BK = HEAD_BQ + B_HEADS
HEAD_BV = HEAD_BK + B_HEADS
N_QKV_HEADS = HEAD_BV + B_HEADS
C_COL0 = N_QKV_HEADS * HEAD_DIM

N_LAT = BATCH * SEQ
N_CTX = BATCH * CTX_LEN
NT = N_LAT + N_CTX

V7X_VMEM_BYTES = 64 * 1024 * 1024
SUBLANES = 8

TM = 512
LAT_TILES = N_LAT // TM
TILES_PER_BATCH = SEQ // TM
ALL_TILES = NT // TM
TF = 512
TC = 256
CONV_HALO = 16

F32 = jnp.float32
BF16 = jnp.bfloat16


def _vmem_limit(block_bytes):
    return int(min(V7X_VMEM_BYTES - 4 * 1024 * 1024, block_bytes + 12 * 1024 * 1024))


def _group_of_tile(i):
    return i // TILES_PER_BATCH


def _modulated(x, gain, shift, scale):
    ms = jnp.mean(x * x, axis=-1, keepdims=True)
    return (x * lax.rsqrt(ms + NORM_EPS) * gain) * (1.0 + scale) + shift


MOD_TN = 1024


def _mod_kernel(c_ref, w_ref, b_ref, o_ref):
    c = c_ref[...]
    a = c * jax.nn.sigmoid(c)
    o_ref[...] = jnp.dot(a, w_ref[...], preferred_element_type=F32,
                         precision=lax.Precision.HIGHEST) + b_ref[...]


def _mod_vectors(c, c_ctx, w_mod, b_mod):
    rows = jnp.concatenate([c, c_ctx[None, :], jnp.zeros((SUBLANES - BATCH - 1, D_MODEL), F32)], axis=0)
    n = N_MOD * D_MODEL
    out = pl.pallas_call(
        _mod_kernel,
        grid=(DEPTH, n // MOD_TN),
        in_specs=[
            pl.BlockSpec((SUBLANES, D_MODEL), lambda l, j: (0, 0)),
            pl.BlockSpec((None, D_MODEL, MOD_TN), lambda l, j: (l, 0, j)),
            pl.BlockSpec((None, 1, MOD_TN), lambda l, j: (l, 0, j)),
        ],
        out_specs=pl.BlockSpec((None, SUBLANES, MOD_TN), lambda l, j: (l, 0, j)),
        out_shape=jax.ShapeDtypeStruct((DEPTH, SUBLANES, n), F32),
        compiler_params=pltpu.CompilerParams(
            dimension_semantics=("arbitrary", "arbitrary"),
            vmem_limit_bytes=_vmem_limit(2 * D_MODEL * MOD_TN * 4)),
        name="mod_vectors",
    )(rows, w_mod, b_mod.reshape(DEPTH, 1, n))
    return out[:, :3].reshape(DEPTH, 3, 3, 3, D_MODEL)


FFN_SUB = 512


def _ffn_kernel(tm, x_ref, mod_ref, gain_ref, wg_ref, wu_ref, wd_ref, *rest):
    o_ref, h_ref = rest[-2:]
    j = pl.program_id(1)

    last = pl.num_programs(1) - 1

    def chunk(first, final):
        for r0 in range(0, tm, FFN_SUB):
            rows = slice(r0, r0 + FFN_SUB)
            if first:
                h = _modulated(x_ref[rows, :], gain_ref[...], mod_ref[0:1, :], mod_ref[1:2, :])
                h_ref[rows, :] = h.astype(BF16)
            h = h_ref[rows, :]
            g = jnp.dot(h, wg_ref[...], preferred_element_type=F32)
            u = jnp.dot(h, wu_ref[...], preferred_element_type=F32)
            a = (g * jax.nn.sigmoid(g)) * u
            d = jnp.dot(a.astype(BF16), wd_ref[...], preferred_element_type=F32)
            if first:
                o_ref[rows, :] = d
            elif final:
                o_ref[rows, :] = x_ref[rows, :] + (0.5 * mod_ref[2:3, :]) * (o_ref[rows, :] + d)
            else:
                o_ref[rows, :] += d

    pl.when(j == 0)(functools.partial(chunk, True, False))
    pl.when(jnp.logical_and(j > 0, j < last))(functools.partial(chunk, False, False))
    pl.when(j == last)(functools.partial(chunk, False, True))


def _ffn(x, dest, mod_l, sub, gain, wg, wu, wd, layer, tm, x_tile0, out_tile0, n_tiles, out_rows):
    assert tm % FFN_SUB == 0 and SEQ % tm == 0 and D_FF // TF >= 2
    blocks = 4 * tm * D_MODEL * 4 + 6 * D_MODEL * TF * 2 + tm * D_MODEL * 2
    operands = [x, mod_l, gain.reshape(1, D_MODEL), wg, wu, wd]
    in_specs = [
        pl.BlockSpec((tm, D_MODEL), lambda i, j: (x_tile0 + i, 0)),
        pl.BlockSpec((None, None, 3, D_MODEL), lambda i, j: ((out_tile0 + i) * tm // SEQ, sub, 0, 0)),
        pl.BlockSpec((1, D_MODEL), lambda i, j: (0, 0)),
        pl.BlockSpec((None, D_MODEL, TF), lambda i, j: (layer, 0, j)),
        pl.BlockSpec((None, D_MODEL, TF), lambda i, j: (layer, 0, j)),
        pl.BlockSpec((None, TF, D_MODEL), lambda i, j: (layer, j, 0)),
    ]
    aliases = {}
    if dest is x:
        aliases = {0: 0}
    elif dest is not None:
        operands.append(dest)
        in_specs.append(pl.BlockSpec(memory_space=pl.ANY))
        aliases = {len(operands) - 1: 0}
    return pl.pallas_call(
        functools.partial(_ffn_kernel, tm),
        grid=(n_tiles, D_FF // TF),
        in_specs=in_specs,
        out_specs=pl.BlockSpec((tm, D_MODEL), lambda i, j: (out_tile0 + i, 0)),
        out_shape=jax.ShapeDtypeStruct((out_rows, D_MODEL), F32),
        scratch_shapes=[pltpu.VMEM((tm, D_MODEL), BF16)],
        input_output_aliases=aliases,
        compiler_params=pltpu.CompilerParams(
            dimension_semantics=("arbitrary", "arbitrary"),
            vmem_limit_bytes=_vmem_limit(blocks)),
        name="ffn",
    )(*operands)


FFN_TM_LAT = 1024
FFN_LAT_TILES = N_LAT // FFN_TM_LAT


def _ffn_streams(x_lat, x_ctx, *args, in_place, out_rows):
    combined = x_ctx is x_lat
    out = _ffn(x_lat, x_lat if in_place else None, *args, FFN_TM_LAT, 0, 0, FFN_LAT_TILES, out_rows)
    if x_ctx is not None:
        src = out if combined else x_ctx
        out = _ffn(src, out, *args, TM, LAT_TILES if combined else 0, LAT_TILES, 1, out_rows)
    return out


GAIN_AQ, GAIN_AK, GAIN_BQ, GAIN_BK = 0, 1, 2, 3


def _head_kind(h):
    if h < HEAD_AK:
        return GAIN_AQ, True
    if h < HEAD_AV:
        return GAIN_AK, True
    if h < HEAD_BQ:
        return None, False
    if h < HEAD_BK:
        return GAIN_BQ, False
    if h < HEAD_BV:
        return GAIN_BK, False
    return None, False


def _inproj_kernel(x_ref, mod_ref, gain_ref, w_ref, hg_ref, cos_ref, sin_ref, qkv_ref, cu_ref, h_ref):
    h = _modulated(x_ref[...], gain_ref[...], mod_ref[0:1, :], mod_ref[1:2, :])
    h_ref[...] = h.astype(BF16)
    lane = lax.broadcasted_iota(jnp.int32, (TM, HEAD_DIM), 1)
    even_quarter = ((lane // (HEAD_DIM // 4)) % 2) == 0
    for pair in range(N_QKV_HEADS // 2):
        c0 = pair * 2 * HEAD_DIM
        y2 = jnp.dot(h_ref[...], w_ref[:, c0:c0 + 2 * HEAD_DIM], preferred_element_type=F32)
        for half in range(2):
            hd = 2 * pair + half
            y = y2[:, half * HEAD_DIM:(half + 1) * HEAD_DIM]
            gain_row, rotary = _head_kind(hd)
            if gain_row is not None:
                ms = jnp.mean(y * y, axis=-1, keepdims=True)
                y = y * lax.rsqrt(ms + NORM_EPS) * hg_ref[gain_row:gain_row + 1, :]
            if rotary:
                swapped = jnp.where(even_quarter,
                                    pltpu.roll(y, HEAD_DIM - HEAD_DIM // 4, 1),
                                    pltpu.roll(y, HEAD_DIM // 4, 1))
                y = y * cos_ref[...] + swapped * sin_ref[...]
            qkv_ref[hd] = y.astype(BF16)
    for blk in range((IN_COLS - C_COL0) // (2 * HEAD_DIM)):
        c0 = C_COL0 + blk * 2 * HEAD_DIM
        cu_ref[:, blk * 2 * HEAD_DIM:(blk + 1) * 2 * HEAD_DIM] = jnp.dot(
            h_ref[...], w_ref[:, c0:c0 + 2 * HEAD_DIM], preferred_element_type=F32)


def _inproj(xs, mod_l, gain, w_in, layer, head_gains, cos_t, sin_t):
    n_cu = IN_COLS - C_COL0
    blocks = (2 * TM * D_MODEL * 4 + D_MODEL * IN_COLS * 2 + 2 * N_QKV_HEADS * TM * HEAD_DIM * 2
              + 2 * TM * n_cu * 4 + 4 * TM * HEAD_DIM * 4 + TM * D_MODEL * 2)
    rope_block = lambda i: (jnp.where(i < LAT_TILES, i % TILES_PER_BATCH, TILES_PER_BATCH), 0)
    return pl.pallas_call(
        _inproj_kernel,
        grid=(ALL_TILES,),
        in_specs=[
            pl.BlockSpec((TM, D_MODEL), lambda i: (i, 0)),
            pl.BlockSpec((None, None, 3, D_MODEL), lambda i: (_group_of_tile(i), 1, 0, 0)),
            pl.BlockSpec((1, D_MODEL), lambda i: (0, 0)),
            pl.BlockSpec((None, D_MODEL, IN_COLS), lambda i: (layer, 0, 0), pipeline_mode=pl.Buffered(1)),
            pl.BlockSpec((SUBLANES, HEAD_DIM), lambda i: (0, 0)),
            pl.BlockSpec((TM, HEAD_DIM), rope_block),
            pl.BlockSpec((TM, HEAD_DIM), rope_block),
        ],
        out_specs=[
            pl.BlockSpec((N_QKV_HEADS, TM, HEAD_DIM), lambda i: (0, i, 0)),
            pl.BlockSpec((TM, n_cu), lambda i: (i, 0)),
        ],
        out_shape=[
            jax.ShapeDtypeStruct((N_QKV_HEADS, NT, HEAD_DIM), BF16),
            jax.ShapeDtypeStruct((NT, n_cu), F32),
        ],
        scratch_shapes=[pltpu.VMEM((TM, D_MODEL), BF16)],
        compiler_params=pltpu.CompilerParams(
            dimension_semantics=("arbitrary",),
            vmem_limit_bytes=_vmem_limit(blocks)),
        name="inproj",
    )(xs, mod_l, gain.reshape(1, D_MODEL), w_in, head_gains, cos_t, sin_t)


def _rope_tables():
    t = np.arange(SEQ)
    n_freq = HEAD_DIM // 4
    inv_freq = ROPE_THETA ** (-np.arange(n_freq, dtype=np.float64) / n_freq)
    ang_r = (t // GRID_W)[:, None] * inv_freq[None, :]
    ang_c = (t % GRID_W)[:, None] * inv_freq[None, :]
    cos_t = np.concatenate([np.cos(ang_r), np.cos(ang_r), np.cos(ang_c), np.cos(ang_c)], axis=-1)
    sin_t = np.concatenate([-np.sin(ang_r), np.sin(ang_r), -np.sin(ang_c), np.sin(ang_c)], axis=-1)
    cos_t = np.concatenate([cos_t, np.ones((TM, HEAD_DIM))], axis=0)
    sin_t = np.concatenate([sin_t, np.zeros((TM, HEAD_DIM))], axis=0)
    return jnp.asarray(cos_t, F32), jnp.asarray(sin_t, F32)


ATTN_TQ = 256
ATTN_BLOCKS = SEQ // ATTN_TQ
ATTN_UNROLL = 8
PLACE_INTERIOR, PLACE_FIRST, PLACE_LAST = 0, 1, 2
N_PLACEMENTS = 3
PLACEMENT_BLOCKS = (1, 0, ATTN_BLOCKS - 1)

A_LEAD = A_WINDOW
A_KEYS = ATTN_TQ + 2 * A_WINDOW
B_ROWS_PER_BLOCK = ATTN_TQ // GRID_W
B_UNION_ROWS = B_ROWS_PER_BLOCK + B_WIN_H
B_LEAD = (B_WIN_H // 2) * GRID_W
B_KEYS = B_UNION_ROWS * GRID_W
LANES = 128
assert ATTN_BLOCKS % ATTN_UNROLL == 0 and ATTN_BLOCKS >= 3
assert A_LEAD % LANES == 0 and B_LEAD % LANES == 0 and A_KEYS % LANES == 0 and B_KEYS % LANES == 0


def _dot_nt(a, b):
    return lax.dot_general(a, b, (((1,), (1,)), ((), ())), preferred_element_type=F32)


def _window_start(first_query, lead, n_keys):
    return int(np.clip(first_query - lead, 0, SEQ - n_keys))


def _local_attn_kernel(lead, n_keys, has_sink, *refs):
    if has_sink:
        sink_ref, refs = refs[0], refs[1:]
    q_ref, k_ref, v_ref, kc_ref, vc_ref, bias_ref, o_ref = refs
    sink = sink_ref[pl.program_id(1) * A_GROUP + pl.program_id(2)] if has_sink else None
    kc = kc_ref[...]
    vc = vc_ref[...]

    def scores(i):
        q0 = pl.multiple_of(i * ATTN_TQ, ATTN_TQ)
        k0 = pl.multiple_of(jnp.clip(q0 - lead, 0, SEQ - n_keys), LANES)
        placement = jnp.where(i == 0, PLACE_FIRST, jnp.where(i == ATTN_BLOCKS - 1, PLACE_LAST, PLACE_INTERIOR))
        q = q_ref[pl.ds(q0, ATTN_TQ), :]
        s = _dot_nt(q, k_ref[pl.ds(k0, n_keys), :]) + bias_ref[placement]
        return q0, k0, s, _dot_nt(q, kc)

    def softmax(q0, k0, s, sc):
        m = jnp.maximum(jnp.max(s, axis=-1, keepdims=True), jnp.max(sc, axis=-1, keepdims=True))
        if has_sink:
            m = jnp.maximum(m, sink)
        p = jnp.exp(s - m)
        pc = jnp.exp(sc - m)
        den = jnp.sum(p, axis=-1, keepdims=True) + jnp.sum(pc, axis=-1, keepdims=True)
        if has_sink:
            den = den + jnp.exp(sink - m)
        return q0, k0, p.astype(BF16), pc.astype(BF16), den

    def output(q0, k0, p, pc, den):
        o = (jnp.dot(p, v_ref[pl.ds(k0, n_keys), :], preferred_element_type=F32)
             + jnp.dot(pc, vc, preferred_element_type=F32))
        o_ref[pl.ds(q0, ATTN_TQ), :] = (o / den).astype(BF16)

    def body(it, carry):
        scored, weighted = {}, {}
        for step in range(ATTN_UNROLL + 2):
            if step < ATTN_UNROLL:
                scored[step] = scores(it * ATTN_UNROLL + step)
            if 0 <= step - 2 < ATTN_UNROLL:
                output(*weighted.pop(step - 2))
            if 0 <= step - 1 < ATTN_UNROLL:
                weighted[step - 1] = softmax(*scored.pop(step - 1))
        return carry

    lax.fori_loop(0, ATTN_BLOCKS // ATTN_UNROLL, body, 0)


def _local_attn_vmem(n_keys):
    return 2 * 4 * SEQ * HEAD_DIM * 2 + 4 * CTX_LEN * HEAD_DIM * 2 + 2 * N_PLACEMENTS * ATTN_TQ * n_keys * 4


def _a_band_table():
    tables = []
    for blk in PLACEMENT_BLOCKS:
        q0 = blk * ATTN_TQ
        kpos = _window_start(q0, A_LEAD, A_KEYS) + np.arange(A_KEYS)[None, :]
        qpos = q0 + np.arange(ATTN_TQ)[:, None]
        tables.append(np.where(np.abs(kpos - qpos) <= A_WINDOW, 0.0, NEG_INF))
    return jnp.asarray(np.stack(tables), F32)


def _attn_a(qkv, sink):
    seq_block = (None, SEQ, HEAD_DIM)
    ctx_block = (None, CTX_LEN, HEAD_DIM)
    ctx_row0 = N_LAT // CTX_LEN
    return pl.pallas_call(
        functools.partial(_local_attn_kernel, A_LEAD, A_KEYS, True),
        grid=(BATCH, A_KV_HEADS, A_GROUP),
        in_specs=[
            pl.BlockSpec(memory_space=pltpu.SMEM),
            pl.BlockSpec(seq_block, lambda b, kv, g: (HEAD_AQ + kv * A_GROUP + g, b, 0)),
            pl.BlockSpec(seq_block, lambda b, kv, g: (HEAD_AK + kv, b, 0)),
            pl.BlockSpec(seq_block, lambda b, kv, g: (HEAD_AV + kv, b, 0)),
            pl.BlockSpec(ctx_block, lambda b, kv, g: (HEAD_AK + kv, ctx_row0 + b, 0)),
            pl.BlockSpec(ctx_block, lambda b, kv, g: (HEAD_AV + kv, ctx_row0 + b, 0)),
            pl.BlockSpec((N_PLACEMENTS, ATTN_TQ, A_KEYS), lambda b, kv, g: (0, 0, 0)),
        ],
        out_specs=pl.BlockSpec((SEQ, HEAD_DIM), lambda b, kv, g: (b, kv * A_GROUP + g)),
        out_shape=jax.ShapeDtypeStruct((N_LAT, A_Q_HEADS * HEAD_DIM), BF16),
        compiler_params=pltpu.CompilerParams(
            dimension_semantics=("arbitrary", "arbitrary", "arbitrary"),
            vmem_limit_bytes=_vmem_limit(_local_attn_vmem(A_KEYS))),
        name="attn_window",
    )(sink, qkv, qkv, qkv, qkv, qkv, _a_band_table())


def _b_bias_tables(rpb):
    j = np.arange(GRID_W)[:, None]
    jk = np.arange(GRID_W)[None, :]
    col_start = np.clip(j - B_WIN_W // 2, 0, GRID_W - B_WIN_W)
    inside = (jk >= col_start) & (jk < col_start + B_WIN_W)
    n_col = 2 * B_WIN_W - 1
    onehot = (inside[:, :, None] & ((jk - j + (B_WIN_W - 1))[:, :, None] == np.arange(n_col))).astype(np.float32)
    by_row = jnp.einsum('lhac,jkc->lhajk', rpb, jnp.asarray(onehot), precision=lax.Precision.HIGHEST)
    by_row = by_row + jnp.asarray(np.where(inside, 0.0, NEG_INF), F32)
    placements = []
    for blk in PLACEMENT_BLOCKS:
        r0 = blk * B_ROWS_PER_BLOCK
        u0 = _window_start(r0 * GRID_W, B_LEAD, B_KEYS) // GRID_W
        per_row = []
        for t in range(B_ROWS_PER_BLOCK):
            r = r0 + t
            rs = int(np.clip(r - B_WIN_H // 2, 0, GRID_ROWS - B_WIN_H))
            before = rs - u0
            after = B_UNION_ROWS - B_WIN_H - before
            assert before >= 0 and after >= 0
            bias_row0 = rs - r + (B_WIN_H - 1)
            piece = by_row[:, :, bias_row0:bias_row0 + B_WIN_H]
            per_row.append(jnp.pad(piece, ((0, 0), (0, 0), (before, after), (0, 0), (0, 0)),
                                   constant_values=NEG_INF))
        placements.append(jnp.stack(per_row, axis=2))
    t = jnp.stack(placements, axis=2)
    return t.transpose(0, 1, 2, 3, 5, 4, 6).reshape(DEPTH, B_HEADS, N_PLACEMENTS, ATTN_TQ, B_KEYS)


def _attn_b(qkv, bias, layer):
    seq_block = (None, SEQ, HEAD_DIM)
    ctx_block = (None, CTX_LEN, HEAD_DIM)
    ctx_row0 = N_LAT // CTX_LEN
    return pl.pallas_call(
        functools.partial(_local_attn_kernel, B_LEAD, B_KEYS, False),
        grid=(BATCH, B_HEADS),
        in_specs=[
            pl.BlockSpec(seq_block, lambda b, h: (HEAD_BQ + h, b, 0)),
            pl.BlockSpec(seq_block, lambda b, h: (HEAD_BK + h, b, 0)),
            pl.BlockSpec(seq_block, lambda b, h: (HEAD_BV + h, b, 0)),
            pl.BlockSpec(ctx_block, lambda b, h: (HEAD_BK + h, ctx_row0 + b, 0)),
            pl.BlockSpec(ctx_block, lambda b, h: (HEAD_BV + h, ctx_row0 + b, 0)),
            pl.BlockSpec((None, None, N_PLACEMENTS, ATTN_TQ, B_KEYS), lambda b, h: (layer, h, 0, 0, 0)),
        ],
        out_specs=pl.BlockSpec((SEQ, HEAD_DIM), lambda b, h: (b, h)),
        out_shape=jax.ShapeDtypeStruct((N_LAT, B_HEADS * HEAD_DIM), BF16),
        compiler_params=pltpu.CompilerParams(
            dimension_semantics=("arbitrary", "arbitrary"),
            vmem_limit_bytes=_vmem_limit(_local_attn_vmem(B_KEYS))),
        name="attn_neighbourhood",
    )(qkv, qkv, qkv, qkv, qkv, bias)


N_CTX_HEADS = A_Q_HEADS + B_HEADS


def _attn_ctx_kernel(sink_ref, q_ref, k_ref, v_ref, o_ref):
    sink = sink_ref[pl.program_id(1)]
    q = q_ref[...]
    s = _dot_nt(q, k_ref[...])
    m = jnp.maximum(jnp.max(s, axis=-1, keepdims=True), sink)
    p = jnp.exp(s - m)
    den = jnp.sum(p, axis=-1, keepdims=True) + jnp.exp(sink - m)
    o = jnp.dot(p.astype(BF16), v_ref[...], preferred_element_type=F32)
    o_ref[...] = (o / den).astype(BF16)


def _attn_ctx(qkv, sinks):
    ctx_block = (None, CTX_LEN, HEAD_DIM)
    ctx_row0 = N_LAT // CTX_LEN
    is_b = lambda h: h >= A_Q_HEADS
    q_head = lambda h: jnp.where(is_b(h), HEAD_BQ + h - A_Q_HEADS, HEAD_AQ + h)
    k_head = lambda h: jnp.where(is_b(h), HEAD_BK + h - A_Q_HEADS, HEAD_AK + h // A_GROUP)
    v_head = lambda h: jnp.where(is_b(h), HEAD_BV + h - A_Q_HEADS, HEAD_AV + h // A_GROUP)
    return pl.pallas_call(
        _attn_ctx_kernel,
        grid=(BATCH, N_CTX_HEADS),
        in_specs=[
            pl.BlockSpec(memory_space=pltpu.SMEM),
            pl.BlockSpec(ctx_block, lambda b, h: (q_head(h), ctx_row0 + b, 0)),
            pl.BlockSpec(ctx_block, lambda b, h: (k_head(h), ctx_row0 + b, 0)),
            pl.BlockSpec(ctx_block, lambda b, h: (v_head(h), ctx_row0 + b, 0)),
        ],
        out_specs=pl.BlockSpec((CTX_LEN, HEAD_DIM), lambda b, h: (b, h)),
        out_shape=jax.ShapeDtypeStruct((N_CTX, N_CTX_HEADS * HEAD_DIM), BF16),
        compiler_params=pltpu.CompilerParams(dimension_semantics=("arbitrary", "arbitrary")),
        name="attn_context",
    )(sinks, qkv, qkv, qkv)


CONV_ROWS = 32
SEQ_TILES = SEQ // TC
CONV_SH_ROWS = TC + SUBLANES * ((C_CONV_WIDTH - 1) // SUBLANES)
assert SUBLANES - 1 + CONV_HALO - C_PAD + CONV_SH_ROWS <= TC + 2 * CONV_HALO


def _glu(u):
    return u[:, :C_CHANNELS] * jax.nn.sigmoid(u[:, C_CHANNELS:])


def _conv_kernel(prev_ref, cur_ref, next_ref, w_ref, b_ref, g_ref, beta_ref, o_ref, ext_ref, sh_ref):
    i = pl.program_id(0)
    n_lat_tiles = BATCH * SEQ_TILES
    first = jnp.logical_or(i % SEQ_TILES == 0, i >= n_lat_tiles)
    last = jnp.logical_or(i % SEQ_TILES == SEQ_TILES - 1, i >= n_lat_tiles)
    ext_ref[0:CONV_HALO, :] = jnp.where(first, 0.0, _glu(prev_ref[...]))
    ext_ref[CONV_HALO:CONV_HALO + TC, :] = _glu(cur_ref[...])
    ext_ref[CONV_HALO + TC:, :] = jnp.where(last, 0.0, _glu(next_ref[...]))
    for b in range(SUBLANES):
        lo = b + CONV_HALO - C_PAD
        sh_ref[b] = ext_ref[lo:lo + CONV_SH_ROWS, :]
    groups = CONV_ROWS // SUBLANES
    for c in range(TC // CONV_ROWS):
        r0 = c * CONV_ROWS
        accs = [jnp.zeros((SUBLANES, C_CHANNELS), F32) for _ in range(groups)]
        for k in range(C_CONV_WIDTH):
            a, b = divmod(k, SUBLANES)
            w_k = w_ref[k]
            for g in range(groups):
                lo = r0 + SUBLANES * (a + g)
                accs[g] = accs[g] + sh_ref[b, lo:lo + SUBLANES, :] * w_k
        acc = jnp.concatenate(accs, axis=0) + b_ref[...]
        mu = jnp.mean(acc, axis=-1, keepdims=True)
        xc = acc - mu
        y = xc * lax.rsqrt(jnp.mean(xc * xc, axis=-1, keepdims=True) + NORM_EPS)
        y = y * g_ref[...] + beta_ref[...]
        o_ref[r0:r0 + CONV_ROWS, :] = (y * jax.nn.sigmoid(y)).astype(BF16)


def _conv(cu, dw_w, dw_b, ln_g, ln_b):
    halo_per_tile = TC // CONV_HALO
    n_halo = NT // CONV_HALO
    vec = pl.BlockSpec((1, C_CHANNELS), lambda i: (0, 0))
    return pl.pallas_call(
        _conv_kernel,
        grid=(NT // TC,),
        in_specs=[
            pl.BlockSpec((CONV_HALO, 2 * C_CHANNELS), lambda i: (jnp.maximum(i * halo_per_tile - 1, 0), 0)),
            pl.BlockSpec((TC, 2 * C_CHANNELS), lambda i: (i, 0)),
            pl.BlockSpec((CONV_HALO, 2 * C_CHANNELS),
                         lambda i: (jnp.minimum((i + 1) * halo_per_tile, n_halo - 1), 0)),
            pl.BlockSpec((C_CONV_WIDTH, SUBLANES, C_CHANNELS), lambda i: (0, 0, 0)),
            vec, vec, vec,
        ],
        out_specs=pl.BlockSpec((TC, C_CHANNELS), lambda i: (i, 0)),
        out_shape=jax.ShapeDtypeStruct((NT, C_CHANNELS), BF16),
        scratch_shapes=[pltpu.VMEM((TC + 2 * CONV_HALO, C_CHANNELS), F32),
                        pltpu.VMEM((SUBLANES, CONV_SH_ROWS, C_CHANNELS), F32)],
        compiler_params=pltpu.CompilerParams(dimension_semantics=("arbitrary",)),
        name="conv_module",
    )(cu, cu, cu, jnp.broadcast_to(dw_w[:, None, :], (C_CONV_WIDTH, SUBLANES, C_CHANNELS)),
      dw_b.reshape(1, -1), ln_g.reshape(1, -1), ln_b.reshape(1, -1))


def _outproj_kernel(widths, x_ref, mod_ref, w_ref, *refs):
    pieces, o_ref = refs[:-1], refs[-1]
    y = None
    k0 = 0
    for piece, width in zip(pieces, widths):
        t = jnp.dot(piece[...], w_ref[k0:k0 + width, :], preferred_element_type=F32)
        y = t if y is None else y + t
        k0 += width
    o_ref[...] = x_ref[...] + mod_ref[2:3, :] * y


def _outproj(xs, mod_l, w_out, layer, pieces, tile0, n_tiles):
    widths = tuple(int(p.shape[1]) for p, _ in pieces)
    assert sum(widths) == MIX_WIDTH
    blocks = 4 * TM * D_MODEL * 4 + 2 * MIX_WIDTH * D_MODEL * 2 + 2 * TM * MIX_WIDTH * 2
    piece_specs = [pl.BlockSpec((TM, w), functools.partial(lambda i, r0: (r0 + i, 0), r0=r0))
                   for w, (_, r0) in zip(widths, pieces)]
    return pl.pallas_call(
        functools.partial(_outproj_kernel, widths),
        grid=(n_tiles,),
        in_specs=[
            pl.BlockSpec((TM, D_MODEL), lambda i: (tile0 + i, 0)),
            pl.BlockSpec((None, None, 3, D_MODEL), lambda i: (_group_of_tile(tile0 + i), 1, 0, 0)),
            pl.BlockSpec((None, MIX_WIDTH, D_MODEL), lambda i: (layer, 0, 0)),
        ] + piece_specs,
        out_specs=pl.BlockSpec((TM, D_MODEL), lambda i: (tile0 + i, 0)),
        out_shape=jax.ShapeDtypeStruct((NT, D_MODEL), F32),
        input_output_aliases={0: 0},
        compiler_params=pltpu.CompilerParams(
            dimension_semantics=("arbitrary",),
            vmem_limit_bytes=_vmem_limit(blocks)),
        name="outproj",
    )(xs, mod_l, w_out, *[p for p, _ in pieces])


def kernel(x, c, ctx, c_ctx, w_mod, b_mod, norm_ffn1, norm_mix, norm_ffn2, ffn1_w_gate, ffn1_w_up,
           ffn1_w_down, ffn2_w_gate, ffn2_w_up, ffn2_w_down, w_in, w_out, a_q_norm, a_k_norm, a_sink,
           b_q_norm, b_k_norm, b_rpb, c_dw_w, c_dw_b, c_ln_g, c_ln_b):
    mods = _mod_vectors(c, c_ctx, w_mod, b_mod)
    cos_t, sin_t = _rope_tables()
    b_bias = _b_bias_tables(b_rpb)
    ffn1_w = [w.astype(BF16) for w in (ffn1_w_gate, ffn1_w_up, ffn1_w_down)]
    ffn2_w = [w.astype(BF16) for w in (ffn2_w_gate, ffn2_w_up, ffn2_w_down)]
    w_in_bf = w_in.astype(BF16)
    w_out_bf = w_out.astype(BF16)
    q_scale = HEAD_DIM ** -0.5
    xs = None
    for l in range(DEPTH):
        last = l == DEPTH - 1
        mod_l = mods[l]
        if l == 0:
            xs = _ffn_streams(x.reshape(N_LAT, D_MODEL), ctx.reshape(N_CTX, D_MODEL), mod_l, 0, norm_ffn1[l],
                              *ffn1_w, l, in_place=False, out_rows=NT)
        else:
            xs = _ffn_streams(xs, xs, mod_l, 0, norm_ffn1[l], *ffn1_w, l, in_place=True, out_rows=NT)
        head_gains = jnp.concatenate([
            (a_q_norm[l] * q_scale)[None], a_k_norm[l][None], (b_q_norm[l] * q_scale)[None],
            b_k_norm[l][None], jnp.zeros((SUBLANES - 4, HEAD_DIM), F32)], axis=0)
        qkv, cu = _inproj(xs, mod_l, norm_mix[l], w_in_bf, l, head_gains, cos_t, sin_t)
        o_a = _attn_a(qkv, a_sink[l])
        o_b = _attn_b(qkv, b_bias, l)
        o_c = _conv(cu, c_dw_w[l], c_dw_b[l], c_ln_g[l], c_ln_b[l])
        xs = _outproj(xs, mod_l, w_out_bf, l, [(o_a, 0), (o_b, 0), (o_c, 0)], 0, LAT_TILES)
        if not last:
            sinks = jnp.concatenate([a_sink[l], jnp.full((B_HEADS,), NEG_INF, F32)])
            o_ctx = _attn_ctx(qkv, sinks)
            xs = _outproj(xs, mod_l, w_out_bf, l, [(o_ctx, 0), (o_c, LAT_TILES)], LAT_TILES, 1)
        if last:
            xs = _ffn_streams(xs, None, mod_l, 2, norm_ffn2[l], *ffn2_w, l, in_place=False, out_rows=N_LAT)
        else:
            xs = _ffn_streams(xs, xs, mod_l, 2, norm_ffn2[l], *ffn2_w, l, in_place=True, out_rows=NT)
    return xs.reshape(BATCH, SEQ, D_MODEL)
```

```python
import functools

import numpy as np
import jax
import jax.numpy as jnp
from jax import lax
from jax.experimental import pallas as pl
from jax.experimental.pallas import tpu as pltpu

D_MODEL = 2048
BATCH = 2
SEQ = 16384
DEPTH = 2
GRID_W = 64
GRID_ROWS = SEQ // GRID_W
CTX_LEN = 256
HEAD_DIM = 128
A_Q_HEADS = 6
A_KV_HEADS = 2
A_GROUP = A_Q_HEADS // A_KV_HEADS
A_WINDOW = 128
B_HEADS = 6
B_WIN_H = 8
B_WIN_W = 16
C_CHANNELS = 512
C_CONV_WIDTH = 31
C_PAD = (C_CONV_WIDTH - 1) // 2
D_FF = 5632
ROPE_THETA = 10000.0
NORM_EPS = 1e-6
NEG_INF = -1e30
N_MOD = 9
IN_COLS = 4608
MIX_WIDTH = 2048

HEAD_AQ = 0
HEAD_AK = HEAD_AQ + A_Q_HEADS
HEAD_AV = HEAD_AK + A_KV_HEADS
HEAD_BQ = HEAD_AV + A_KV_HEADS
HEAD_BK = HEAD_BQ + B_HEADS
HEAD_BV = HEAD_BK + B_HEADS
N_QKV_HEADS = HEAD_BV + B_HEADS
C_COL0 = N_QKV_HEADS * HEAD_DIM

N_LAT = BATCH * SEQ
N_CTX = BATCH * CTX_LEN
NT = N_LAT + N_CTX

V7X_VMEM_BYTES = 64 * 1024 * 1024
SUBLANES = 8

TM = 512
LAT_TILES = N_LAT // TM
TILES_PER_BATCH = SEQ // TM
ALL_TILES = NT // TM
TF = 512
TC = 256
CONV_HALO = 16

F32 = jnp.float32
BF16 = jnp.bfloat16


def _vmem_limit(block_bytes):
    return int(min(V7X_VMEM_BYTES - 4 * 1024 * 1024, block_bytes + 12 * 1024 * 1024))


def _group_of_tile(i):
    return i // TILES_PER_BATCH


def _modulated(x, gain, shift, scale):
    ms = jnp.mean(x * x, axis=-1, keepdims=True)
    return (x * lax.rsqrt(ms + NORM_EPS) * gain) * (1.0 + scale) + shift


MOD_TN = 1024


def _mod_kernel(c_ref, w_ref, b_ref, o_ref):
    c = c_ref[...]
    a = c * jax.nn.sigmoid(c)
    o_ref[...] = jnp.dot(a, w_ref[...], preferred_element_type=F32,
                         precision=lax.Precision.HIGHEST) + b_ref[...]


def _mod_vectors(c, c_ctx, w_mod, b_mod):
    rows = jnp.concatenate([c, c_ctx[None, :], jnp.zeros((SUBLANES - BATCH - 1, D_MODEL), F32)], axis=0)
    n = N_MOD * D_MODEL
    out = pl.pallas_call(
        _mod_kernel,
        grid=(DEPTH, n // MOD_TN),
        in_specs=[
            pl.BlockSpec((SUBLANES, D_MODEL), lambda l, j: (0, 0)),
            pl.BlockSpec((None, D_MODEL, MOD_TN), lambda l, j: (l, 0, j)),
            pl.BlockSpec((None, 1, MOD_TN), lambda l, j: (l, 0, j)),
        ],
        out_specs=pl.BlockSpec((None, SUBLANES, MOD_TN), lambda l, j: (l, 0, j)),
        out_shape=jax.ShapeDtypeStruct((DEPTH, SUBLANES, n), F32),
        compiler_params=pltpu.CompilerParams(
            dimension_semantics=("arbitrary", "arbitrary"),
            vmem_limit_bytes=_vmem_limit(2 * D_MODEL * MOD_TN * 4)),
        name="mod_vectors",
    )(rows, w_mod, b_mod.reshape(DEPTH, 1, n))
    return out[:, :3].reshape(DEPTH, 3, 3, 3, D_MODEL)


FFN_SUB = 512


def _ffn_kernel(tm, x_ref, mod_ref, gain_ref, wg_ref, wu_ref, wd_ref, *rest):
    o_ref, h_ref = rest[-2:]
    j = pl.program_id(1)

    last = pl.num_programs(1) - 1

    def chunk(first, final):
        for r0 in range(0, tm, FFN_SUB):
            rows = slice(r0, r0 + FFN_SUB)
            if first:
                h = _modulated(x_ref[rows, :], gain_ref[...], mod_ref[0:1, :], mod_ref[1:2, :])
                h_ref[rows, :] = h.astype(BF16)
            h = h_ref[rows, :]
            g = jnp.dot(h, wg_ref[...], preferred_element_type=F32)
            u = jnp.dot(h, wu_ref[...], preferred_element_type=F32)
            a = (g * jax.nn.sigmoid(g)) * u
            d = jnp.dot(a.astype(BF16), wd_ref[...], preferred_element_type=F32)
            if first:
                o_ref[rows, :] = d
            elif final:
                o_ref[rows, :] = x_ref[rows, :] + (0.5 * mod_ref[2:3, :]) * (o_ref[rows, :] + d)
            else:
                o_ref[rows, :] += d

    pl.when(j == 0)(functools.partial(chunk, True, False))
    pl.when(jnp.logical_and(j > 0, j < last))(functools.partial(chunk, False, False))
    pl.when(j == last)(functools.partial(chunk, False, True))


def _ffn(x, dest, mod_l, sub, gain, wg, wu, wd, layer, tm, x_tile0, out_tile0, n_tiles, out_rows):
    assert tm % FFN_SUB == 0 and SEQ % tm == 0 and D_FF // TF >= 2
    blocks = 4 * tm * D_MODEL * 4 + 6 * D_MODEL * TF * 2 + tm * D_MODEL * 2
    operands = [x, mod_l, gain.reshape(1, D_MODEL), wg, wu, wd]
    in_specs = [
        pl.BlockSpec((tm, D_MODEL), lambda i, j: (x_tile0 + i, 0)),
        pl.BlockSpec((None, None, 3, D_MODEL), lambda i, j: ((out_tile0 + i) * tm // SEQ, sub, 0, 0)),
        pl.BlockSpec((1, D_MODEL), lambda i, j: (0, 0)),
        pl.BlockSpec((None, D_MODEL, TF), lambda i, j: (layer, 0, j)),
        pl.BlockSpec((None, D_MODEL, TF), lambda i, j: (layer, 0, j)),
        pl.BlockSpec((None, TF, D_MODEL), lambda i, j: (layer, j, 0)),
    ]
    aliases = {}
    if dest is x:
        aliases = {0: 0}
    elif dest is not None:
        operands.append(dest)
        in_specs.append(pl.BlockSpec(memory_space=pl.ANY))
        aliases = {len(operands) - 1: 0}
    return pl.pallas_call(
        functools.partial(_ffn_kernel, tm),
        grid=(n_tiles, D_FF // TF),
        in_specs=in_specs,
        out_specs=pl.BlockSpec((tm, D_MODEL), lambda i, j: (out_tile0 + i, 0)),
        out_shape=jax.ShapeDtypeStruct((out_rows, D_MODEL), F32),
        scratch_shapes=[pltpu.VMEM((tm, D_MODEL), BF16)],
        input_output_aliases=aliases,
        compiler_params=pltpu.CompilerParams(
            dimension_semantics=("arbitrary", "arbitrary"),
            vmem_limit_bytes=_vmem_limit(blocks)),
        name="ffn",
    )(*operands)


FFN_TM_LAT = 1024
FFN_LAT_TILES = N_LAT // FFN_TM_LAT


def _ffn_streams(x_lat, x_ctx, *args, in_place, out_rows):
    combined = x_ctx is x_lat
    out = _ffn(x_lat, x_lat if in_place else None, *args, FFN_TM_LAT, 0, 0, FFN_LAT_TILES, out_rows)
    if x_ctx is not None:
        src = out if combined else x_ctx
        out = _ffn(src, out, *args, TM, LAT_TILES if combined else 0, LAT_TILES, 1, out_rows)
    return out


GAIN_AQ, GAIN_AK, GAIN_BQ, GAIN_BK = 0, 1, 2, 3


def _head_kind(h):
    if h < HEAD_AK:
        return GAIN_AQ, True
    if h < HEAD_AV:
        return GAIN_AK, True
    if h < HEAD_BQ:
        return None, False
    if h < HEAD_BK:
        return GAIN_BQ, False
    if h < HEAD_BV:
        return GAIN_BK, False
    return None, False


def _inproj_kernel(x_ref, mod_ref, gain_ref, w_ref, hg_ref, cos_ref, sin_ref, qkv_ref, cu_ref, h_ref):
    h = _modulated(x_ref[...], gain_ref[...], mod_ref[0:1, :], mod_ref[1:2, :])
    h_ref[...] = h.astype(BF16)
    lane = lax.broadcasted_iota(jnp.int32, (TM, HEAD_DIM), 1)
    even_quarter = ((lane // (HEAD_DIM // 4)) % 2) == 0
    for pair in range(N_QKV_HEADS // 2):
        c0 = pair * 2 * HEAD_DIM
        y2 = jnp.dot(h_ref[...], w_ref[:, c0:c0 + 2 * HEAD_DIM], preferred_element_type=F32)
        for half in range(2):
            hd = 2 * pair + half
            y = y2[:, half * HEAD_DIM:(half + 1) * HEAD_DIM]
            gain_row, rotary = _head_kind(hd)
            if gain_row is not None:
                ms = jnp.mean(y * y, axis=-1, keepdims=True)
                y = y * lax.rsqrt(ms + NORM_EPS) * hg_ref[gain_row:gain_row + 1, :]
            if rotary:
                swapped = jnp.where(even_quarter,
                                    pltpu.roll(y, HEAD_DIM - HEAD_DIM // 4, 1),
                                    pltpu.roll(y, HEAD_DIM // 4, 1))
                y = y * cos_ref[...] + swapped * sin_ref[...]
            qkv_ref[hd] = y.astype(BF16)
    for blk in range((IN_COLS - C_COL0) // (2 * HEAD_DIM)):
        c0 = C_COL0 + blk * 2 * HEAD_DIM
        cu_ref[:, blk * 2 * HEAD_DIM:(blk + 1) * 2 * HEAD_DIM] = jnp.dot(
            h_ref[...], w_ref[:, c0:c0 + 2 * HEAD_DIM], preferred_element_type=F32)


def _inproj(xs, mod_l, gain, w_in, layer, head_gains, cos_t, sin_t):
    n_cu = IN_COLS - C_COL0
    blocks = (2 * TM * D_MODEL * 4 + D_MODEL * IN_COLS * 2 + 2 * N_QKV_HEADS * TM * HEAD_DIM * 2
              + 2 * TM * n_cu * 4 + 4 * TM * HEAD_DIM * 4 + TM * D_MODEL * 2)
    rope_block = lambda i: (jnp.where(i < LAT_TILES, i % TILES_PER_BATCH, TILES_PER_BATCH), 0)
    return pl.pallas_call(
        _inproj_kernel,
        grid=(ALL_TILES,),
        in_specs=[
            pl.BlockSpec((TM, D_MODEL), lambda i: (i, 0)),
            pl.BlockSpec((None, None, 3, D_MODEL), lambda i: (_group_of_tile(i), 1, 0, 0)),
            pl.BlockSpec((1, D_MODEL), lambda i: (0, 0)),
            pl.BlockSpec((None, D_MODEL, IN_COLS), lambda i: (layer, 0, 0), pipeline_mode=pl.Buffered(1)),
            pl.BlockSpec((SUBLANES, HEAD_DIM), lambda i: (0, 0)),
            pl.BlockSpec((TM, HEAD_DIM), rope_block),
            pl.BlockSpec((TM, HEAD_DIM), rope_block),
        ],
        out_specs=[
            pl.BlockSpec((N_QKV_HEADS, TM, HEAD_DIM), lambda i: (0, i, 0)),
            pl.BlockSpec((TM, n_cu), lambda i: (i, 0)),
        ],
        out_shape=[
            jax.ShapeDtypeStruct((N_QKV_HEADS, NT, HEAD_DIM), BF16),
            jax.ShapeDtypeStruct((NT, n_cu), F32),
        ],
        scratch_shapes=[pltpu.VMEM((TM, D_MODEL), BF16)],
        compiler_params=pltpu.CompilerParams(
            dimension_semantics=("arbitrary",),
            vmem_limit_bytes=_vmem_limit(blocks)),
        name="inproj",
    )(xs, mod_l, gain.reshape(1, D_MODEL), w_in, head_gains, cos_t, sin_t)


def _rope_tables():
    t = np.arange(SEQ)
    n_freq = HEAD_DIM // 4
    inv_freq = ROPE_THETA ** (-np.arange(n_freq, dtype=np.float64) / n_freq)
    ang_r = (t // GRID_W)[:, None] * inv_freq[None, :]
    ang_c = (t % GRID_W)[:, None] * inv_freq[None, :]
    cos_t = np.concatenate([np.cos(ang_r), np.cos(ang_r), np.cos(ang_c), np.cos(ang_c)], axis=-1)
    sin_t = np.concatenate([-np.sin(ang_r), np.sin(ang_r), -np.sin(ang_c), np.sin(ang_c)], axis=-1)
    cos_t = np.concatenate([cos_t, np.ones((TM, HEAD_DIM))], axis=0)
    sin_t = np.concatenate([sin_t, np.zeros((TM, HEAD_DIM))], axis=0)
    return jnp.asarray(cos_t, F32), jnp.asarray(sin_t, F32)


ATTN_TQ = 256
ATTN_BLOCKS = SEQ // ATTN_TQ
ATTN_UNROLL = 8
PLACE_INTERIOR, PLACE_FIRST, PLACE_LAST = 0, 1, 2
N_PLACEMENTS = 3
PLACEMENT_BLOCKS = (1, 0, ATTN_BLOCKS - 1)

A_LEAD = A_WINDOW
A_KEYS = ATTN_TQ + 2 * A_WINDOW
B_ROWS_PER_BLOCK = ATTN_TQ // GRID_W
B_UNION_ROWS = B_ROWS_PER_BLOCK + B_WIN_H
B_LEAD = (B_WIN_H // 2) * GRID_W
B_KEYS = B_UNION_ROWS * GRID_W
LANES = 128
assert ATTN_BLOCKS % ATTN_UNROLL == 0 and ATTN_BLOCKS >= 3
assert A_LEAD % LANES == 0 and B_LEAD % LANES == 0 and A_KEYS % LANES == 0 and B_KEYS % LANES == 0


def _dot_nt(a, b):
    return lax.dot_general(a, b, (((1,), (1,)), ((), ())), preferred_element_type=F32)


def _window_start(first_query, lead, n_keys):
    return int(np.clip(first_query - lead, 0, SEQ - n_keys))


def _local_attn_kernel(lead, n_keys, has_sink, *refs):
    if has_sink:
        sink_ref, refs = refs[0], refs[1:]
    q_ref, k_ref, v_ref, kc_ref, vc_ref, bias_ref = refs[:6]
    o_ref = refs[-1]
    sink = sink_ref[pl.program_id(1) * A_GROUP + pl.program_id(2)] if has_sink else None
    kc = kc_ref[...]
    vc = vc_ref[...]

    def scores(i):
        q0 = pl.multiple_of(i * ATTN_TQ, ATTN_TQ)
        k0 = pl.multiple_of(jnp.clip(q0 - lead, 0, SEQ - n_keys), LANES)
        placement = jnp.where(i == 0, PLACE_FIRST, jnp.where(i == ATTN_BLOCKS - 1, PLACE_LAST, PLACE_INTERIOR))
        q = q_ref[pl.ds(q0, ATTN_TQ), :]
        s = _dot_nt(q, k_ref[pl.ds(k0, n_keys), :]) + bias_ref[placement]
        return q0, k0, s, _dot_nt(q, kc)

    def softmax(q0, k0, s, sc):
        m = jnp.maximum(jnp.max(s, axis=-1, keepdims=True), jnp.max(sc, axis=-1, keepdims=True))
        if has_sink:
            m = jnp.maximum(m, sink)
        p = jnp.exp(s - m)
        pc = jnp.exp(sc - m)
        den = jnp.sum(p, axis=-1, keepdims=True) + jnp.sum(pc, axis=-1, keepdims=True)
        if has_sink:
            den = den + jnp.exp(sink - m)
        return q0, k0, p.astype(BF16), pc.astype(BF16), den

    def output(q0, k0, p, pc, den):
        o = (jnp.dot(p, v_ref[pl.ds(k0, n_keys), :], preferred_element_type=F32)
             + jnp.dot(pc, vc, preferred_element_type=F32))
        o_ref[pl.ds(q0, ATTN_TQ), :] = (o / den).astype(BF16)

    def body(it, carry):
        scored, weighted = {}, {}
        for step in range(ATTN_UNROLL + 2):
            if step < ATTN_UNROLL:
                scored[step] = scores(it * ATTN_UNROLL + step)
            if 0 <= step - 2 < ATTN_UNROLL:
                output(*weighted.pop(step - 2))
            if 0 <= step - 1 < ATTN_UNROLL:
                weighted[step - 1] = softmax(*scored.pop(step - 1))
        return carry

    lax.fori_loop(0, ATTN_BLOCKS // ATTN_UNROLL, body, 0)


def _local_attn_vmem(n_keys):
    return 2 * 4 * SEQ * HEAD_DIM * 2 + 4 * CTX_LEN * HEAD_DIM * 2 + 2 * N_PLACEMENTS * ATTN_TQ * n_keys * 4


def _a_band_table():
    tables = []
    for blk in PLACEMENT_BLOCKS:
        q0 = blk * ATTN_TQ
        kpos = _window_start(q0, A_LEAD, A_KEYS) + np.arange(A_KEYS)[None, :]
        qpos = q0 + np.arange(ATTN_TQ)[:, None]
        tables.append(np.where(np.abs(kpos - qpos) <= A_WINDOW, 0.0, NEG_INF))
    return jnp.asarray(np.stack(tables), F32)


def _attn_a(qkv, sink):
    seq_block = (None, SEQ, HEAD_DIM)
    ctx_block = (None, CTX_LEN, HEAD_DIM)
    ctx_row0 = N_LAT // CTX_LEN
    return pl.pallas_call(
        functools.partial(_local_attn_kernel, A_LEAD, A_KEYS, True),
        grid=(BATCH, A_KV_HEADS, A_GROUP),
        in_specs=[
            pl.BlockSpec(memory_space=pltpu.SMEM),
            pl.BlockSpec(seq_block, lambda b, kv, g: (HEAD_AQ + kv * A_GROUP + g, b, 0)),
            pl.BlockSpec(seq_block, lambda b, kv, g: (HEAD_AK + kv, b, 0)),
            pl.BlockSpec(seq_block, lambda b, kv, g: (HEAD_AV + kv, b, 0)),
            pl.BlockSpec(ctx_block, lambda b, kv, g: (HEAD_AK + kv, ctx_row0 + b, 0)),
            pl.BlockSpec(ctx_block, lambda b, kv, g: (HEAD_AV + kv, ctx_row0 + b, 0)),
            pl.BlockSpec((N_PLACEMENTS, ATTN_TQ, A_KEYS), lambda b, kv, g: (0, 0, 0)),
        ],
        out_specs=pl.BlockSpec((SEQ, HEAD_DIM), lambda b, kv, g: (b, kv * A_GROUP + g)),
        out_shape=jax.ShapeDtypeStruct((N_LAT, (A_Q_HEADS + B_HEADS) * HEAD_DIM), BF16),
        compiler_params=pltpu.CompilerParams(
            dimension_semantics=("arbitrary", "arbitrary", "arbitrary"),
            vmem_limit_bytes=_vmem_limit(_local_attn_vmem(A_KEYS))),
        name="attn_window",
    )(sink, qkv, qkv, qkv, qkv, qkv, _a_band_table())


def _b_bias_tables(rpb):
    j = np.arange(GRID_W)[:, None]
    jk = np.arange(GRID_W)[None, :]
    col_start = np.clip(j - B_WIN_W // 2, 0, GRID_W - B_WIN_W)
    inside = (jk >= col_start) & (jk < col_start + B_WIN_W)
    n_col = 2 * B_WIN_W - 1
    onehot = (inside[:, :, None] & ((jk - j + (B_WIN_W - 1))[:, :, None] == np.arange(n_col))).astype(np.float32)
    by_row = jnp.einsum('lhac,jkc->lhajk', rpb, jnp.asarray(onehot), precision=lax.Precision.HIGHEST)
    by_row = by_row + jnp.asarray(np.where(inside, 0.0, NEG_INF), F32)
    masked = jnp.full((DEPTH, B_HEADS, GRID_W, GRID_W), NEG_INF, F32)
    query_rows = []
    for blk in PLACEMENT_BLOCKS:
        r0 = blk * B_ROWS_PER_BLOCK
        u0 = _window_start(r0 * GRID_W, B_LEAD, B_KEYS) // GRID_W
        for t in range(B_ROWS_PER_BLOCK):
            r = r0 + t
            rs = int(np.clip(r - B_WIN_H // 2, 0, GRID_ROWS - B_WIN_H))
            before = rs - u0
            assert 0 <= before <= B_UNION_ROWS - B_WIN_H
            bias_row0 = rs - r + (B_WIN_H - 1)
            pieces = [by_row[:, :, bias_row0 + u - before] if before <= u < before + B_WIN_H else masked
                      for u in range(B_UNION_ROWS)]
            query_rows.append(jnp.concatenate(pieces, axis=-1))
    return jnp.stack(query_rows, axis=2).reshape(DEPTH, B_HEADS, N_PLACEMENTS, ATTN_TQ, B_KEYS)


def _attn_b(qkv, bias, layer, o_ab):
    seq_block = (None, SEQ, HEAD_DIM)
    ctx_block = (None, CTX_LEN, HEAD_DIM)
    ctx_row0 = N_LAT // CTX_LEN
    return pl.pallas_call(
        functools.partial(_local_attn_kernel, B_LEAD, B_KEYS, False),
        grid=(BATCH, B_HEADS),
        in_specs=[
            pl.BlockSpec(seq_block, lambda b, h: (HEAD_BQ + h, b, 0)),
            pl.BlockSpec(seq_block, lambda b, h: (HEAD_BK + h, b, 0)),
            pl.BlockSpec(seq_block, lambda b, h: (HEAD_BV + h, b, 0)),
            pl.BlockSpec(ctx_block, lambda b, h: (HEAD_BK + h, ctx_row0 + b, 0)),
            pl.BlockSpec(ctx_block, lambda b, h: (HEAD_BV + h, ctx_row0 + b, 0)),
            pl.BlockSpec((None, None, N_PLACEMENTS, ATTN_TQ, B_KEYS), lambda b, h: (layer, h, 0, 0, 0)),
            pl.BlockSpec(memory_space=pl.ANY),
        ],
        out_specs=pl.BlockSpec((SEQ, HEAD_DIM), lambda b, h: (b, A_Q_HEADS + h)),
        out_shape=jax.ShapeDtypeStruct(o_ab.shape, BF16),
        input_output_aliases={6: 0},
        compiler_params=pltpu.CompilerParams(
            dimension_semantics=("arbitrary", "arbitrary"),
            vmem_limit_bytes=_vmem_limit(_local_attn_vmem(B_KEYS))),
        name="attn_neighbourhood",
    )(qkv, qkv, qkv, qkv, qkv, bias, o_ab)


N_CTX_HEADS = A_Q_HEADS + B_HEADS


def _attn_ctx_kernel(sink_ref, q_ref, k_ref, v_ref, o_ref):
    sink = sink_ref[pl.program_id(1)]
    q = q_ref[...]
    s = _dot_nt(q, k_ref[...])
    m = jnp.maximum(jnp.max(s, axis=-1, keepdims=True), sink)
    p = jnp.exp(s - m)
    den = jnp.sum(p, axis=-1, keepdims=True) + jnp.exp(sink - m)
    o = jnp.dot(p.astype(BF16), v_ref[...], preferred_element_type=F32)
    o_ref[...] = (o / den).astype(BF16)


def _attn_ctx(qkv, sinks):
    ctx_block = (None, CTX_LEN, HEAD_DIM)
    ctx_row0 = N_LAT // CTX_LEN
    is_b = lambda h: h >= A_Q_HEADS
    q_head = lambda h: jnp.where(is_b(h), HEAD_BQ + h - A_Q_HEADS, HEAD_AQ + h)
    k_head = lambda h: jnp.where(is_b(h), HEAD_BK + h - A_Q_HEADS, HEAD_AK + h // A_GROUP)
    v_head = lambda h: jnp.where(is_b(h), HEAD_BV + h - A_Q_HEADS, HEAD_AV + h // A_GROUP)
    return pl.pallas_call(
        _attn_ctx_kernel,
        grid=(BATCH, N_CTX_HEADS),
        in_specs=[
            pl.BlockSpec(memory_space=pltpu.SMEM),
            pl.BlockSpec(ctx_block, lambda b, h: (q_head(h), ctx_row0 + b, 0)),
            pl.BlockSpec(ctx_block, lambda b, h: (k_head(h), ctx_row0 + b, 0)),
            pl.BlockSpec(ctx_block, lambda b, h: (v_head(h), ctx_row0 + b, 0)),
        ],
        out_specs=pl.BlockSpec((CTX_LEN, HEAD_DIM), lambda b, h: (b, h)),
        out_shape=jax.ShapeDtypeStruct((N_CTX, N_CTX_HEADS * HEAD_DIM), BF16),
        compiler_params=pltpu.CompilerParams(dimension_semantics=("arbitrary", "arbitrary")),
        name="attn_context",
    )(sinks, qkv, qkv, qkv)


CONV_ROWS = 32
CONV_SH_ROWS = TC + SUBLANES * ((C_CONV_WIDTH - 1) // SUBLANES)
assert SUBLANES - 1 + CONV_HALO - C_PAD + CONV_SH_ROWS <= TC + 2 * CONV_HALO


def _glu(u):
    return u[:, :C_CHANNELS] * jax.nn.sigmoid(u[:, C_CHANNELS:])


def _conv_fill(n_rows, first, last, prev_ref, cur_ref, next_ref, ext_ref):
    ext_ref[0:CONV_HALO, :] = jnp.where(first, 0.0, _glu(prev_ref[...]))
    ext_ref[CONV_HALO:CONV_HALO + n_rows, :] = _glu(cur_ref[...])
    ext_ref[CONV_HALO + n_rows:, :] = jnp.where(last, 0.0, _glu(next_ref[...]))


def _conv_shift(row0, ext_ref, sh_ref):
    for b in range(SUBLANES):
        lo = row0 + b + CONV_HALO - C_PAD
        sh_ref[b] = ext_ref[lo:lo + CONV_SH_ROWS, :]


def _conv_chunk(row0, c, sh_ref, w_ref, b_ref, g_ref, beta_ref, out_ref):
    groups = CONV_ROWS // SUBLANES
    r0 = c * CONV_ROWS
    accs = [jnp.zeros((SUBLANES, C_CHANNELS), F32) for _ in range(groups)]
    for k in range(C_CONV_WIDTH):
        a, b = divmod(k, SUBLANES)
        w_k = w_ref[k]
        for g in range(groups):
            lo = r0 + SUBLANES * (a + g)
            accs[g] = accs[g] + sh_ref[b, lo:lo + SUBLANES, :] * w_k
    acc = jnp.concatenate(accs, axis=0) + b_ref[...]
    mu = jnp.mean(acc, axis=-1, keepdims=True)
    xc = acc - mu
    y = xc * lax.rsqrt(jnp.mean(xc * xc, axis=-1, keepdims=True) + NORM_EPS)
    y = y * g_ref[...] + beta_ref[...]
    out_ref[row0 + r0:row0 + r0 + CONV_ROWS, :] = (y * jax.nn.sigmoid(y)).astype(BF16)


CONV_CHUNKS = TC // CONV_ROWS


def _conv_operands(cu, n_rows, tile0, dw_w, dw_b, ln_g, ln_b):
    halo_per_tile = n_rows // CONV_HALO
    n_halo = cu.shape[0] // CONV_HALO
    vec = pl.BlockSpec((1, C_CHANNELS), lambda i: (0, 0))
    specs = [
        pl.BlockSpec((CONV_HALO, 2 * C_CHANNELS), lambda i: (jnp.maximum((tile0 + i) * halo_per_tile - 1, 0), 0)),
        pl.BlockSpec((n_rows, 2 * C_CHANNELS), lambda i: (tile0 + i, 0)),
        pl.BlockSpec((CONV_HALO, 2 * C_CHANNELS),
                     lambda i: (jnp.minimum((tile0 + i + 1) * halo_per_tile, n_halo - 1), 0)),
        pl.BlockSpec((C_CONV_WIDTH, SUBLANES, C_CHANNELS), lambda i: (0, 0, 0)),
        vec, vec, vec,
    ]
    operands = [cu, cu, cu, jnp.broadcast_to(dw_w[:, None, :], (C_CONV_WIDTH, SUBLANES, C_CHANNELS)),
                dw_b.reshape(1, -1), ln_g.reshape(1, -1), ln_b.reshape(1, -1)]
    return operands, specs


def _conv_scratch(n_rows):
    return [pltpu.VMEM((n_rows + 2 * CONV_HALO, C_CHANNELS), F32),
            pltpu.VMEM((SUBLANES, CONV_SH_ROWS, C_CHANNELS), F32)]


def _conv_ctx_kernel(prev_ref, cur_ref, next_ref, w_ref, b_ref, g_ref, beta_ref, o_ref, ext_ref, sh_ref):
    _conv_fill(TC, True, True, prev_ref, cur_ref, next_ref, ext_ref)
    _conv_shift(0, ext_ref, sh_ref)
    for c in range(CONV_CHUNKS):
        _conv_chunk(0, c, sh_ref, w_ref, b_ref, g_ref, beta_ref, o_ref)


def _conv_ctx(cu, dw_w, dw_b, ln_g, ln_b):
    assert CTX_LEN == TC
    operands, specs = _conv_operands(cu, TC, N_LAT // TC, dw_w, dw_b, ln_g, ln_b)
    return pl.pallas_call(
        _conv_ctx_kernel,
        grid=(N_CTX // TC,),
        in_specs=specs,
        out_specs=pl.BlockSpec((TC, C_CHANNELS), lambda i: (i, 0)),
        out_shape=jax.ShapeDtypeStruct((N_CTX, C_CHANNELS), BF16),
        scratch_shapes=_conv_scratch(TC),
        compiler_params=pltpu.CompilerParams(dimension_semantics=("arbitrary",)),
        name="conv_context",
    )(*operands)


N_CONV_OPERANDS = 7
OUTPROJ_SLABS = 8
assert ((TM // TC) * CONV_CHUNKS) % OUTPROJ_SLABS == 0


def _outproj_kernel(widths, fused_conv, x_ref, mod_ref, w_ref, *refs):
    if fused_conv:
        pieces = list(refs[:len(widths) - 1])
        conv_refs = refs[len(widths) - 1:len(widths) - 1 + N_CONV_OPERANDS]
        o_ref, ext_ref, sh_ref, oc_ref = refs[len(widths) - 1 + N_CONV_OPERANDS:]
    else:
        pieces, o_ref = list(refs[:-1]), refs[-1]
    def projected(cols):
        y = None
        k0 = 0
        for piece, width in zip(pieces, widths):
            t = jnp.dot(piece[...], w_ref[k0:k0 + width, cols], preferred_element_type=F32)
            y = t if y is None else y + t
            k0 += width
        return y

    if not fused_conv:
        o_ref[...] = x_ref[...] + mod_ref[2:3, :] * projected(slice(None))
        return

    prev_ref, cur_ref, next_ref, cw_ref, cb_ref, cg_ref, cbeta_ref = conv_refs
    i = pl.program_id(0)
    _conv_fill(TM, i % TILES_PER_BATCH == 0, i % TILES_PER_BATCH == TILES_PER_BATCH - 1,
               prev_ref, cur_ref, next_ref, ext_ref)
    n_chunks = (TM // TC) * CONV_CHUNKS
    slab = D_MODEL // OUTPROJ_SLABS
    chunks_per_slab = n_chunks // OUTPROJ_SLABS
    for s in range(OUTPROJ_SLABS):
        cols = slice(s * slab, (s + 1) * slab)
        o_ref[:, cols] = projected(cols)
        for chunk in range(s * chunks_per_slab, (s + 1) * chunks_per_slab):
            sub, c = divmod(chunk, CONV_CHUNKS)
            if c == 0:
                _conv_shift(sub * TC, ext_ref, sh_ref)
            _conv_chunk(sub * TC, c, sh_ref, cw_ref, cb_ref, cg_ref, cbeta_ref, oc_ref)
    k_conv = MIX_WIDTH - widths[-1]
    y = o_ref[...] + jnp.dot(oc_ref[...], w_ref[k_conv:, :], preferred_element_type=F32)
    o_ref[...] = x_ref[...] + mod_ref[2:3, :] * y


def _outproj(xs, mod_l, w_out, layer, pieces, tile0, n_tiles, conv=None):
    widths = tuple(int(p.shape[1]) for p, _ in pieces) + ((C_CHANNELS,) if conv is not None else ())
    assert sum(widths) == MIX_WIDTH
    blocks = 4 * TM * D_MODEL * 4 + MIX_WIDTH * D_MODEL * 2 + 2 * TM * MIX_WIDTH * 2
    operands = [xs, mod_l, w_out] + [p for p, _ in pieces]
    in_specs = [
        pl.BlockSpec((TM, D_MODEL), lambda i: (tile0 + i, 0)),
        pl.BlockSpec((None, None, 3, D_MODEL), lambda i: (_group_of_tile(tile0 + i), 1, 0, 0)),
        pl.BlockSpec((None, MIX_WIDTH, D_MODEL), lambda i: (layer, 0, 0), pipeline_mode=pl.Buffered(1)),
    ] + [pl.BlockSpec((TM, w), functools.partial(lambda i, r0: (r0 + i, 0), r0=r0))
         for w, (_, r0) in zip(widths, pieces)]
    scratch = []
    if conv is not None:
        assert tile0 == 0 and n_tiles == LAT_TILES and TM % TC == 0
        conv_operands, conv_specs = _conv_operands(conv[0], TM, 0, *conv[1:])
        operands += conv_operands
        in_specs += conv_specs
        scratch = _conv_scratch(TM) + [pltpu.VMEM((TM, C_CHANNELS), BF16)]
        blocks += 4 * TM * C_CHANNELS * 4 + (TM + SUBLANES * CONV_SH_ROWS) * C_CHANNELS * 4
    return pl.pallas_call(
        functools.partial(_outproj_kernel, widths, conv is not None),
        grid=(n_tiles,),
        in_specs=in_specs,
        out_specs=pl.BlockSpec((TM, D_MODEL), lambda i: (tile0 + i, 0)),
        out_shape=jax.ShapeDtypeStruct((NT, D_MODEL), F32),
        scratch_shapes=scratch,
        input_output_aliases={0: 0},
        compiler_params=pltpu.CompilerParams(
            dimension_semantics=("arbitrary",),
            vmem_limit_bytes=_vmem_limit(blocks)),
        name="outproj",
    )(*operands)


def kernel(x, c, ctx, c_ctx, w_mod, b_mod, norm_ffn1, norm_mix, norm_ffn2, ffn1_w_gate, ffn1_w_up,
           ffn1_w_down, ffn2_w_gate, ffn2_w_up, ffn2_w_down, w_in, w_out, a_q_norm, a_k_norm, a_sink,
           b_q_norm, b_k_norm, b_rpb, c_dw_w, c_dw_b, c_ln_g, c_ln_b):
    mods = _mod_vectors(c, c_ctx, w_mod, b_mod)
    cos_t, sin_t = _rope_tables()
    b_bias = _b_bias_tables(b_rpb)
    ffn1_w = [w.astype(BF16) for w in (ffn1_w_gate, ffn1_w_up, ffn1_w_down)]
    ffn2_w = [w.astype(BF16) for w in (ffn2_w_gate, ffn2_w_up, ffn2_w_down)]
    w_in_bf = w_in.astype(BF16)
    w_out_bf = w_out.astype(BF16)
    q_scale = HEAD_DIM ** -0.5
    xs = None
    for l in range(DEPTH):
        last = l == DEPTH - 1
        mod_l = mods[l]
        if l == 0:
            xs = _ffn_streams(x.reshape(N_LAT, D_MODEL), ctx.reshape(N_CTX, D_MODEL), mod_l, 0, norm_ffn1[l],
                              *ffn1_w, l, in_place=False, out_rows=NT)
        else:
            xs = _ffn_streams(xs, xs, mod_l, 0, norm_ffn1[l], *ffn1_w, l, in_place=True, out_rows=NT)
        head_gains = jnp.concatenate([
            (a_q_norm[l] * q_scale)[None], a_k_norm[l][None], (b_q_norm[l] * q_scale)[None],
            b_k_norm[l][None], jnp.zeros((SUBLANES - 4, HEAD_DIM), F32)], axis=0)
        qkv, cu = _inproj(xs, mod_l, norm_mix[l], w_in_bf, l, head_gains, cos_t, sin_t)
        o_ab = _attn_b(qkv, b_bias, l, _attn_a(qkv, a_sink[l]))
        conv_params = (c_dw_w[l], c_dw_b[l], c_ln_g[l], c_ln_b[l])
        xs = _outproj(xs, mod_l, w_out_bf, l, [(o_ab, 0)], 0, LAT_TILES, conv=(cu,) + conv_params)
        if not last:
            sinks = jnp.concatenate([a_sink[l], jnp.full((B_HEADS,), NEG_INF, F32)])
            o_ctx = _attn_ctx(qkv, sinks)
            o_c_ctx = _conv_ctx(cu, *conv_params)
            xs = _outproj(xs, mod_l, w_out_bf, l, [(o_ctx, 0), (o_c_ctx, 0)], LAT_TILES, 1)
        if last:
            xs = _ffn_streams(xs, None, mod_l, 2, norm_ffn2[l], *ffn2_w, l, in_place=False, out_rows=N_LAT)
        else:
            xs = _ffn_streams(xs, xs, mod_l, 2, norm_ffn2[l], *ffn2_w, l, in_place=True, out_rows=NT)
    return xs.reshape(BATCH, SEQ, D_MODEL)
```

```python
import functools

import numpy as np
import jax
import jax.numpy as jnp
from jax import lax
from jax.experimental import pallas as pl
from jax.experimental.pallas import tpu as pltpu

D_MODEL = 2048
BATCH = 2
SEQ = 16384
DEPTH = 2
GRID_W = 64
GRID_ROWS = SEQ // GRID_W
CTX_LEN = 256
HEAD_DIM = 128
A_Q_HEADS = 6
A_KV_HEADS = 2
A_GROUP = A_Q_HEADS // A_KV_HEADS
A_WINDOW = 128
B_HEADS = 6
B_WIN_H = 8
B_WIN_W = 16
C_CHANNELS = 512
C_CONV_WIDTH = 31
C_PAD = (C_CONV_WIDTH - 1) // 2
D_FF = 5632
ROPE_THETA = 10000.0
NORM_EPS = 1e-6
NEG_INF = -1e30
N_MOD = 9
IN_COLS = 4608
MIX_WIDTH = 2048

HEAD_AQ = 0
HEAD_AK = HEAD_AQ + A_Q_HEADS
HEAD_AV = HEAD_AK + A_KV_HEADS
HEAD_BQ = HEAD_AV + A_KV_HEADS
HEAD_BK = HEAD_BQ + B_HEADS
HEAD_BV = HEAD_BK + B_HEADS
N_QKV_HEADS = HEAD_BV + B_HEADS
C_COL0 = N_QKV_HEADS * HEAD_DIM

N_LAT = BATCH * SEQ
N_CTX = BATCH * CTX_LEN
NT = N_LAT + N_CTX

V7X_VMEM_BYTES = 64 * 1024 * 1024
SUBLANES = 8
LANES = 128

TM = 512
LAT_TILES = N_LAT // TM
TILES_PER_BATCH = SEQ // TM
ALL_TILES = NT // TM
TF = 512
TC = 256
CONV_HALO = 16

F32 = jnp.float32
BF16 = jnp.bfloat16


def _vmem_limit(block_bytes):
    return int(min(V7X_VMEM_BYTES - 4 * 1024 * 1024, block_bytes + 12 * 1024 * 1024))


def _group_of_tile(i):
    return i // TILES_PER_BATCH


def _modulated(x, gain, shift, scale):
    ms = jnp.mean(x * x, axis=-1, keepdims=True)
    return (x * lax.rsqrt(ms + NORM_EPS) * gain) * (1.0 + scale) + shift


MOD_TN = 1024
MOD_GROUPS = BATCH + 1
MOD_UNROLL = 8


def _mod_kernel(c_ref, w_ref, b_ref, o_ref, a_ref):
    @pl.when(jnp.logical_and(pl.program_id(0) == 0, pl.program_id(1) == 0))
    def _():
        c = c_ref[...]
        a_ref[...] = c * jax.nn.sigmoid(c)

    lane_tiles = MOD_TN // LANES

    def body(step, accs):
        accs = list(accs)
        for u in range(MOD_UNROLL):
            r = pl.multiple_of((step * MOD_UNROLL + u) * SUBLANES, SUBLANES)
            for m in range(MOD_GROUPS):
                a = a_ref[m, pl.ds(r, SUBLANES), :]
                for t in range(lane_tiles):
                    idx = m * lane_tiles + t
                    accs[idx] = accs[idx] + w_ref[pl.ds(r, SUBLANES), t * LANES:(t + 1) * LANES] * a
        return tuple(accs)

    zero = jnp.zeros((SUBLANES, LANES), F32)
    accs = lax.fori_loop(0, D_MODEL // (SUBLANES * MOD_UNROLL), body, (zero,) * (MOD_GROUPS * lane_tiles))
    o_ref[...] = jnp.zeros_like(o_ref)
    for m in range(MOD_GROUPS):
        row = jnp.concatenate([jnp.sum(accs[m * lane_tiles + t], axis=0, keepdims=True)
                               for t in range(lane_tiles)], axis=1)
        o_ref[m:m + 1, :] = row + b_ref[...]


def _mod_vectors(c, c_ctx, w_mod, b_mod):
    rows = jnp.concatenate([c, c_ctx[None, :]], axis=0)
    c_cols = jnp.broadcast_to(rows[:, :, None], (MOD_GROUPS, D_MODEL, LANES))
    n = N_MOD * D_MODEL
    out = pl.pallas_call(
        _mod_kernel,
        grid=(DEPTH, n // MOD_TN),
        in_specs=[
            pl.BlockSpec((MOD_GROUPS, D_MODEL, LANES), lambda l, j: (0, 0, 0)),
            pl.BlockSpec((None, D_MODEL, MOD_TN), lambda l, j: (l, 0, j)),
            pl.BlockSpec((None, 1, MOD_TN), lambda l, j: (l, 0, j)),
        ],
        out_specs=pl.BlockSpec((None, SUBLANES, MOD_TN), lambda l, j: (l, 0, j)),
        out_shape=jax.ShapeDtypeStruct((DEPTH, SUBLANES, n), F32),
        scratch_shapes=[pltpu.VMEM((MOD_GROUPS, D_MODEL, LANES), F32)],
        compiler_params=pltpu.CompilerParams(
            dimension_semantics=("arbitrary", "arbitrary"),
            vmem_limit_bytes=_vmem_limit(2 * D_MODEL * MOD_TN * 4 + 3 * MOD_GROUPS * D_MODEL * LANES * 4)),
        name="mod_vectors",
    )(c_cols, w_mod, b_mod.reshape(DEPTH, 1, n))
    return out[:, :MOD_GROUPS].reshape(DEPTH, MOD_GROUPS, 3, 3, D_MODEL)


FFN_SUB = 512


def _ffn_kernel(tm, x_ref, mod_ref, gain_ref, wg_ref, wu_ref, wd_ref, *rest):
    o_ref, h_ref = rest[-2:]
    j = pl.program_id(1)

    last = pl.num_programs(1) - 1

    def chunk(first, final):
        for r0 in range(0, tm, FFN_SUB):
            rows = slice(r0, r0 + FFN_SUB)
            if first:
                h = _modulated(x_ref[rows, :], gain_ref[...], mod_ref[0:1, :], mod_ref[1:2, :])
                h_ref[rows, :] = h.astype(BF16)
            h = h_ref[rows, :]
            g = jnp.dot(h, wg_ref[...], preferred_element_type=F32)
            u = jnp.dot(h, wu_ref[...], preferred_element_type=F32)
            a = (g * jax.nn.sigmoid(g)) * u
            d = jnp.dot(a.astype(BF16), wd_ref[...], preferred_element_type=F32)
            if first:
                o_ref[rows, :] = d
            elif final:
                o_ref[rows, :] = x_ref[rows, :] + (0.5 * mod_ref[2:3, :]) * (o_ref[rows, :] + d)
            else:
                o_ref[rows, :] += d

    pl.when(j == 0)(functools.partial(chunk, True, False))
    pl.when(jnp.logical_and(j > 0, j < last))(functools.partial(chunk, False, False))
    pl.when(j == last)(functools.partial(chunk, False, True))


def _ffn(x, dest, mod_l, sub, gain, wg, wu, wd, layer, tm, x_tile0, out_tile0, n_tiles, out_rows):
    assert tm % FFN_SUB == 0 and SEQ % tm == 0 and D_FF // TF >= 2
    blocks = 4 * tm * D_MODEL * 4 + 6 * D_MODEL * TF * 2 + tm * D_MODEL * 2
    operands = [x, mod_l, gain.reshape(1, D_MODEL), wg, wu, wd]
    in_specs = [
        pl.BlockSpec((tm, D_MODEL), lambda i, j: (x_tile0 + i, 0)),
        pl.BlockSpec((None, None, 3, D_MODEL), lambda i, j: ((out_tile0 + i) * tm // SEQ, sub, 0, 0)),
        pl.BlockSpec((1, D_MODEL), lambda i, j: (0, 0)),
        pl.BlockSpec((None, D_MODEL, TF), lambda i, j: (layer, 0, j)),
        pl.BlockSpec((None, D_MODEL, TF), lambda i, j: (layer, 0, j)),
        pl.BlockSpec((None, TF, D_MODEL), lambda i, j: (layer, j, 0)),
    ]
    aliases = {}
    if dest is x:
        aliases = {0: 0}
    elif dest is not None:
        operands.append(dest)
        in_specs.append(pl.BlockSpec(memory_space=pl.ANY))
        aliases = {len(operands) - 1: 0}
    return pl.pallas_call(
        functools.partial(_ffn_kernel, tm),
        grid=(n_tiles, D_FF // TF),
        in_specs=in_specs,
        out_specs=pl.BlockSpec((tm, D_MODEL), lambda i, j: (out_tile0 + i, 0)),
        out_shape=jax.ShapeDtypeStruct((out_rows, D_MODEL), F32),
        scratch_shapes=[pltpu.VMEM((tm, D_MODEL), BF16)],
        input_output_aliases=aliases,
        compiler_params=pltpu.CompilerParams(
            dimension_semantics=("arbitrary", "arbitrary"),
            vmem_limit_bytes=_vmem_limit(blocks)),
        name="ffn",
    )(*operands)


FFN_TM_LAT = 1024
FFN_LAT_TILES = N_LAT // FFN_TM_LAT


def _ffn_streams(x_lat, x_ctx, *args, in_place, out_rows):
    combined = x_ctx is x_lat
    out = _ffn(x_lat, x_lat if in_place else None, *args, FFN_TM_LAT, 0, 0, FFN_LAT_TILES, out_rows)
    if x_ctx is not None:
        src = out if combined else x_ctx
        out = _ffn(src, out, *args, TM, LAT_TILES if combined else 0, LAT_TILES, 1, out_rows)
    return out


GAIN_AQ, GAIN_AK, GAIN_BQ, GAIN_BK = 0, 1, 2, 3


def _head_kind(h):
    if h < HEAD_AK:
        return GAIN_AQ, True
    if h < HEAD_AV:
        return GAIN_AK, True
    if h < HEAD_BQ:
        return None, False
    if h < HEAD_BK:
        return GAIN_BQ, False
    if h < HEAD_BV:
        return GAIN_BK, False
    return None, False


def _inproj_kernel(x_ref, mod_ref, gain_ref, w_ref, hg_ref, cos_ref, sin_ref, qkv_ref, cu_ref, h_ref):
    h = _modulated(x_ref[...], gain_ref[...], mod_ref[0:1, :], mod_ref[1:2, :])
    h_ref[...] = h.astype(BF16)
    lane = lax.broadcasted_iota(jnp.int32, (TM, HEAD_DIM), 1)
    even_quarter = ((lane // (HEAD_DIM // 4)) % 2) == 0
    for pair in range(N_QKV_HEADS // 2):
        c0 = pair * 2 * HEAD_DIM
        y2 = jnp.dot(h_ref[...], w_ref[:, c0:c0 + 2 * HEAD_DIM], preferred_element_type=F32)
        for half in range(2):
            hd = 2 * pair + half
            y = y2[:, half * HEAD_DIM:(half + 1) * HEAD_DIM]
            gain_row, rotary = _head_kind(hd)
            if gain_row is not None:
                ms = jnp.mean(y * y, axis=-1, keepdims=True)
                y = y * lax.rsqrt(ms + NORM_EPS) * hg_ref[gain_row:gain_row + 1, :]
            if rotary:
                swapped = jnp.where(even_quarter,
                                    pltpu.roll(y, HEAD_DIM - HEAD_DIM // 4, 1),
                                    pltpu.roll(y, HEAD_DIM // 4, 1))
                y = y * cos_ref[...] + swapped * sin_ref[...]
            qkv_ref[hd] = y.astype(BF16)
    for blk in range((IN_COLS - C_COL0) // (2 * HEAD_DIM)):
        c0 = C_COL0 + blk * 2 * HEAD_DIM
        cu_ref[:, blk * 2 * HEAD_DIM:(blk + 1) * 2 * HEAD_DIM] = jnp.dot(
            h_ref[...], w_ref[:, c0:c0 + 2 * HEAD_DIM], preferred_element_type=F32)


def _inproj(xs, mod_l, gain, w_in, layer, head_gains, cos_t, sin_t):
    n_cu = IN_COLS - C_COL0
    blocks = (2 * TM * D_MODEL * 4 + D_MODEL * IN_COLS * 2 + 2 * N_QKV_HEADS * TM * HEAD_DIM * 2
              + 2 * TM * n_cu * 4 + 4 * TM * HEAD_DIM * 4 + TM * D_MODEL * 2)
    rope_block = lambda i: (jnp.where(i < LAT_TILES, i % TILES_PER_BATCH, TILES_PER_BATCH), 0)
    return pl.pallas_call(
        _inproj_kernel,
        grid=(ALL_TILES,),
        in_specs=[
            pl.BlockSpec((TM, D_MODEL), lambda i: (i, 0)),
            pl.BlockSpec((None, None, 3, D_MODEL), lambda i: (_group_of_tile(i), 1, 0, 0)),
            pl.BlockSpec((1, D_MODEL), lambda i: (0, 0)),
            pl.BlockSpec((None, D_MODEL, IN_COLS), lambda i: (layer, 0, 0), pipeline_mode=pl.Buffered(1)),
            pl.BlockSpec((SUBLANES, HEAD_DIM), lambda i: (0, 0)),
            pl.BlockSpec((TM, HEAD_DIM), rope_block),
            pl.BlockSpec((TM, HEAD_DIM), rope_block),
        ],
        out_specs=[
            pl.BlockSpec((N_QKV_HEADS, TM, HEAD_DIM), lambda i: (0, i, 0)),
            pl.BlockSpec((TM, n_cu), lambda i: (i, 0)),
        ],
        out_shape=[
            jax.ShapeDtypeStruct((N_QKV_HEADS, NT, HEAD_DIM), BF16),
            jax.ShapeDtypeStruct((NT, n_cu), F32),
        ],
        scratch_shapes=[pltpu.VMEM((TM, D_MODEL), BF16)],
        compiler_params=pltpu.CompilerParams(
            dimension_semantics=("arbitrary",),
            vmem_limit_bytes=_vmem_limit(blocks)),
        name="inproj",
    )(xs, mod_l, gain.reshape(1, D_MODEL), w_in, head_gains, cos_t, sin_t)


def _rope_tables():
    t = np.arange(SEQ)
    n_freq = HEAD_DIM // 4
    inv_freq = ROPE_THETA ** (-np.arange(n_freq, dtype=np.float64) / n_freq)
    ang_r = (t // GRID_W)[:, None] * inv_freq[None, :]
    ang_c = (t % GRID_W)[:, None] * inv_freq[None, :]
    cos_t = np.concatenate([np.cos(ang_r), np.cos(ang_r), np.cos(ang_c), np.cos(ang_c)], axis=-1)
    sin_t = np.concatenate([-np.sin(ang_r), np.sin(ang_r), -np.sin(ang_c), np.sin(ang_c)], axis=-1)
    cos_t = np.concatenate([cos_t, np.ones((TM, HEAD_DIM))], axis=0)
    sin_t = np.concatenate([sin_t, np.zeros((TM, HEAD_DIM))], axis=0)
    return jnp.asarray(cos_t, F32), jnp.asarray(sin_t, F32)


LOG2_E = float(np.log2(np.e))
ATTN_TQ = 256
ATTN_BLOCKS = SEQ // ATTN_TQ
ATTN_UNROLL = 8
PLACE_INTERIOR, PLACE_FIRST, PLACE_LAST = 0, 1, 2
N_PLACEMENTS = 3
PLACEMENT_BLOCKS = (1, 0, ATTN_BLOCKS - 1)

A_LEAD = A_WINDOW
A_KEYS = ATTN_TQ + 2 * A_WINDOW
B_ROWS_PER_BLOCK = ATTN_TQ // GRID_W
B_UNION_ROWS = B_ROWS_PER_BLOCK + B_WIN_H
B_LEAD = (B_WIN_H // 2) * GRID_W
B_KEYS = B_UNION_ROWS * GRID_W
assert ATTN_BLOCKS % ATTN_UNROLL == 0 and ATTN_BLOCKS >= 3
assert A_LEAD % LANES == 0 and B_LEAD % LANES == 0 and A_KEYS % LANES == 0 and B_KEYS % LANES == 0


def _dot_nt(a, b):
    return lax.dot_general(a, b, (((1,), (1,)), ((), ())), preferred_element_type=F32)


def _window_start(first_query, lead, n_keys):
    return int(np.clip(first_query - lead, 0, SEQ - n_keys))


def _local_attn_kernel(lead, n_keys, has_sink, *refs):
    if has_sink:
        sink_ref, refs = refs[0], refs[1:]
    q_ref, k_ref, v_ref, kc_ref, vc_ref, bias_ref = refs[:6]
    o_ref = refs[-1]
    sink = sink_ref[pl.program_id(1) * A_GROUP + pl.program_id(2)] if has_sink else None
    kc = kc_ref[...]
    vc = vc_ref[...]

    def scores(i):
        q0 = pl.multiple_of(i * ATTN_TQ, ATTN_TQ)
        k0 = pl.multiple_of(jnp.clip(q0 - lead, 0, SEQ - n_keys), LANES)
        placement = jnp.where(i == 0, PLACE_FIRST, jnp.where(i == ATTN_BLOCKS - 1, PLACE_LAST, PLACE_INTERIOR))
        q = q_ref[pl.ds(q0, ATTN_TQ), :]
        s = _dot_nt(q, k_ref[pl.ds(k0, n_keys), :]) + bias_ref[placement]
        return q0, k0, s, _dot_nt(q, kc)

    def softmax(q0, k0, s, sc):
        m = jnp.maximum(jnp.max(s, axis=-1, keepdims=True), jnp.max(sc, axis=-1, keepdims=True))
        if has_sink:
            m = jnp.maximum(m, sink)
        p = jnp.exp2(s - m)
        pc = jnp.exp2(sc - m)
        den = jnp.sum(p, axis=-1, keepdims=True) + jnp.sum(pc, axis=-1, keepdims=True)
        if has_sink:
            den = den + jnp.exp2(sink - m)
        return q0, k0, p.astype(BF16), pc.astype(BF16), den

    def output(q0, k0, p, pc, den):
        o = (jnp.dot(p, v_ref[pl.ds(k0, n_keys), :], preferred_element_type=F32)
             + jnp.dot(pc, vc, preferred_element_type=F32))
        o_ref[pl.ds(q0, ATTN_TQ), :] = (o / den).astype(BF16)

    def body(it, carry):
        scored, weighted = {}, {}
        for step in range(ATTN_UNROLL + 2):
            if step < ATTN_UNROLL:
                scored[step] = scores(it * ATTN_UNROLL + step)
            if 0 <= step - 2 < ATTN_UNROLL:
                output(*weighted.pop(step - 2))
            if 0 <= step - 1 < ATTN_UNROLL:
                weighted[step - 1] = softmax(*scored.pop(step - 1))
        return carry

    lax.fori_loop(0, ATTN_BLOCKS // ATTN_UNROLL, body, 0)


def _local_attn_vmem(n_keys):
    return 2 * 4 * SEQ * HEAD_DIM * 2 + 4 * CTX_LEN * HEAD_DIM * 2 + 2 * N_PLACEMENTS * ATTN_TQ * n_keys * 4


def _a_band_table():
    tables = []
    for blk in PLACEMENT_BLOCKS:
        q0 = blk * ATTN_TQ
        kpos = _window_start(q0, A_LEAD, A_KEYS) + np.arange(A_KEYS)[None, :]
        qpos = q0 + np.arange(ATTN_TQ)[:, None]
        tables.append(np.where(np.abs(kpos - qpos) <= A_WINDOW, 0.0, NEG_INF))
    return jnp.asarray(np.stack(tables), F32)


def _attn_a(qkv, sink):
    seq_block = (None, SEQ, HEAD_DIM)
    ctx_block = (None, CTX_LEN, HEAD_DIM)
    ctx_row0 = N_LAT // CTX_LEN
    return pl.pallas_call(
        functools.partial(_local_attn_kernel, A_LEAD, A_KEYS, True),
        grid=(BATCH, A_KV_HEADS, A_GROUP),
        in_specs=[
            pl.BlockSpec(memory_space=pltpu.SMEM),
            pl.BlockSpec(seq_block, lambda b, kv, g: (HEAD_AQ + kv * A_GROUP + g, b, 0)),
            pl.BlockSpec(seq_block, lambda b, kv, g: (HEAD_AK + kv, b, 0)),
            pl.BlockSpec(seq_block, lambda b, kv, g: (HEAD_AV + kv, b, 0)),
            pl.BlockSpec(ctx_block, lambda b, kv, g: (HEAD_AK + kv, ctx_row0 + b, 0)),
            pl.BlockSpec(ctx_block, lambda b, kv, g: (HEAD_AV + kv, ctx_row0 + b, 0)),
            pl.BlockSpec((N_PLACEMENTS, ATTN_TQ, A_KEYS), lambda b, kv, g: (0, 0, 0)),
        ],
        out_specs=pl.BlockSpec((SEQ, HEAD_DIM), lambda b, kv, g: (b, kv * A_GROUP + g)),
        out_shape=jax.ShapeDtypeStruct((N_LAT, (A_Q_HEADS + B_HEADS) * HEAD_DIM), BF16),
        compiler_params=pltpu.CompilerParams(
            dimension_semantics=("arbitrary", "arbitrary", "arbitrary"),
            vmem_limit_bytes=_vmem_limit(_local_attn_vmem(A_KEYS))),
        name="attn_window",
    )(sink, qkv, qkv, qkv, qkv, qkv, _a_band_table())


def _b_bias_tables(rpb):
    j = np.arange(GRID_W)[:, None]
    jk = np.arange(GRID_W)[None, :]
    col_start = np.clip(j - B_WIN_W // 2, 0, GRID_W - B_WIN_W)
    inside = (jk >= col_start) & (jk < col_start + B_WIN_W)
    n_col = 2 * B_WIN_W - 1
    onehot = (inside[:, :, None] & ((jk - j + (B_WIN_W - 1))[:, :, None] == np.arange(n_col))).astype(np.float32)
    by_row = jnp.einsum('lhac,jkc->lhajk', rpb, jnp.asarray(onehot), precision=lax.Precision.HIGHEST)
    by_row = by_row * LOG2_E + jnp.asarray(np.where(inside, 0.0, NEG_INF), F32)
    masked = jnp.full((DEPTH, B_HEADS, GRID_W, GRID_W), NEG_INF, F32)
    query_rows = []
    for blk in PLACEMENT_BLOCKS:
        r0 = blk * B_ROWS_PER_BLOCK
        u0 = _window_start(r0 * GRID_W, B_LEAD, B_KEYS) // GRID_W
        for t in range(B_ROWS_PER_BLOCK):
            r = r0 + t
            rs = int(np.clip(r - B_WIN_H // 2, 0, GRID_ROWS - B_WIN_H))
            before = rs - u0
            assert 0 <= before <= B_UNION_ROWS - B_WIN_H
            bias_row0 = rs - r + (B_WIN_H - 1)
            pieces = [by_row[:, :, bias_row0 + u - before] if before <= u < before + B_WIN_H else masked
                      for u in range(B_UNION_ROWS)]
            query_rows.append(jnp.concatenate(pieces, axis=-1))
    return jnp.stack(query_rows, axis=2).reshape(DEPTH, B_HEADS, N_PLACEMENTS, ATTN_TQ, B_KEYS)


def _attn_b(qkv, bias, layer, o_ab):
    seq_block = (None, SEQ, HEAD_DIM)
    ctx_block = (None, CTX_LEN, HEAD_DIM)
    ctx_row0 = N_LAT // CTX_LEN
    return pl.pallas_call(
        functools.partial(_local_attn_kernel, B_LEAD, B_KEYS, False),
        grid=(BATCH, B_HEADS),
        in_specs=[
            pl.BlockSpec(seq_block, lambda b, h: (HEAD_BQ + h, b, 0)),
            pl.BlockSpec(seq_block, lambda b, h: (HEAD_BK + h, b, 0)),
            pl.BlockSpec(seq_block, lambda b, h: (HEAD_BV + h, b, 0)),
            pl.BlockSpec(ctx_block, lambda b, h: (HEAD_BK + h, ctx_row0 + b, 0)),
            pl.BlockSpec(ctx_block, lambda b, h: (HEAD_BV + h, ctx_row0 + b, 0)),
            pl.BlockSpec((None, None, N_PLACEMENTS, ATTN_TQ, B_KEYS), lambda b, h: (layer, h, 0, 0, 0)),
            pl.BlockSpec(memory_space=pl.ANY),
        ],
        out_specs=pl.BlockSpec((SEQ, HEAD_DIM), lambda b, h: (b, A_Q_HEADS + h)),
        out_shape=jax.ShapeDtypeStruct(o_ab.shape, BF16),
        input_output_aliases={6: 0},
        compiler_params=pltpu.CompilerParams(
            dimension_semantics=("arbitrary", "arbitrary"),
            vmem_limit_bytes=_vmem_limit(_local_attn_vmem(B_KEYS))),
        name="attn_neighbourhood",
    )(qkv, qkv, qkv, qkv, qkv, bias, o_ab)


N_CTX_HEADS = A_Q_HEADS + B_HEADS


def _attn_ctx_kernel(sink_ref, q_ref, k_ref, v_ref, o_ref):
    sink = sink_ref[pl.program_id(1)]
    q = q_ref[...]
    s = _dot_nt(q, k_ref[...])
    m = jnp.maximum(jnp.max(s, axis=-1, keepdims=True), sink)
    p = jnp.exp2(s - m)
    den = jnp.sum(p, axis=-1, keepdims=True) + jnp.exp2(sink - m)
    o = jnp.dot(p.astype(BF16), v_ref[...], preferred_element_type=F32)
    o_ref[...] = (o / den).astype(BF16)


def _attn_ctx(qkv, sinks):
    ctx_block = (None, CTX_LEN, HEAD_DIM)
    ctx_row0 = N_LAT // CTX_LEN
    is_b = lambda h: h >= A_Q_HEADS
    q_head = lambda h: jnp.where(is_b(h), HEAD_BQ + h - A_Q_HEADS, HEAD_AQ + h)
    k_head = lambda h: jnp.where(is_b(h), HEAD_BK + h - A_Q_HEADS, HEAD_AK + h // A_GROUP)
    v_head = lambda h: jnp.where(is_b(h), HEAD_BV + h - A_Q_HEADS, HEAD_AV + h // A_GROUP)
    return pl.pallas_call(
        _attn_ctx_kernel,
        grid=(BATCH, N_CTX_HEADS),
        in_specs=[
            pl.BlockSpec(memory_space=pltpu.SMEM),
            pl.BlockSpec(ctx_block, lambda b, h: (q_head(h), ctx_row0 + b, 0)),
            pl.BlockSpec(ctx_block, lambda b, h: (k_head(h), ctx_row0 + b, 0)),
            pl.BlockSpec(ctx_block, lambda b, h: (v_head(h), ctx_row0 + b, 0)),
        ],
        out_specs=pl.BlockSpec((CTX_LEN, HEAD_DIM), lambda b, h: (b, h)),
        out_shape=jax.ShapeDtypeStruct((N_CTX, N_CTX_HEADS * HEAD_DIM), BF16),
        compiler_params=pltpu.CompilerParams(dimension_semantics=("arbitrary", "arbitrary")),
        name="attn_context",
    )(sinks, qkv, qkv, qkv)


CONV_ROWS = 32
CONV_SH_ROWS = TC + SUBLANES * ((C_CONV_WIDTH - 1) // SUBLANES)
assert SUBLANES - 1 + CONV_HALO - C_PAD + CONV_SH_ROWS <= TC + 2 * CONV_HALO


def _glu(u):
    return u[:, :C_CHANNELS] * jax.nn.sigmoid(u[:, C_CHANNELS:])


def _conv_fill(n_rows, first, last, prev_ref, cur_ref, next_ref, ext_ref):
    ext_ref[0:CONV_HALO, :] = jnp.where(first, 0.0, _glu(prev_ref[...]))
    ext_ref[CONV_HALO:CONV_HALO + n_rows, :] = _glu(cur_ref[...])
    ext_ref[CONV_HALO + n_rows:, :] = jnp.where(last, 0.0, _glu(next_ref[...]))


def _conv_shift(row0, ext_ref, sh_ref):
    for b in range(SUBLANES):
        lo = row0 + b + CONV_HALO - C_PAD
        sh_ref[b] = ext_ref[lo:lo + CONV_SH_ROWS, :]


def _conv_chunk(row0, c, sh_ref, w_ref, b_ref, g_ref, beta_ref, out_ref):
    groups = CONV_ROWS // SUBLANES
    r0 = c * CONV_ROWS
    accs = [jnp.zeros((SUBLANES, C_CHANNELS), F32) for _ in range(groups)]
    for k in range(C_CONV_WIDTH):
        a, b = divmod(k, SUBLANES)
        w_k = w_ref[k]
        for g in range(groups):
            lo = r0 + SUBLANES * (a + g)
            accs[g] = accs[g] + sh_ref[b, lo:lo + SUBLANES, :] * w_k
    acc = jnp.concatenate(accs, axis=0) + b_ref[...]
    mu = jnp.mean(acc, axis=-1, keepdims=True)
    xc = acc - mu
    y = xc * lax.rsqrt(jnp.mean(xc * xc, axis=-1, keepdims=True) + NORM_EPS)
    y = y * g_ref[...] + beta_ref[...]
    out_ref[row0 + r0:row0 + r0 + CONV_ROWS, :] = (y * jax.nn.sigmoid(y)).astype(BF16)


CONV_CHUNKS = TC // CONV_ROWS


def _conv_operands(cu, n_rows, tile0, dw_w, dw_b, ln_g, ln_b):
    halo_per_tile = n_rows // CONV_HALO
    n_halo = cu.shape[0] // CONV_HALO
    vec = pl.BlockSpec((1, C_CHANNELS), lambda i: (0, 0))
    specs = [
        pl.BlockSpec((CONV_HALO, 2 * C_CHANNELS), lambda i: (jnp.maximum((tile0 + i) * halo_per_tile - 1, 0), 0)),
        pl.BlockSpec((n_rows, 2 * C_CHANNELS), lambda i: (tile0 + i, 0)),
        pl.BlockSpec((CONV_HALO, 2 * C_CHANNELS),
                     lambda i: (jnp.minimum((tile0 + i + 1) * halo_per_tile, n_halo - 1), 0)),
        pl.BlockSpec((C_CONV_WIDTH, SUBLANES, C_CHANNELS), lambda i: (0, 0, 0)),
        vec, vec, vec,
    ]
    operands = [cu, cu, cu, jnp.broadcast_to(dw_w[:, None, :], (C_CONV_WIDTH, SUBLANES, C_CHANNELS)),
                dw_b.reshape(1, -1), ln_g.reshape(1, -1), ln_b.reshape(1, -1)]
    return operands, specs


def _conv_scratch(n_rows):
    return [pltpu.VMEM((n_rows + 2 * CONV_HALO, C_CHANNELS), F32),
            pltpu.VMEM((SUBLANES, CONV_SH_ROWS, C_CHANNELS), F32)]


def _conv_ctx_kernel(prev_ref, cur_ref, next_ref, w_ref, b_ref, g_ref, beta_ref, o_ref, ext_ref, sh_ref):
    _conv_fill(TC, True, True, prev_ref, cur_ref, next_ref, ext_ref)
    _conv_shift(0, ext_ref, sh_ref)
    for c in range(CONV_CHUNKS):
        _conv_chunk(0, c, sh_ref, w_ref, b_ref, g_ref, beta_ref, o_ref)


def _conv_ctx(cu, dw_w, dw_b, ln_g, ln_b):
    assert CTX_LEN == TC
    operands, specs = _conv_operands(cu, TC, N_LAT // TC, dw_w, dw_b, ln_g, ln_b)
    return pl.pallas_call(
        _conv_ctx_kernel,
        grid=(N_CTX // TC,),
        in_specs=specs,
        out_specs=pl.BlockSpec((TC, C_CHANNELS), lambda i: (i, 0)),
        out_shape=jax.ShapeDtypeStruct((N_CTX, C_CHANNELS), BF16),
        scratch_shapes=_conv_scratch(TC),
        compiler_params=pltpu.CompilerParams(dimension_semantics=("arbitrary",)),
        name="conv_context",
    )(*operands)


N_CONV_OPERANDS = 7
OUTPROJ_SLABS = 8
assert ((TM // TC) * CONV_CHUNKS) % OUTPROJ_SLABS == 0


def _outproj_kernel(widths, fused_conv, x_ref, mod_ref, w_ref, *refs):
    if fused_conv:
        pieces = list(refs[:len(widths) - 1])
        conv_refs = refs[len(widths) - 1:len(widths) - 1 + N_CONV_OPERANDS]
        o_ref, ext_ref, sh_ref, oc_ref = refs[len(widths) - 1 + N_CONV_OPERANDS:]
    else:
        pieces, o_ref = list(refs[:-1]), refs[-1]
    def projected(cols):
        y = None
        k0 = 0
        for piece, width in zip(pieces, widths):
            t = jnp.dot(piece[...], w_ref[k0:k0 + width, cols], preferred_element_type=F32)
            y = t if y is None else y + t
            k0 += width
        return y

    if not fused_conv:
        o_ref[...] = x_ref[...] + mod_ref[2:3, :] * projected(slice(None))
        return

    prev_ref, cur_ref, next_ref, cw_ref, cb_ref, cg_ref, cbeta_ref = conv_refs
    i = pl.program_id(0)
    _conv_fill(TM, i % TILES_PER_BATCH == 0, i % TILES_PER_BATCH == TILES_PER_BATCH - 1,
               prev_ref, cur_ref, next_ref, ext_ref)
    n_chunks = (TM // TC) * CONV_CHUNKS
    slab = D_MODEL // OUTPROJ_SLABS
    chunks_per_slab = n_chunks // OUTPROJ_SLABS
    for s in range(OUTPROJ_SLABS):
        cols = slice(s * slab, (s + 1) * slab)
        o_ref[:, cols] = projected(cols)
        for chunk in range(s * chunks_per_slab, (s + 1) * chunks_per_slab):
            sub, c = divmod(chunk, CONV_CHUNKS)
            if c == 0:
                _conv_shift(sub * TC, ext_ref, sh_ref)
            _conv_chunk(sub * TC, c, sh_ref, cw_ref, cb_ref, cg_ref, cbeta_ref, oc_ref)
    k_conv = MIX_WIDTH - widths[-1]
    y = o_ref[...] + jnp.dot(oc_ref[...], w_ref[k_conv:, :], preferred_element_type=F32)
    o_ref[...] = x_ref[...] + mod_ref[2:3, :] * y


def _outproj(xs, mod_l, w_out, layer, pieces, tile0, n_tiles, conv=None):
    widths = tuple(int(p.shape[1]) for p, _ in pieces) + ((C_CHANNELS,) if conv is not None else ())
    assert sum(widths) == MIX_WIDTH
    blocks = 4 * TM * D_MODEL * 4 + MIX_WIDTH * D_MODEL * 2 + 2 * TM * MIX_WIDTH * 2
    operands = [xs, mod_l, w_out] + [p for p, _ in pieces]
    in_specs = [
        pl.BlockSpec((TM, D_MODEL), lambda i: (tile0 + i, 0)),
        pl.BlockSpec((None, None, 3, D_MODEL), lambda i: (_group_of_tile(tile0 + i), 1, 0, 0)),
        pl.BlockSpec((None, MIX_WIDTH, D_MODEL), lambda i: (layer, 0, 0), pipeline_mode=pl.Buffered(1)),
    ] + [pl.BlockSpec((TM, w), functools.partial(lambda i, r0: (r0 + i, 0), r0=r0))
         for w, (_, r0) in zip(widths, pieces)]
    scratch = []
    if conv is not None:
        assert tile0 == 0 and n_tiles == LAT_TILES and TM % TC == 0
        conv_operands, conv_specs = _conv_operands(conv[0], TM, 0, *conv[1:])
        operands += conv_operands
        in_specs += conv_specs
        scratch = _conv_scratch(TM) + [pltpu.VMEM((TM, C_CHANNELS), BF16)]
        blocks += 4 * TM * C_CHANNELS * 4 + (TM + SUBLANES * CONV_SH_ROWS) * C_CHANNELS * 4
    return pl.pallas_call(
        functools.partial(_outproj_kernel, widths, conv is not None),
        grid=(n_tiles,),
        in_specs=in_specs,
        out_specs=pl.BlockSpec((TM, D_MODEL), lambda i: (tile0 + i, 0)),
        out_shape=jax.ShapeDtypeStruct((NT, D_MODEL), F32),
        scratch_shapes=scratch,
        input_output_aliases={0: 0},
        compiler_params=pltpu.CompilerParams(
            dimension_semantics=("arbitrary",),
            vmem_limit_bytes=_vmem_limit(blocks)),
        name="outproj",
    )(*operands)


def kernel(x, c, ctx, c_ctx, w_mod, b_mod, norm_ffn1, norm_mix, norm_ffn2, ffn1_w_gate, ffn1_w_up,
           ffn1_w_down, ffn2_w_gate, ffn2_w_up, ffn2_w_down, w_in, w_out, a_q_norm, a_k_norm, a_sink,
           b_q_norm, b_k_norm, b_rpb, c_dw_w, c_dw_b, c_ln_g, c_ln_b):
    mods = _mod_vectors(c, c_ctx, w_mod, b_mod)
    cos_t, sin_t = _rope_tables()
    b_bias = _b_bias_tables(b_rpb)
    ffn1_w = [w.astype(BF16) for w in (ffn1_w_gate, ffn1_w_up, ffn1_w_down)]
    ffn2_w = [w.astype(BF16) for w in (ffn2_w_gate, ffn2_w_up, ffn2_w_down)]
    w_in_bf = w_in.astype(BF16)
    w_out_bf = w_out.astype(BF16)
    q_scale = HEAD_DIM ** -0.5 * LOG2_E
    xs = None
    for l in range(DEPTH):
        last = l == DEPTH - 1
        mod_l = mods[l]
        if l == 0:
            xs = _ffn_streams(x.reshape(N_LAT, D_MODEL), ctx.reshape(N_CTX, D_MODEL), mod_l, 0, norm_ffn1[l],
                              *ffn1_w, l, in_place=False, out_rows=NT)
        else:
            xs = _ffn_streams(xs, xs, mod_l, 0, norm_ffn1[l], *ffn1_w, l, in_place=True, out_rows=NT)
        head_gains = jnp.concatenate([
            (a_q_norm[l] * q_scale)[None], a_k_norm[l][None], (b_q_norm[l] * q_scale)[None],
            b_k_norm[l][None], jnp.zeros((SUBLANES - 4, HEAD_DIM), F32)], axis=0)
        qkv, cu = _inproj(xs, mod_l, norm_mix[l], w_in_bf, l, head_gains, cos_t, sin_t)
        sink_l = a_sink[l] * LOG2_E
        o_ab = _attn_b(qkv, b_bias, l, _attn_a(qkv, sink_l))
        conv_params = (c_dw_w[l], c_dw_b[l], c_ln_g[l], c_ln_b[l])
        xs = _outproj(xs, mod_l, w_out_bf, l, [(o_ab, 0)], 0, LAT_TILES, conv=(cu,) + conv_params)
        if not last:
            sinks = jnp.concatenate([sink_l, jnp.full((B_HEADS,), NEG_INF, F32)])
            o_ctx = _attn_ctx(qkv, sinks)
            o_c_ctx = _conv_ctx(cu, *conv_params)
            xs = _outproj(xs, mod_l, w_out_bf, l, [(o_ctx, 0), (o_c_ctx, 0)], LAT_TILES, 1)
        if last:
            xs = _ffn_streams(xs, None, mod_l, 2, norm_ffn2[l], *ffn2_w, l, in_place=False, out_rows=N_LAT)
        else:
            xs = _ffn_streams(xs, xs, mod_l, 2, norm_ffn2[l], *ffn2_w, l, in_place=True, out_rows=NT)
    return xs.reshape(BATCH, SEQ, D_MODEL)
```

```python
import functools

import numpy as np
import jax
import jax.numpy as jnp
from jax import lax
from jax.experimental import pallas as pl
from jax.experimental.pallas import tpu as pltpu

D_MODEL = 2048
BATCH = 2
SEQ = 16384
DEPTH = 2
GRID_W = 64
GRID_ROWS = SEQ // GRID_W
CTX_LEN = 256
HEAD_DIM = 128
A_Q_HEADS = 6
A_KV_HEADS = 2
A_GROUP = A_Q_HEADS // A_KV_HEADS
A_WINDOW = 128
B_HEADS = 6
B_WIN_H = 8
B_WIN_W = 16
C_CHANNELS = 512
C_CONV_WIDTH = 31
C_PAD = (C_CONV_WIDTH - 1) // 2
D_FF = 5632
ROPE_THETA = 10000.0
NORM_EPS = 1e-6
NEG_INF = -1e30
N_MOD = 9
IN_COLS = 4608
MIX_WIDTH = 2048

HEAD_AQ = 0
HEAD_AK = HEAD_AQ + A_Q_HEADS
HEAD_AV = HEAD_AK + A_KV_HEADS
HEAD_BQ = HEAD_AV + A_KV_HEADS
HEAD_BK = HEAD_BQ + B_HEADS
HEAD_BV = HEAD_BK + B_HEADS
N_QKV_HEADS = HEAD_BV + B_HEADS
C_COL0 = N_QKV_HEADS * HEAD_DIM

N_LAT = BATCH * SEQ
N_CTX = BATCH * CTX_LEN
NT = N_LAT + N_CTX

V7X_VMEM_BYTES = 64 * 1024 * 1024
SUBLANES = 8
LANES = 128

TM = 512
LAT_TILES = N_LAT // TM
TILES_PER_BATCH = SEQ // TM
ALL_TILES = NT // TM
TF = 512
TC = 256
CONV_HALO = 16

F32 = jnp.float32
BF16 = jnp.bfloat16


def _vmem_limit(block_bytes):
    return int(min(V7X_VMEM_BYTES - 4 * 1024 * 1024, block_bytes + 12 * 1024 * 1024))


def _group_of_tile(i):
    return i // TILES_PER_BATCH


def _modulated(x, gain, shift, scale):
    ms = jnp.mean(x * x, axis=-1, keepdims=True)
    return (x * lax.rsqrt(ms + NORM_EPS) * gain) * (1.0 + scale) + shift


MOD_TN = 1024
MOD_GROUPS = BATCH + 1
MOD_UNROLL = 8


def _mod_kernel(c_ref, w_ref, b_ref, o_ref, a_ref):
    @pl.when(jnp.logical_and(pl.program_id(0) == 0, pl.program_id(1) == 0))
    def _():
        c = c_ref[...]
        a_ref[...] = c * jax.nn.sigmoid(c)

    lane_tiles = MOD_TN // LANES

    def body(step, accs):
        accs = list(accs)
        for u in range(MOD_UNROLL):
            r = pl.multiple_of((step * MOD_UNROLL + u) * SUBLANES, SUBLANES)
            for m in range(MOD_GROUPS):
                a = a_ref[m, pl.ds(r, SUBLANES), :]
                for t in range(lane_tiles):
                    idx = m * lane_tiles + t
                    accs[idx] = accs[idx] + w_ref[pl.ds(r, SUBLANES), t * LANES:(t + 1) * LANES] * a
        return tuple(accs)

    zero = jnp.zeros((SUBLANES, LANES), F32)
    accs = lax.fori_loop(0, D_MODEL // (SUBLANES * MOD_UNROLL), body, (zero,) * (MOD_GROUPS * lane_tiles))
    o_ref[...] = jnp.zeros_like(o_ref)
    for m in range(MOD_GROUPS):
        row = jnp.concatenate([jnp.sum(accs[m * lane_tiles + t], axis=0, keepdims=True)
                               for t in range(lane_tiles)], axis=1)
        o_ref[m:m + 1, :] = row + b_ref[...]


def _mod_vectors(c, c_ctx, w_mod, b_mod):
    rows = jnp.concatenate([c, c_ctx[None, :]], axis=0)
    c_cols = jnp.broadcast_to(rows[:, :, None], (MOD_GROUPS, D_MODEL, LANES))
    n = N_MOD * D_MODEL
    out = pl.pallas_call(
        _mod_kernel,
        grid=(DEPTH, n // MOD_TN),
        in_specs=[
            pl.BlockSpec((MOD_GROUPS, D_MODEL, LANES), lambda l, j: (0, 0, 0)),
            pl.BlockSpec((None, D_MODEL, MOD_TN), lambda l, j: (l, 0, j)),
            pl.BlockSpec((None, 1, MOD_TN), lambda l, j: (l, 0, j)),
        ],
        out_specs=pl.BlockSpec((None, SUBLANES, MOD_TN), lambda l, j: (l, 0, j)),
        out_shape=jax.ShapeDtypeStruct((DEPTH, SUBLANES, n), F32),
        scratch_shapes=[pltpu.VMEM((MOD_GROUPS, D_MODEL, LANES), F32)],
        compiler_params=pltpu.CompilerParams(
            dimension_semantics=("arbitrary", "arbitrary"),
            vmem_limit_bytes=_vmem_limit(2 * D_MODEL * MOD_TN * 4 + 3 * MOD_GROUPS * D_MODEL * LANES * 4)),
        name="mod_vectors",
    )(c_cols, w_mod, b_mod.reshape(DEPTH, 1, n))
    return out[:, :MOD_GROUPS].reshape(DEPTH, MOD_GROUPS, 3, 3, D_MODEL)


FFN_SUB = 512


FFN_CHUNKS = D_FF // TF
FFN_SLOTS = 3
N_FFN_WEIGHTS = 3


def _ffn_kernel(tm, layer, x_ref, mod_ref, gain_ref, wg_hbm, wu_hbm, wd_hbm, *rest):
    o_ref, h_ref, wg_buf, wu_buf, wd_buf, sem = rest[-6:]
    i = pl.program_id(0)
    n_tiles = pl.num_programs(0)

    def copies(c):
        slot = c % FFN_SLOTS
        cols = pl.ds(c * TF, TF)
        return (pltpu.make_async_copy(wg_hbm.at[layer, :, cols], wg_buf.at[slot], sem.at[0, slot]),
                pltpu.make_async_copy(wu_hbm.at[layer, :, cols], wu_buf.at[slot], sem.at[1, slot]),
                pltpu.make_async_copy(wd_hbm.at[layer, cols, :], wd_buf.at[slot], sem.at[2, slot]))

    def start(c):
        for copy in copies(c):
            copy.start()

    def wait(c):
        for copy in copies(c):
            copy.wait()

    pl.when(i == 0)(functools.partial(start, 0))
    wait(0)
    start(1)
    n_sub = tm // FFN_SUB
    for c in range(FFN_CHUNKS):
        slot = c % FFN_SLOTS
        for sub in range(n_sub):
            rows = slice(sub * FFN_SUB, (sub + 1) * FFN_SUB)
            if c == 0:
                h = _modulated(x_ref[rows, :], gain_ref[...], mod_ref[0:1, :], mod_ref[1:2, :])
                h_ref[rows, :] = h.astype(BF16)
            h = h_ref[rows, :]
            g = jnp.dot(h, wg_buf[slot], preferred_element_type=F32)
            u = jnp.dot(h, wu_buf[slot], preferred_element_type=F32)
            a = (g * jax.nn.sigmoid(g)) * u
            d = jnp.dot(a.astype(BF16), wd_buf[slot], preferred_element_type=F32)
            if c == 0:
                o_ref[rows, :] = d
            elif c == FFN_CHUNKS - 1:
                o_ref[rows, :] = x_ref[rows, :] + (0.5 * mod_ref[2:3, :]) * (o_ref[rows, :] + d)
            else:
                o_ref[rows, :] += d
            if sub == 0:
                if c + 1 < FFN_CHUNKS:
                    wait(c + 1)
                if c + 2 < FFN_CHUNKS:
                    start(c + 2)
                if c == FFN_CHUNKS - 1:
                    pl.when(i + 1 < n_tiles)(functools.partial(start, 0))


def _ffn(x, dest, mod_l, sub, gain, wg, wu, wd, layer, tm, x_tile0, out_tile0, n_tiles, out_rows):
    assert tm % FFN_SUB == 0 and SEQ % tm == 0
    assert FFN_SLOTS >= 3 and FFN_CHUNKS >= 2 and (FFN_CHUNKS - 1) % FFN_SLOTS != 0
    blocks = 4 * tm * D_MODEL * 4 + FFN_SLOTS * N_FFN_WEIGHTS * D_MODEL * TF * 2 + tm * D_MODEL * 2
    operands = [x, mod_l, gain.reshape(1, D_MODEL), wg, wu, wd]
    in_specs = [
        pl.BlockSpec((tm, D_MODEL), lambda i: (x_tile0 + i, 0)),
        pl.BlockSpec((None, None, 3, D_MODEL), lambda i: ((out_tile0 + i) * tm // SEQ, sub, 0, 0)),
        pl.BlockSpec((1, D_MODEL), lambda i: (0, 0)),
        pl.BlockSpec(memory_space=pl.ANY),
        pl.BlockSpec(memory_space=pl.ANY),
        pl.BlockSpec(memory_space=pl.ANY),
    ]
    aliases = {}
    if dest is x:
        aliases = {0: 0}
    elif dest is not None:
        operands.append(dest)
        in_specs.append(pl.BlockSpec(memory_space=pl.ANY))
        aliases = {len(operands) - 1: 0}
    return pl.pallas_call(
        functools.partial(_ffn_kernel, tm, layer),
        grid=(n_tiles,),
        in_specs=in_specs,
        out_specs=pl.BlockSpec((tm, D_MODEL), lambda i: (out_tile0 + i, 0)),
        out_shape=jax.ShapeDtypeStruct((out_rows, D_MODEL), F32),
        scratch_shapes=[pltpu.VMEM((tm, D_MODEL), BF16),
                        pltpu.VMEM((FFN_SLOTS, D_MODEL, TF), BF16),
                        pltpu.VMEM((FFN_SLOTS, D_MODEL, TF), BF16),
                        pltpu.VMEM((FFN_SLOTS, TF, D_MODEL), BF16),
                        pltpu.SemaphoreType.DMA((N_FFN_WEIGHTS, FFN_SLOTS))],
        input_output_aliases=aliases,
        compiler_params=pltpu.CompilerParams(
            dimension_semantics=("arbitrary",),
            vmem_limit_bytes=_vmem_limit(blocks)),
        name="ffn",
    )(*operands)


FFN_TM_LAT = 1024
FFN_LAT_TILES = N_LAT // FFN_TM_LAT


def _ffn_streams(x_lat, x_ctx, *args, in_place, out_rows):
    combined = x_ctx is x_lat
    out = _ffn(x_lat, x_lat if in_place else None, *args, FFN_TM_LAT, 0, 0, FFN_LAT_TILES, out_rows)
    if x_ctx is not None:
        src = out if combined else x_ctx
        out = _ffn(src, out, *args, TM, LAT_TILES if combined else 0, LAT_TILES, 1, out_rows)
    return out


GAIN_AQ, GAIN_AK, GAIN_BQ, GAIN_BK = 0, 1, 2, 3


def _head_kind(h):
    if h < HEAD_AK:
        return GAIN_AQ, True
    if h < HEAD_AV:
        return GAIN_AK, True
    if h < HEAD_BQ:
        return None, False
    if h < HEAD_BK:
        return GAIN_BQ, False
    if h < HEAD_BV:
        return GAIN_BK, False
    return None, False


def _inproj_kernel(x_ref, mod_ref, gain_ref, w_ref, hg_ref, cos_ref, sin_ref, qkv_ref, cu_ref, h_ref):
    h = _modulated(x_ref[...], gain_ref[...], mod_ref[0:1, :], mod_ref[1:2, :])
    h_ref[...] = h.astype(BF16)
    lane = lax.broadcasted_iota(jnp.int32, (TM, HEAD_DIM), 1)
    even_quarter = ((lane // (HEAD_DIM // 4)) % 2) == 0
    for pair in range(N_QKV_HEADS // 2):
        c0 = pair * 2 * HEAD_DIM
        y2 = jnp.dot(h_ref[...], w_ref[:, c0:c0 + 2 * HEAD_DIM], preferred_element_type=F32)
        for half in range(2):
            hd = 2 * pair + half
            y = y2[:, half * HEAD_DIM:(half + 1) * HEAD_DIM]
            gain_row, rotary = _head_kind(hd)
            if gain_row is not None:
                ms = jnp.mean(y * y, axis=-1, keepdims=True)
                y = y * lax.rsqrt(ms + NORM_EPS) * hg_ref[gain_row:gain_row + 1, :]
            if rotary:
                swapped = jnp.where(even_quarter,
                                    pltpu.roll(y, HEAD_DIM - HEAD_DIM // 4, 1),
                                    pltpu.roll(y, HEAD_DIM // 4, 1))
                y = y * cos_ref[...] + swapped * sin_ref[...]
            qkv_ref[hd] = y.astype(BF16)
    for blk in range((IN_COLS - C_COL0) // (2 * HEAD_DIM)):
        c0 = C_COL0 + blk * 2 * HEAD_DIM
        cu_ref[:, blk * 2 * HEAD_DIM:(blk + 1) * 2 * HEAD_DIM] = jnp.dot(
            h_ref[...], w_ref[:, c0:c0 + 2 * HEAD_DIM], preferred_element_type=F32)


def _inproj(xs, mod_l, gain, w_in, layer, head_gains, cos_t, sin_t):
    n_cu = IN_COLS - C_COL0
    blocks = (2 * TM * D_MODEL * 4 + D_MODEL * IN_COLS * 2 + 2 * N_QKV_HEADS * TM * HEAD_DIM * 2
              + 2 * TM * n_cu * 4 + 4 * TM * HEAD_DIM * 4 + TM * D_MODEL * 2)
    rope_block = lambda i: (jnp.where(i < LAT_TILES, i % TILES_PER_BATCH, TILES_PER_BATCH), 0)
    return pl.pallas_call(
        _inproj_kernel,
        grid=(ALL_TILES,),
        in_specs=[
            pl.BlockSpec((TM, D_MODEL), lambda i: (i, 0)),
            pl.BlockSpec((None, None, 3, D_MODEL), lambda i: (_group_of_tile(i), 1, 0, 0)),
            pl.BlockSpec((1, D_MODEL), lambda i: (0, 0)),
            pl.BlockSpec((None, D_MODEL, IN_COLS), lambda i: (layer, 0, 0), pipeline_mode=pl.Buffered(1)),
            pl.BlockSpec((SUBLANES, HEAD_DIM), lambda i: (0, 0)),
            pl.BlockSpec((TM, HEAD_DIM), rope_block),
            pl.BlockSpec((TM, HEAD_DIM), rope_block),
        ],
        out_specs=[
            pl.BlockSpec((N_QKV_HEADS, TM, HEAD_DIM), lambda i: (0, i, 0)),
            pl.BlockSpec((TM, n_cu), lambda i: (i, 0)),
        ],
        out_shape=[
            jax.ShapeDtypeStruct((N_QKV_HEADS, NT, HEAD_DIM), BF16),
            jax.ShapeDtypeStruct((NT, n_cu), F32),
        ],
        scratch_shapes=[pltpu.VMEM((TM, D_MODEL), BF16)],
        compiler_params=pltpu.CompilerParams(
            dimension_semantics=("arbitrary",),
            vmem_limit_bytes=_vmem_limit(blocks)),
        name="inproj",
    )(xs, mod_l, gain.reshape(1, D_MODEL), w_in, head_gains, cos_t, sin_t)


def _rope_tables():
    t = np.arange(SEQ)
    n_freq = HEAD_DIM // 4
    inv_freq = ROPE_THETA ** (-np.arange(n_freq, dtype=np.float64) / n_freq)
    ang_r = (t // GRID_W)[:, None] * inv_freq[None, :]
    ang_c = (t % GRID_W)[:, None] * inv_freq[None, :]
    cos_t = np.concatenate([np.cos(ang_r), np.cos(ang_r), np.cos(ang_c), np.cos(ang_c)], axis=-1)
    sin_t = np.concatenate([-np.sin(ang_r), np.sin(ang_r), -np.sin(ang_c), np.sin(ang_c)], axis=-1)
    cos_t = np.concatenate([cos_t, np.ones((TM, HEAD_DIM))], axis=0)
    sin_t = np.concatenate([sin_t, np.zeros((TM, HEAD_DIM))], axis=0)
    return jnp.asarray(cos_t, F32), jnp.asarray(sin_t, F32)


LOG2_E = float(np.log2(np.e))
ATTN_TQ = 256
ATTN_BLOCKS = SEQ // ATTN_TQ
ATTN_UNROLL = 8
PLACE_INTERIOR, PLACE_FIRST, PLACE_LAST = 0, 1, 2
N_PLACEMENTS = 3
PLACEMENT_BLOCKS = (1, 0, ATTN_BLOCKS - 1)

A_LEAD = A_WINDOW
A_KEYS = ATTN_TQ + 2 * A_WINDOW
B_ROWS_PER_BLOCK = ATTN_TQ // GRID_W
B_UNION_ROWS = B_ROWS_PER_BLOCK + B_WIN_H
B_LEAD = (B_WIN_H // 2) * GRID_W
B_KEYS = B_UNION_ROWS * GRID_W
assert ATTN_BLOCKS % ATTN_UNROLL == 0 and ATTN_BLOCKS >= 3
assert A_LEAD % LANES == 0 and B_LEAD % LANES == 0 and A_KEYS % LANES == 0 and B_KEYS % LANES == 0


def _dot_nt(a, b):
    return lax.dot_general(a, b, (((1,), (1,)), ((), ())), preferred_element_type=F32)


def _window_start(first_query, lead, n_keys):
    return int(np.clip(first_query - lead, 0, SEQ - n_keys))


def _local_attn_kernel(lead, n_keys, has_sink, *refs):
    if has_sink:
        sink_ref, refs = refs[0], refs[1:]
    q_ref, k_ref, v_ref, kc_ref, vc_ref, bias_ref = refs[:6]
    o_ref = refs[-1]
    sink = sink_ref[pl.program_id(1) * A_GROUP + pl.program_id(2)] if has_sink else None
    kc = kc_ref[...]
    vc = vc_ref[...]

    def scores(i):
        q0 = pl.multiple_of(i * ATTN_TQ, ATTN_TQ)
        k0 = pl.multiple_of(jnp.clip(q0 - lead, 0, SEQ - n_keys), LANES)
        placement = jnp.where(i == 0, PLACE_FIRST, jnp.where(i == ATTN_BLOCKS - 1, PLACE_LAST, PLACE_INTERIOR))
        q = q_ref[pl.ds(q0, ATTN_TQ), :]
        s = _dot_nt(q, k_ref[pl.ds(k0, n_keys), :]) + bias_ref[placement]
        return q0, k0, s, _dot_nt(q, kc)

    def softmax(q0, k0, s, sc):
        m = jnp.maximum(jnp.max(s, axis=-1, keepdims=True), jnp.max(sc, axis=-1, keepdims=True))
        if has_sink:
            m = jnp.maximum(m, sink)
        p = jnp.exp2(s - m)
        pc = jnp.exp2(sc - m)
        den = jnp.sum(p, axis=-1, keepdims=True) + jnp.sum(pc, axis=-1, keepdims=True)
        if has_sink:
            den = den + jnp.exp2(sink - m)
        return q0, k0, p.astype(BF16), pc.astype(BF16), den

    def output(q0, k0, p, pc, den):
        o = (jnp.dot(p, v_ref[pl.ds(k0, n_keys), :], preferred_element_type=F32)
             + jnp.dot(pc, vc, preferred_element_type=F32))
        o_ref[pl.ds(q0, ATTN_TQ), :] = (o / den).astype(BF16)

    def body(it, carry):
        scored, weighted = {}, {}
        for step in range(ATTN_UNROLL + 2):
            if step < ATTN_UNROLL:
                scored[step] = scores(it * ATTN_UNROLL + step)
            if 0 <= step - 2 < ATTN_UNROLL:
                output(*weighted.pop(step - 2))
            if 0 <= step - 1 < ATTN_UNROLL:
                weighted[step - 1] = softmax(*scored.pop(step - 1))
        return carry

    lax.fori_loop(0, ATTN_BLOCKS // ATTN_UNROLL, body, 0)


def _local_attn_vmem(n_keys):
    return 2 * 4 * SEQ * HEAD_DIM * 2 + 4 * CTX_LEN * HEAD_DIM * 2 + 2 * N_PLACEMENTS * ATTN_TQ * n_keys * 4


def _a_band_table():
    tables = []
    for blk in PLACEMENT_BLOCKS:
        q0 = blk * ATTN_TQ
        kpos = _window_start(q0, A_LEAD, A_KEYS) + np.arange(A_KEYS)[None, :]
        qpos = q0 + np.arange(ATTN_TQ)[:, None]
        tables.append(np.where(np.abs(kpos - qpos) <= A_WINDOW, 0.0, NEG_INF))
    return jnp.asarray(np.stack(tables), F32)


def _attn_a(qkv, sink):
    seq_block = (None, SEQ, HEAD_DIM)
    ctx_block = (None, CTX_LEN, HEAD_DIM)
    ctx_row0 = N_LAT // CTX_LEN
    return pl.pallas_call(
        functools.partial(_local_attn_kernel, A_LEAD, A_KEYS, True),
        grid=(BATCH, A_KV_HEADS, A_GROUP),
        in_specs=[
            pl.BlockSpec(memory_space=pltpu.SMEM),
            pl.BlockSpec(seq_block, lambda b, kv, g: (HEAD_AQ + kv * A_GROUP + g, b, 0)),
            pl.BlockSpec(seq_block, lambda b, kv, g: (HEAD_AK + kv, b, 0)),
            pl.BlockSpec(seq_block, lambda b, kv, g: (HEAD_AV + kv, b, 0)),
            pl.BlockSpec(ctx_block, lambda b, kv, g: (HEAD_AK + kv, ctx_row0 + b, 0)),
            pl.BlockSpec(ctx_block, lambda b, kv, g: (HEAD_AV + kv, ctx_row0 + b, 0)),
            pl.BlockSpec((N_PLACEMENTS, ATTN_TQ, A_KEYS), lambda b, kv, g: (0, 0, 0)),
        ],
        out_specs=pl.BlockSpec((SEQ, HEAD_DIM), lambda b, kv, g: (b, kv * A_GROUP + g)),
        out_shape=jax.ShapeDtypeStruct((N_LAT, (A_Q_HEADS + B_HEADS) * HEAD_DIM), BF16),
        compiler_params=pltpu.CompilerParams(
            dimension_semantics=("arbitrary", "arbitrary", "arbitrary"),
            vmem_limit_bytes=_vmem_limit(_local_attn_vmem(A_KEYS))),
        name="attn_window",
    )(sink, qkv, qkv, qkv, qkv, qkv, _a_band_table())


def _b_bias_tables(rpb):
    j = np.arange(GRID_W)[:, None]
    jk = np.arange(GRID_W)[None, :]
    col_start = np.clip(j - B_WIN_W // 2, 0, GRID_W - B_WIN_W)
    inside = (jk >= col_start) & (jk < col_start + B_WIN_W)
    n_col = 2 * B_WIN_W - 1
    onehot = (inside[:, :, None] & ((jk - j + (B_WIN_W - 1))[:, :, None] == np.arange(n_col))).astype(np.float32)
    by_row = jnp.einsum('lhac,jkc->lhajk', rpb, jnp.asarray(onehot), precision=lax.Precision.HIGHEST)
    by_row = by_row * LOG2_E + jnp.asarray(np.where(inside, 0.0, NEG_INF), F32)
    masked = jnp.full((DEPTH, B_HEADS, GRID_W, GRID_W), NEG_INF, F32)
    query_rows = []
    for blk in PLACEMENT_BLOCKS:
        r0 = blk * B_ROWS_PER_BLOCK
        u0 = _window_start(r0 * GRID_W, B_LEAD, B_KEYS) // GRID_W
        for t in range(B_ROWS_PER_BLOCK):
            r = r0 + t
            rs = int(np.clip(r - B_WIN_H // 2, 0, GRID_ROWS - B_WIN_H))
            before = rs - u0
            assert 0 <= before <= B_UNION_ROWS - B_WIN_H
            bias_row0 = rs - r + (B_WIN_H - 1)
            pieces = [by_row[:, :, bias_row0 + u - before] if before <= u < before + B_WIN_H else masked
                      for u in range(B_UNION_ROWS)]
            query_rows.append(jnp.concatenate(pieces, axis=-1))
    return jnp.stack(query_rows, axis=2).reshape(DEPTH, B_HEADS, N_PLACEMENTS, ATTN_TQ, B_KEYS)


def _attn_b(qkv, bias, layer, o_ab):
    seq_block = (None, SEQ, HEAD_DIM)
    ctx_block = (None, CTX_LEN, HEAD_DIM)
    ctx_row0 = N_LAT // CTX_LEN
    return pl.pallas_call(
        functools.partial(_local_attn_kernel, B_LEAD, B_KEYS, False),
        grid=(BATCH, B_HEADS),
        in_specs=[
            pl.BlockSpec(seq_block, lambda b, h: (HEAD_BQ + h, b, 0)),
            pl.BlockSpec(seq_block, lambda b, h: (HEAD_BK + h, b, 0)),
            pl.BlockSpec(seq_block, lambda b, h: (HEAD_BV + h, b, 0)),
            pl.BlockSpec(ctx_block, lambda b, h: (HEAD_BK + h, ctx_row0 + b, 0)),
            pl.BlockSpec(ctx_block, lambda b, h: (HEAD_BV + h, ctx_row0 + b, 0)),
            pl.BlockSpec((None, None, N_PLACEMENTS, ATTN_TQ, B_KEYS), lambda b, h: (layer, h, 0, 0, 0)),
            pl.BlockSpec(memory_space=pl.ANY),
        ],
        out_specs=pl.BlockSpec((SEQ, HEAD_DIM), lambda b, h: (b, A_Q_HEADS + h)),
        out_shape=jax.ShapeDtypeStruct(o_ab.shape, BF16),
        input_output_aliases={6: 0},
        compiler_params=pltpu.CompilerParams(
            dimension_semantics=("arbitrary", "arbitrary"),
            vmem_limit_bytes=_vmem_limit(_local_attn_vmem(B_KEYS))),
        name="attn_neighbourhood",
    )(qkv, qkv, qkv, qkv, qkv, bias, o_ab)


N_CTX_HEADS = A_Q_HEADS + B_HEADS


def _attn_ctx_kernel(sink_ref, q_ref, k_ref, v_ref, o_ref):
    sink = sink_ref[pl.program_id(1)]
    q = q_ref[...]
    s = _dot_nt(q, k_ref[...])
    m = jnp.maximum(jnp.max(s, axis=-1, keepdims=True), sink)
    p = jnp.exp2(s - m)
    den = jnp.sum(p, axis=-1, keepdims=True) + jnp.exp2(sink - m)
    o = jnp.dot(p.astype(BF16), v_ref[...], preferred_element_type=F32)
    o_ref[...] = (o / den).astype(BF16)


def _attn_ctx(qkv, sinks):
    ctx_block = (None, CTX_LEN, HEAD_DIM)
    ctx_row0 = N_LAT // CTX_LEN
    is_b = lambda h: h >= A_Q_HEADS
    q_head = lambda h: jnp.where(is_b(h), HEAD_BQ + h - A_Q_HEADS, HEAD_AQ + h)
    k_head = lambda h: jnp.where(is_b(h), HEAD_BK + h - A_Q_HEADS, HEAD_AK + h // A_GROUP)
    v_head = lambda h: jnp.where(is_b(h), HEAD_BV + h - A_Q_HEADS, HEAD_AV + h // A_GROUP)
    return pl.pallas_call(
        _attn_ctx_kernel,
        grid=(BATCH, N_CTX_HEADS),
        in_specs=[
            pl.BlockSpec(memory_space=pltpu.SMEM),
            pl.BlockSpec(ctx_block, lambda b, h: (q_head(h), ctx_row0 + b, 0)),
            pl.BlockSpec(ctx_block, lambda b, h: (k_head(h), ctx_row0 + b, 0)),
            pl.BlockSpec(ctx_block, lambda b, h: (v_head(h), ctx_row0 + b, 0)),
        ],
        out_specs=pl.BlockSpec((CTX_LEN, HEAD_DIM), lambda b, h: (b, h)),
        out_shape=jax.ShapeDtypeStruct((N_CTX, N_CTX_HEADS * HEAD_DIM), BF16),
        compiler_params=pltpu.CompilerParams(dimension_semantics=("arbitrary", "arbitrary")),
        name="attn_context",
    )(sinks, qkv, qkv, qkv)


CONV_ROWS = 32
CONV_SH_ROWS = TC + SUBLANES * ((C_CONV_WIDTH - 1) // SUBLANES)
assert SUBLANES - 1 + CONV_HALO - C_PAD + CONV_SH_ROWS <= TC + 2 * CONV_HALO


def _glu(u):
    return u[:, :C_CHANNELS] * jax.nn.sigmoid(u[:, C_CHANNELS:])


def _conv_fill(n_rows, first, last, prev_ref, cur_ref, next_ref, ext_ref):
    ext_ref[0:CONV_HALO, :] = jnp.where(first, 0.0, _glu(prev_ref[...]))
    ext_ref[CONV_HALO:CONV_HALO + n_rows, :] = _glu(cur_ref[...])
    ext_ref[CONV_HALO + n_rows:, :] = jnp.where(last, 0.0, _glu(next_ref[...]))


def _conv_shift(row0, ext_ref, sh_ref):
    for b in range(SUBLANES):
        lo = row0 + b + CONV_HALO - C_PAD
        sh_ref[b] = ext_ref[lo:lo + CONV_SH_ROWS, :]


def _conv_chunk(row0, c, sh_ref, w_ref, b_ref, g_ref, beta_ref, out_ref):
    groups = CONV_ROWS // SUBLANES
    r0 = c * CONV_ROWS
    accs = [jnp.zeros((SUBLANES, C_CHANNELS), F32) for _ in range(groups)]
    for k in range(C_CONV_WIDTH):
        a, b = divmod(k, SUBLANES)
        w_k = w_ref[k]
        for g in range(groups):
            lo = r0 + SUBLANES * (a + g)
            accs[g] = accs[g] + sh_ref[b, lo:lo + SUBLANES, :] * w_k
    acc = jnp.concatenate(accs, axis=0) + b_ref[...]
    mu = jnp.mean(acc, axis=-1, keepdims=True)
    xc = acc - mu
    y = xc * lax.rsqrt(jnp.mean(xc * xc, axis=-1, keepdims=True) + NORM_EPS)
    y = y * g_ref[...] + beta_ref[...]
    out_ref[row0 + r0:row0 + r0 + CONV_ROWS, :] = (y * jax.nn.sigmoid(y)).astype(BF16)


CONV_CHUNKS = TC // CONV_ROWS


def _conv_operands(cu, n_rows, tile0, dw_w, dw_b, ln_g, ln_b):
    halo_per_tile = n_rows // CONV_HALO
    n_halo = cu.shape[0] // CONV_HALO
    vec = pl.BlockSpec((1, C_CHANNELS), lambda i: (0, 0))
    specs = [
        pl.BlockSpec((CONV_HALO, 2 * C_CHANNELS), lambda i: (jnp.maximum((tile0 + i) * halo_per_tile - 1, 0), 0)),
        pl.BlockSpec((n_rows, 2 * C_CHANNELS), lambda i: (tile0 + i, 0)),
        pl.BlockSpec((CONV_HALO, 2 * C_CHANNELS),
                     lambda i: (jnp.minimum((tile0 + i + 1) * halo_per_tile, n_halo - 1), 0)),
        pl.BlockSpec((C_CONV_WIDTH, SUBLANES, C_CHANNELS), lambda i: (0, 0, 0)),
        vec, vec, vec,
    ]
    operands = [cu, cu, cu, jnp.broadcast_to(dw_w[:, None, :], (C_CONV_WIDTH, SUBLANES, C_CHANNELS)),
                dw_b.reshape(1, -1), ln_g.reshape(1, -1), ln_b.reshape(1, -1)]
    return operands, specs


def _conv_scratch(n_rows):
    return [pltpu.VMEM((n_rows + 2 * CONV_HALO, C_CHANNELS), F32),
            pltpu.VMEM((SUBLANES, CONV_SH_ROWS, C_CHANNELS), F32)]


def _conv_ctx_kernel(prev_ref, cur_ref, next_ref, w_ref, b_ref, g_ref, beta_ref, o_ref, ext_ref, sh_ref):
    _conv_fill(TC, True, True, prev_ref, cur_ref, next_ref, ext_ref)
    _conv_shift(0, ext_ref, sh_ref)
    for c in range(CONV_CHUNKS):
        _conv_chunk(0, c, sh_ref, w_ref, b_ref, g_ref, beta_ref, o_ref)


def _conv_ctx(cu, dw_w, dw_b, ln_g, ln_b):
    assert CTX_LEN == TC
    operands, specs = _conv_operands(cu, TC, N_LAT // TC, dw_w, dw_b, ln_g, ln_b)
    return pl.pallas_call(
        _conv_ctx_kernel,
        grid=(N_CTX // TC,),
        in_specs=specs,
        out_specs=pl.BlockSpec((TC, C_CHANNELS), lambda i: (i, 0)),
        out_shape=jax.ShapeDtypeStruct((N_CTX, C_CHANNELS), BF16),
        scratch_shapes=_conv_scratch(TC),
        compiler_params=pltpu.CompilerParams(dimension_semantics=("arbitrary",)),
        name="conv_context",
    )(*operands)


N_CONV_OPERANDS = 7
OUTPROJ_SLABS = 8
assert ((TM // TC) * CONV_CHUNKS) % OUTPROJ_SLABS == 0


def _outproj_kernel(widths, fused_conv, x_ref, mod_ref, w_ref, *refs):
    if fused_conv:
        pieces = list(refs[:len(widths) - 1])
        conv_refs = refs[len(widths) - 1:len(widths) - 1 + N_CONV_OPERANDS]
        o_ref, ext_ref, sh_ref, oc_ref = refs[len(widths) - 1 + N_CONV_OPERANDS:]
    else:
        pieces, o_ref = list(refs[:-1]), refs[-1]
    def projected(cols):
        y = None
        k0 = 0
        for piece, width in zip(pieces, widths):
            t = jnp.dot(piece[...], w_ref[k0:k0 + width, cols], preferred_element_type=F32)
            y = t if y is None else y + t
            k0 += width
        return y

    if not fused_conv:
        o_ref[...] = x_ref[...] + mod_ref[2:3, :] * projected(slice(None))
        return

    prev_ref, cur_ref, next_ref, cw_ref, cb_ref, cg_ref, cbeta_ref = conv_refs
    i = pl.program_id(0)
    _conv_fill(TM, i % TILES_PER_BATCH == 0, i % TILES_PER_BATCH == TILES_PER_BATCH - 1,
               prev_ref, cur_ref, next_ref, ext_ref)
    n_chunks = (TM // TC) * CONV_CHUNKS
    slab = D_MODEL // OUTPROJ_SLABS
    chunks_per_slab = n_chunks // OUTPROJ_SLABS
    for s in range(OUTPROJ_SLABS):
        cols = slice(s * slab, (s + 1) * slab)
        o_ref[:, cols] = projected(cols)
        for chunk in range(s * chunks_per_slab, (s + 1) * chunks_per_slab):
            sub, c = divmod(chunk, CONV_CHUNKS)
            if c == 0:
                _conv_shift(sub * TC, ext_ref, sh_ref)
            _conv_chunk(sub * TC, c, sh_ref, cw_ref, cb_ref, cg_ref, cbeta_ref, oc_ref)
    k_conv = MIX_WIDTH - widths[-1]
    y = o_ref[...] + jnp.dot(oc_ref[...], w_ref[k_conv:, :], preferred_element_type=F32)
    o_ref[...] = x_ref[...] + mod_ref[2:3, :] * y


def _outproj(xs, mod_l, w_out, layer, pieces, tile0, n_tiles, conv=None):
    widths = tuple(int(p.shape[1]) for p, _ in pieces) + ((C_CHANNELS,) if conv is not None else ())
    assert sum(widths) == MIX_WIDTH
    blocks = 4 * TM * D_MODEL * 4 + MIX_WIDTH * D_MODEL * 2 + 2 * TM * MIX_WIDTH * 2
    operands = [xs, mod_l, w_out] + [p for p, _ in pieces]
    in_specs = [
        pl.BlockSpec((TM, D_MODEL), lambda i: (tile0 + i, 0)),
        pl.BlockSpec((None, None, 3, D_MODEL), lambda i: (_group_of_tile(tile0 + i), 1, 0, 0)),
        pl.BlockSpec((None, MIX_WIDTH, D_MODEL), lambda i: (layer, 0, 0), pipeline_mode=pl.Buffered(1)),
    ] + [pl.BlockSpec((TM, w), functools.partial(lambda i, r0: (r0 + i, 0), r0=r0))
         for w, (_, r0) in zip(widths, pieces)]
    scratch = []
    if conv is not None:
        assert tile0 == 0 and n_tiles == LAT_TILES and TM % TC == 0
        conv_operands, conv_specs = _conv_operands(conv[0], TM, 0, *conv[1:])
        operands += conv_operands
        in_specs += conv_specs
        scratch = _conv_scratch(TM) + [pltpu.VMEM((TM, C_CHANNELS), BF16)]
        blocks += 4 * TM * C_CHANNELS * 4 + (TM + SUBLANES * CONV_SH_ROWS) * C_CHANNELS * 4
    return pl.pallas_call(
        functools.partial(_outproj_kernel, widths, conv is not None),
        grid=(n_tiles,),
        in_specs=in_specs,
        out_specs=pl.BlockSpec((TM, D_MODEL), lambda i: (tile0 + i, 0)),
        out_shape=jax.ShapeDtypeStruct((NT, D_MODEL), F32),
        scratch_shapes=scratch,
        input_output_aliases={0: 0},
        compiler_params=pltpu.CompilerParams(
            dimension_semantics=("arbitrary",),
            vmem_limit_bytes=_vmem_limit(blocks)),
        name="outproj",
    )(*operands)


def kernel(x, c, ctx, c_ctx, w_mod, b_mod, norm_ffn1, norm_mix, norm_ffn2, ffn1_w_gate, ffn1_w_up,
           ffn1_w_down, ffn2_w_gate, ffn2_w_up, ffn2_w_down, w_in, w_out, a_q_norm, a_k_norm, a_sink,
           b_q_norm, b_k_norm, b_rpb, c_dw_w, c_dw_b, c_ln_g, c_ln_b):
    mods = _mod_vectors(c, c_ctx, w_mod, b_mod)
    cos_t, sin_t = _rope_tables()
    b_bias = _b_bias_tables(b_rpb)
    ffn1_w = [w.astype(BF16) for w in (ffn1_w_gate, ffn1_w_up, ffn1_w_down)]
    ffn2_w = [w.astype(BF16) for w in (ffn2_w_gate, ffn2_w_up, ffn2_w_down)]
    w_in_bf = w_in.astype(BF16)
    w_out_bf = w_out.astype(BF16)
    q_scale = HEAD_DIM ** -0.5 * LOG2_E
    xs = None
    for l in range(DEPTH):
        last = l == DEPTH - 1
        mod_l = mods[l]
        if l == 0:
            xs = _ffn_streams(x.reshape(N_LAT, D_MODEL), ctx.reshape(N_CTX, D_MODEL), mod_l, 0, norm_ffn1[l],
                              *ffn1_w, l, in_place=False, out_rows=NT)
        else:
            xs = _ffn_streams(xs, xs, mod_l, 0, norm_ffn1[l], *ffn1_w, l, in_place=True, out_rows=NT)
        head_gains = jnp.concatenate([
            (a_q_norm[l] * q_scale)[None], a_k_norm[l][None], (b_q_norm[l] * q_scale)[None],
            b_k_norm[l][None], jnp.zeros((SUBLANES - 4, HEAD_DIM), F32)], axis=0)
        qkv, cu = _inproj(xs, mod_l, norm_mix[l], w_in_bf, l, head_gains, cos_t, sin_t)
        sink_l = a_sink[l] * LOG2_E
        o_ab = _attn_b(qkv, b_bias, l, _attn_a(qkv, sink_l))
        conv_params = (c_dw_w[l], c_dw_b[l], c_ln_g[l], c_ln_b[l])
        xs = _outproj(xs, mod_l, w_out_bf, l, [(o_ab, 0)], 0, LAT_TILES, conv=(cu,) + conv_params)
        if not last:
            sinks = jnp.concatenate([sink_l, jnp.full((B_HEADS,), NEG_INF, F32)])
            o_ctx = _attn_ctx(qkv, sinks)
            o_c_ctx = _conv_ctx(cu, *conv_params)
            xs = _outproj(xs, mod_l, w_out_bf, l, [(o_ctx, 0), (o_c_ctx, 0)], LAT_TILES, 1)
        if last:
            xs = _ffn_streams(xs, None, mod_l, 2, norm_ffn2[l], *ffn2_w, l, in_place=False, out_rows=N_LAT)
        else:
            xs = _ffn_streams(xs, xs, mod_l, 2, norm_ffn2[l], *ffn2_w, l, in_place=True, out_rows=NT)
    return xs.reshape(BATCH, SEQ, D_MODEL)
```

```python
import functools

import numpy as np
import jax
import jax.numpy as jnp
from jax import lax
from jax.experimental import pallas as pl
from jax.experimental.pallas import tpu as pltpu

D_MODEL = 2048
BATCH = 2
SEQ = 16384
DEPTH = 2
GRID_W = 64
GRID_ROWS = SEQ // GRID_W
CTX_LEN = 256
HEAD_DIM = 128
A_Q_HEADS = 6
A_KV_HEADS = 2
A_GROUP = A_Q_HEADS // A_KV_HEADS
A_WINDOW = 128
B_HEADS = 6
B_WIN_H = 8
B_WIN_W = 16
C_CHANNELS = 512
C_CONV_WIDTH = 31
C_PAD = (C_CONV_WIDTH - 1) // 2
D_FF = 5632
ROPE_THETA = 10000.0
NORM_EPS = 1e-6
NEG_INF = -1e30
N_MOD = 9
IN_COLS = 4608
MIX_WIDTH = 2048

HEAD_AQ = 0
HEAD_AK = HEAD_AQ + A_Q_HEADS
HEAD_AV = HEAD_AK + A_KV_HEADS
HEAD_BQ = HEAD_AV + A_KV_HEADS
HEAD_BK = HEAD_BQ + B_HEADS
HEAD_BV = HEAD_BK + B_HEADS
N_QKV_HEADS = HEAD_BV + B_HEADS
C_COL0 = N_QKV_HEADS * HEAD_DIM

N_LAT = BATCH * SEQ
N_CTX = BATCH * CTX_LEN

V7X_VMEM_BYTES = 64 * 1024 * 1024
SUBLANES = 8
LANES = 128

TM = 512
TILES_PER_BATCH = SEQ // TM
TF = 512
TC = 256
CONV_HALO = 16

F32 = jnp.float32
BF16 = jnp.bfloat16


def _vmem_limit(block_bytes):
    return int(min(V7X_VMEM_BYTES - 4 * 1024 * 1024, block_bytes + 12 * 1024 * 1024))


def _group_of_tile(is_ctx, tile_rows):
    return (lambda i: BATCH) if is_ctx else (lambda i: i * tile_rows // SEQ)


def _in_hbm(stream):
    return pltpu.with_memory_space_constraint(stream, pltpu.HBM)


def _modulated(x, gain, shift, scale):
    ms = jnp.mean(x * x, axis=-1, keepdims=True)
    return (x * lax.rsqrt(ms + NORM_EPS) * gain) * (1.0 + scale) + shift


MOD_TN = 1024
MOD_GROUPS = BATCH + 1
MOD_UNROLL = 8


def _mod_kernel(c_ref, w_ref, b_ref, o_ref, a_ref):
    @pl.when(jnp.logical_and(pl.program_id(0) == 0, pl.program_id(1) == 0))
    def _():
        c = c_ref[...]
        a_ref[...] = c * jax.nn.sigmoid(c)

    lane_tiles = MOD_TN // LANES

    def body(step, accs):
        accs = list(accs)
        for u in range(MOD_UNROLL):
            r = pl.multiple_of((step * MOD_UNROLL + u) * SUBLANES, SUBLANES)
            for m in range(MOD_GROUPS):
                a = a_ref[m, pl.ds(r, SUBLANES), :]
                for t in range(lane_tiles):
                    idx = m * lane_tiles + t
                    accs[idx] = accs[idx] + w_ref[pl.ds(r, SUBLANES), t * LANES:(t + 1) * LANES] * a
        return tuple(accs)

    zero = jnp.zeros((SUBLANES, LANES), F32)
    accs = lax.fori_loop(0, D_MODEL // (SUBLANES * MOD_UNROLL), body, (zero,) * (MOD_GROUPS * lane_tiles))
    o_ref[...] = jnp.zeros_like(o_ref)
    for m in range(MOD_GROUPS):
        row = jnp.concatenate([jnp.sum(accs[m * lane_tiles + t], axis=0, keepdims=True)
                               for t in range(lane_tiles)], axis=1)
        o_ref[m:m + 1, :] = row + b_ref[...]


def _mod_vectors(c, c_ctx, w_mod, b_mod):
    rows = jnp.concatenate([c, c_ctx[None, :]], axis=0)
    c_cols = jnp.broadcast_to(rows[:, :, None], (MOD_GROUPS, D_MODEL, LANES))
    n = N_MOD * D_MODEL
    out = pl.pallas_call(
        _mod_kernel,
        grid=(DEPTH, n // MOD_TN),
        in_specs=[
            pl.BlockSpec((MOD_GROUPS, D_MODEL, LANES), lambda l, j: (0, 0, 0)),
            pl.BlockSpec((None, D_MODEL, MOD_TN), lambda l, j: (l, 0, j)),
            pl.BlockSpec((None, 1, MOD_TN), lambda l, j: (l, 0, j)),
        ],
        out_specs=pl.BlockSpec((None, SUBLANES, MOD_TN), lambda l, j: (l, 0, j)),
        out_shape=jax.ShapeDtypeStruct((DEPTH, SUBLANES, n), F32),
        scratch_shapes=[pltpu.VMEM((MOD_GROUPS, D_MODEL, LANES), F32)],
        compiler_params=pltpu.CompilerParams(
            dimension_semantics=("arbitrary", "arbitrary"),
            vmem_limit_bytes=_vmem_limit(2 * D_MODEL * MOD_TN * 4 + 3 * MOD_GROUPS * D_MODEL * LANES * 4)),
        name="mod_vectors",
    )(c_cols, w_mod, b_mod.reshape(DEPTH, 1, n))
    return out[:, :MOD_GROUPS].reshape(DEPTH, MOD_GROUPS, 3, 3, D_MODEL)


FFN_SUB = 512
FFN_TM_LAT = 1024


def _ffn_kernel(tm, x_ref, mod_ref, gain_ref, wg_ref, wu_ref, wd_ref, o_ref, h_ref):
    j = pl.program_id(1)
    last = pl.num_programs(1) - 1

    def chunk(first, final):
        for r0 in range(0, tm, FFN_SUB):
            rows = slice(r0, r0 + FFN_SUB)
            if first:
                h = _modulated(x_ref[rows, :], gain_ref[...], mod_ref[0:1, :], mod_ref[1:2, :])
                h_ref[rows, :] = h.astype(BF16)
            h = h_ref[rows, :]
            g = jnp.dot(h, wg_ref[...], preferred_element_type=F32)
            u = jnp.dot(h, wu_ref[...], preferred_element_type=F32)
            a = (g * jax.nn.sigmoid(g)) * u
            d = jnp.dot(a.astype(BF16), wd_ref[...], preferred_element_type=F32)
            if first:
                o_ref[rows, :] = d
            elif final:
                o_ref[rows, :] = x_ref[rows, :] + (0.5 * mod_ref[2:3, :]) * (o_ref[rows, :] + d)
            else:
                o_ref[rows, :] += d

    pl.when(j == 0)(functools.partial(chunk, True, False))
    pl.when(jnp.logical_and(j > 0, j < last))(functools.partial(chunk, False, False))
    pl.when(j == last)(functools.partial(chunk, False, True))


def _ffn(x, is_ctx, mod_l, sub, gain, wg, wu, wd, layer, in_place):
    tm = TM if is_ctx else FFN_TM_LAT
    rows = x.shape[0]
    assert tm % FFN_SUB == 0 and rows % tm == 0 and SEQ % tm == 0 and D_FF // TF >= 2
    blocks = 4 * tm * D_MODEL * 4 + 6 * D_MODEL * TF * 2 + tm * D_MODEL * 2
    group = _group_of_tile(is_ctx, tm)
    return pl.pallas_call(
        functools.partial(_ffn_kernel, tm),
        grid=(rows // tm, D_FF // TF),
        in_specs=[
            pl.BlockSpec((tm, D_MODEL), lambda i, j: (i, 0)),
            pl.BlockSpec((None, None, 3, D_MODEL), lambda i, j: (group(i), sub, 0, 0)),
            pl.BlockSpec((1, D_MODEL), lambda i, j: (0, 0)),
            pl.BlockSpec((None, D_MODEL, TF), lambda i, j: (layer, 0, j)),
            pl.BlockSpec((None, D_MODEL, TF), lambda i, j: (layer, 0, j)),
            pl.BlockSpec((None, TF, D_MODEL), lambda i, j: (layer, j, 0)),
        ],
        out_specs=pl.BlockSpec((tm, D_MODEL), lambda i, j: (i, 0)),
        out_shape=jax.ShapeDtypeStruct((rows, D_MODEL), F32),
        scratch_shapes=[pltpu.VMEM((tm, D_MODEL), BF16)],
        input_output_aliases={0: 0} if in_place else {},
        compiler_params=pltpu.CompilerParams(
            dimension_semantics=("arbitrary", "arbitrary"),
            vmem_limit_bytes=_vmem_limit(blocks)),
        name="ffn",
    )(_in_hbm(x), mod_l, gain.reshape(1, D_MODEL), wg, wu, wd)


GAIN_AQ, GAIN_AK, GAIN_BQ, GAIN_BK = 0, 1, 2, 3


def _head_kind(h):
    if h < HEAD_AK:
        return GAIN_AQ, True
    if h < HEAD_AV:
        return GAIN_AK, True
    if h < HEAD_BQ:
        return None, False
    if h < HEAD_BK:
        return GAIN_BQ, False
    if h < HEAD_BV:
        return GAIN_BK, False
    return None, False


def _inproj_kernel(positioned, x_ref, mod_ref, gain_ref, w_ref, hg_ref, *refs):
    if positioned:
        cos_ref, sin_ref, qkv_ref, cu_ref, h_ref = refs
    else:
        qkv_ref, cu_ref, h_ref = refs
    h = _modulated(x_ref[...], gain_ref[...], mod_ref[0:1, :], mod_ref[1:2, :])
    h_ref[...] = h.astype(BF16)
    lane = lax.broadcasted_iota(jnp.int32, (TM, HEAD_DIM), 1)
    even_quarter = ((lane // (HEAD_DIM // 4)) % 2) == 0
    for pair in range(N_QKV_HEADS // 2):
        c0 = pair * 2 * HEAD_DIM
        y2 = jnp.dot(h_ref[...], w_ref[:, c0:c0 + 2 * HEAD_DIM], preferred_element_type=F32)
        for half in range(2):
            hd = 2 * pair + half
            y = y2[:, half * HEAD_DIM:(half + 1) * HEAD_DIM]
            gain_row, rotary = _head_kind(hd)
            if gain_row is not None:
                ms = jnp.mean(y * y, axis=-1, keepdims=True)
                y = y * lax.rsqrt(ms + NORM_EPS) * hg_ref[gain_row:gain_row + 1, :]
            if rotary and positioned:
                swapped = jnp.where(even_quarter,
                                    pltpu.roll(y, HEAD_DIM - HEAD_DIM // 4, 1),
                                    pltpu.roll(y, HEAD_DIM // 4, 1))
                y = y * cos_ref[...] + swapped * sin_ref[...]
            qkv_ref[hd] = y.astype(BF16)
    for blk in range((IN_COLS - C_COL0) // (2 * HEAD_DIM)):
        c0 = C_COL0 + blk * 2 * HEAD_DIM
        cu_ref[:, blk * 2 * HEAD_DIM:(blk + 1) * 2 * HEAD_DIM] = jnp.dot(
            h_ref[...], w_ref[:, c0:c0 + 2 * HEAD_DIM], preferred_element_type=F32)


def _inproj(xs, is_ctx, mod_l, gain, w_in, layer, head_gains, rope):
    rows = xs.shape[0]
    n_cu = IN_COLS - C_COL0
    blocks = (2 * TM * D_MODEL * 4 + D_MODEL * IN_COLS * 2 + 2 * N_QKV_HEADS * TM * HEAD_DIM * 2
              + 2 * TM * n_cu * 4 + 4 * TM * HEAD_DIM * 4 + TM * D_MODEL * 2)
    group = _group_of_tile(is_ctx, TM)
    operands = [xs, mod_l, gain.reshape(1, D_MODEL), w_in, head_gains]
    in_specs = [
        pl.BlockSpec((TM, D_MODEL), lambda i: (i, 0)),
        pl.BlockSpec((None, None, 3, D_MODEL), lambda i: (group(i), 1, 0, 0)),
        pl.BlockSpec((1, D_MODEL), lambda i: (0, 0)),
        pl.BlockSpec((None, D_MODEL, IN_COLS), lambda i: (layer, 0, 0), pipeline_mode=pl.Buffered(1)),
        pl.BlockSpec((SUBLANES, HEAD_DIM), lambda i: (0, 0)),
    ]
    if not is_ctx:
        operands += list(rope)
        in_specs += [pl.BlockSpec((TM, HEAD_DIM), lambda i: (i % TILES_PER_BATCH, 0))] * 2
    return pl.pallas_call(
        functools.partial(_inproj_kernel, not is_ctx),
        grid=(rows // TM,),
        in_specs=in_specs,
        out_specs=[
            pl.BlockSpec((N_QKV_HEADS, TM, HEAD_DIM), lambda i: (0, i, 0)),
            pl.BlockSpec((TM, n_cu), lambda i: (i, 0)),
        ],
        out_shape=[
            jax.ShapeDtypeStruct((N_QKV_HEADS, rows, HEAD_DIM), BF16),
            jax.ShapeDtypeStruct((rows, n_cu), F32),
        ],
        scratch_shapes=[pltpu.VMEM((TM, D_MODEL), BF16)],
        compiler_params=pltpu.CompilerParams(
            dimension_semantics=("arbitrary",),
            vmem_limit_bytes=_vmem_limit(blocks)),
        name="inproj",
    )(*operands)


def _rope_tables():
    t = np.arange(SEQ)
    n_freq = HEAD_DIM // 4
    inv_freq = ROPE_THETA ** (-np.arange(n_freq, dtype=np.float64) / n_freq)
    ang_r = (t // GRID_W)[:, None] * inv_freq[None, :]
    ang_c = (t % GRID_W)[:, None] * inv_freq[None, :]
    cos_t = np.concatenate([np.cos(ang_r), np.cos(ang_r), np.cos(ang_c), np.cos(ang_c)], axis=-1)
    sin_t = np.concatenate([-np.sin(ang_r), np.sin(ang_r), -np.sin(ang_c), np.sin(ang_c)], axis=-1)
    return jnp.asarray(cos_t, F32), jnp.asarray(sin_t, F32)


LOG2_E = float(np.log2(np.e))
ATTN_TQ = 256
ATTN_BLOCKS = SEQ // ATTN_TQ
ATTN_UNROLL = 8
PLACE_INTERIOR, PLACE_FIRST, PLACE_LAST = 0, 1, 2
N_PLACEMENTS = 3
PLACEMENT_BLOCKS = (1, 0, ATTN_BLOCKS - 1)

A_LEAD = A_WINDOW
A_KEYS = ATTN_TQ + 2 * A_WINDOW
B_ROWS_PER_BLOCK = ATTN_TQ // GRID_W
B_UNION_ROWS = B_ROWS_PER_BLOCK + B_WIN_H
B_LEAD = (B_WIN_H // 2) * GRID_W
B_KEYS = B_UNION_ROWS * GRID_W
assert ATTN_BLOCKS % ATTN_UNROLL == 0 and ATTN_BLOCKS >= 3
assert A_LEAD % LANES == 0 and B_LEAD % LANES == 0 and A_KEYS % LANES == 0 and B_KEYS % LANES == 0


def _dot_nt(a, b):
    return lax.dot_general(a, b, (((1,), (1,)), ((), ())), preferred_element_type=F32)


def _window_start(first_query, lead, n_keys):
    return int(np.clip(first_query - lead, 0, SEQ - n_keys))


def _local_attn_kernel(lead, n_keys, has_sink, *refs):
    if has_sink:
        sink_ref, refs = refs[0], refs[1:]
    q_ref, k_ref, v_ref, kc_ref, vc_ref, bias_ref, o_ref = refs
    sink = sink_ref[pl.program_id(1) * A_GROUP + pl.program_id(2)] if has_sink else None
    kc = kc_ref[...]
    vc = vc_ref[...]

    def scores(i):
        q0 = pl.multiple_of(i * ATTN_TQ, ATTN_TQ)
        k0 = pl.multiple_of(jnp.clip(q0 - lead, 0, SEQ - n_keys), LANES)
        placement = jnp.where(i == 0, PLACE_FIRST, jnp.where(i == ATTN_BLOCKS - 1, PLACE_LAST, PLACE_INTERIOR))
        q = q_ref[pl.ds(q0, ATTN_TQ), :]
        s = _dot_nt(q, k_ref[pl.ds(k0, n_keys), :]) + bias_ref[placement]
        return q0, k0, s, _dot_nt(q, kc)

    def softmax(q0, k0, s, sc):
        m = jnp.maximum(jnp.max(s, axis=-1, keepdims=True), jnp.max(sc, axis=-1, keepdims=True))
        if has_sink:
            m = jnp.maximum(m, sink)
        p = jnp.exp2(s - m)
        pc = jnp.exp2(sc - m)
        den = jnp.sum(p, axis=-1, keepdims=True) + jnp.sum(pc, axis=-1, keepdims=True)
        if has_sink:
            den = den + jnp.exp2(sink - m)
        return q0, k0, p.astype(BF16), pc.astype(BF16), den

    def output(q0, k0, p, pc, den):
        o = (jnp.dot(p, v_ref[pl.ds(k0, n_keys), :], preferred_element_type=F32)
             + jnp.dot(pc, vc, preferred_element_type=F32))
        o_ref[pl.ds(q0, ATTN_TQ), :] = (o / den).astype(BF16)

    def body(it, carry):
        scored, weighted = {}, {}
        for step in range(ATTN_UNROLL + 2):
            if step < ATTN_UNROLL:
                scored[step] = scores(it * ATTN_UNROLL + step)
            if 0 <= step - 2 < ATTN_UNROLL:
                output(*weighted.pop(step - 2))
            if 0 <= step - 1 < ATTN_UNROLL:
                weighted[step - 1] = softmax(*scored.pop(step - 1))
        return carry

    lax.fori_loop(0, ATTN_BLOCKS // ATTN_UNROLL, body, 0)


def _local_attn_vmem(n_keys):
    return 2 * 4 * SEQ * HEAD_DIM * 2 + 4 * CTX_LEN * HEAD_DIM * 2 + 2 * N_PLACEMENTS * ATTN_TQ * n_keys * 4


SEQ_BLOCK = (None, SEQ, HEAD_DIM)
CTX_BLOCK = (None, CTX_LEN, HEAD_DIM)


def _a_band_table():
    tables = []
    for blk in PLACEMENT_BLOCKS:
        q0 = blk * ATTN_TQ
        kpos = _window_start(q0, A_LEAD, A_KEYS) + np.arange(A_KEYS)[None, :]
        qpos = q0 + np.arange(ATTN_TQ)[:, None]
        tables.append(np.where(np.abs(kpos - qpos) <= A_WINDOW, 0.0, NEG_INF))
    return jnp.asarray(np.stack(tables), F32)


def _attn_a(qkv, qkv_ctx, sink):
    return pl.pallas_call(
        functools.partial(_local_attn_kernel, A_LEAD, A_KEYS, True),
        grid=(BATCH, A_KV_HEADS, A_GROUP),
        in_specs=[
            pl.BlockSpec(memory_space=pltpu.SMEM),
            pl.BlockSpec(SEQ_BLOCK, lambda b, kv, g: (HEAD_AQ + kv * A_GROUP + g, b, 0)),
            pl.BlockSpec(SEQ_BLOCK, lambda b, kv, g: (HEAD_AK + kv, b, 0)),
            pl.BlockSpec(SEQ_BLOCK, lambda b, kv, g: (HEAD_AV + kv, b, 0)),
            pl.BlockSpec(CTX_BLOCK, lambda b, kv, g: (HEAD_AK + kv, b, 0)),
            pl.BlockSpec(CTX_BLOCK, lambda b, kv, g: (HEAD_AV + kv, b, 0)),
            pl.BlockSpec((N_PLACEMENTS, ATTN_TQ, A_KEYS), lambda b, kv, g: (0, 0, 0)),
        ],
        out_specs=pl.BlockSpec((SEQ, HEAD_DIM), lambda b, kv, g: (b, kv * A_GROUP + g)),
        out_shape=jax.ShapeDtypeStruct((N_LAT, A_Q_HEADS * HEAD_DIM), BF16),
        compiler_params=pltpu.CompilerParams(
            dimension_semantics=("arbitrary", "arbitrary", "arbitrary"),
            vmem_limit_bytes=_vmem_limit(_local_attn_vmem(A_KEYS))),
        name="attn_window",
    )(sink, qkv, qkv, qkv, qkv_ctx, qkv_ctx, _a_band_table())


def _b_bias_tables(rpb):
    j = np.arange(GRID_W)[:, None]
    jk = np.arange(GRID_W)[None, :]
    col_start = np.clip(j - B_WIN_W // 2, 0, GRID_W - B_WIN_W)
    inside = (jk >= col_start) & (jk < col_start + B_WIN_W)
    n_col = 2 * B_WIN_W - 1
    onehot = (inside[:, :, None] & ((jk - j + (B_WIN_W - 1))[:, :, None] == np.arange(n_col))).astype(np.float32)
    by_row = jnp.einsum('lhac,jkc->lhajk', rpb, jnp.asarray(onehot), precision=lax.Precision.HIGHEST)
    by_row = by_row * LOG2_E + jnp.asarray(np.where(inside, 0.0, NEG_INF), F32)
    masked = jnp.full((DEPTH, B_HEADS, GRID_W, GRID_W), NEG_INF, F32)
    query_rows = []
    for blk in PLACEMENT_BLOCKS:
        r0 = blk * B_ROWS_PER_BLOCK
        u0 = _window_start(r0 * GRID_W, B_LEAD, B_KEYS) // GRID_W
        for t in range(B_ROWS_PER_BLOCK):
            r = r0 + t
            rs = int(np.clip(r - B_WIN_H // 2, 0, GRID_ROWS - B_WIN_H))
            before = rs - u0
            assert 0 <= before <= B_UNION_ROWS - B_WIN_H
            bias_row0 = rs - r + (B_WIN_H - 1)
            pieces = [by_row[:, :, bias_row0 + u - before] if before <= u < before + B_WIN_H else masked
                      for u in range(B_UNION_ROWS)]
            query_rows.append(jnp.concatenate(pieces, axis=-1))
    return jnp.stack(query_rows, axis=2).reshape(DEPTH, B_HEADS, N_PLACEMENTS, ATTN_TQ, B_KEYS)


def _attn_b(qkv, qkv_ctx, bias, layer):
    return pl.pallas_call(
        functools.partial(_local_attn_kernel, B_LEAD, B_KEYS, False),
        grid=(BATCH, B_HEADS),
        in_specs=[
            pl.BlockSpec(SEQ_BLOCK, lambda b, h: (HEAD_BQ + h, b, 0)),
            pl.BlockSpec(SEQ_BLOCK, lambda b, h: (HEAD_BK + h, b, 0)),
            pl.BlockSpec(SEQ_BLOCK, lambda b, h: (HEAD_BV + h, b, 0)),
            pl.BlockSpec(CTX_BLOCK, lambda b, h: (HEAD_BK + h, b, 0)),
            pl.BlockSpec(CTX_BLOCK, lambda b, h: (HEAD_BV + h, b, 0)),
            pl.BlockSpec((None, None, N_PLACEMENTS, ATTN_TQ, B_KEYS), lambda b, h: (layer, h, 0, 0, 0)),
        ],
        out_specs=pl.BlockSpec((SEQ, HEAD_DIM), lambda b, h: (b, h)),
        out_shape=jax.ShapeDtypeStruct((N_LAT, B_HEADS * HEAD_DIM), BF16),
        compiler_params=pltpu.CompilerParams(
            dimension_semantics=("arbitrary", "arbitrary"),
            vmem_limit_bytes=_vmem_limit(_local_attn_vmem(B_KEYS))),
        name="attn_neighbourhood",
    )(qkv, qkv, qkv, qkv_ctx, qkv_ctx, bias)


N_CTX_HEADS = A_Q_HEADS + B_HEADS


def _attn_ctx_kernel(sink_ref, q_ref, k_ref, v_ref, o_ref):
    sink = sink_ref[pl.program_id(1)]
    q = q_ref[...]
    s = _dot_nt(q, k_ref[...])
    m = jnp.maximum(jnp.max(s, axis=-1, keepdims=True), sink)
    p = jnp.exp2(s - m)
    den = jnp.sum(p, axis=-1, keepdims=True) + jnp.exp2(sink - m)
    o = jnp.dot(p.astype(BF16), v_ref[...], preferred_element_type=F32)
    o_ref[...] = (o / den).astype(BF16)


def _attn_ctx(qkv_ctx, sinks):
    is_b = lambda h: h >= A_Q_HEADS
    q_head = lambda h: jnp.where(is_b(h), HEAD_BQ + h - A_Q_HEADS, HEAD_AQ + h)
    k_head = lambda h: jnp.where(is_b(h), HEAD_BK + h - A_Q_HEADS, HEAD_AK + h // A_GROUP)
    v_head = lambda h: jnp.where(is_b(h), HEAD_BV + h - A_Q_HEADS, HEAD_AV + h // A_GROUP)
    return pl.pallas_call(
        _attn_ctx_kernel,
        grid=(BATCH, N_CTX_HEADS),
        in_specs=[
            pl.BlockSpec(memory_space=pltpu.SMEM),
            pl.BlockSpec(CTX_BLOCK, lambda b, h: (q_head(h), b, 0)),
            pl.BlockSpec(CTX_BLOCK, lambda b, h: (k_head(h), b, 0)),
            pl.BlockSpec(CTX_BLOCK, lambda b, h: (v_head(h), b, 0)),
        ],
        out_specs=pl.BlockSpec((CTX_LEN, HEAD_DIM), lambda b, h: (b, h)),
        out_shape=jax.ShapeDtypeStruct((N_CTX, N_CTX_HEADS * HEAD_DIM), BF16),
        compiler_params=pltpu.CompilerParams(dimension_semantics=("arbitrary", "arbitrary")),
        name="attn_context",
    )(sinks, qkv_ctx, qkv_ctx, qkv_ctx)


CONV_ROWS = 32
CONV_SH_ROWS = TC + SUBLANES * ((C_CONV_WIDTH - 1) // SUBLANES)
assert SUBLANES - 1 + CONV_HALO - C_PAD + CONV_SH_ROWS <= TC + 2 * CONV_HALO
CONV_CHUNKS = TC // CONV_ROWS


def _glu(u):
    return u[:, :C_CHANNELS] * jax.nn.sigmoid(u[:, C_CHANNELS:])


def _conv_fill(n_rows, first, last, prev_ref, cur_ref, next_ref, ext_ref):
    ext_ref[0:CONV_HALO, :] = jnp.where(first, 0.0, _glu(prev_ref[...]))
    ext_ref[CONV_HALO:CONV_HALO + n_rows, :] = _glu(cur_ref[...])
    ext_ref[CONV_HALO + n_rows:, :] = jnp.where(last, 0.0, _glu(next_ref[...]))


def _conv_shift(row0, ext_ref, sh_ref):
    for b in range(SUBLANES):
        lo = row0 + b + CONV_HALO - C_PAD
        sh_ref[b] = ext_ref[lo:lo + CONV_SH_ROWS, :]


def _conv_chunk(row0, c, sh_ref, w_ref, b_ref, g_ref, beta_ref, out_ref):
    groups = CONV_ROWS // SUBLANES
    r0 = c * CONV_ROWS
    accs = [jnp.zeros((SUBLANES, C_CHANNELS), F32) for _ in range(groups)]
    for k in range(C_CONV_WIDTH):
        a, b = divmod(k, SUBLANES)
        w_k = w_ref[k]
        for g in range(groups):
            lo = r0 + SUBLANES * (a + g)
            accs[g] = accs[g] + sh_ref[b, lo:lo + SUBLANES, :] * w_k
    acc = jnp.concatenate(accs, axis=0) + b_ref[...]
    mu = jnp.mean(acc, axis=-1, keepdims=True)
    xc = acc - mu
    y = xc * lax.rsqrt(jnp.mean(xc * xc, axis=-1, keepdims=True) + NORM_EPS)
    y = y * g_ref[...] + beta_ref[...]
    out_ref[row0 + r0:row0 + r0 + CONV_ROWS, :] = (y * jax.nn.sigmoid(y)).astype(BF16)


def _conv_operands(cu, n_rows, dw_w, dw_b, ln_g, ln_b):
    halo_per_tile = n_rows // CONV_HALO
    n_halo = cu.shape[0] // CONV_HALO
    vec = pl.BlockSpec((1, C_CHANNELS), lambda i: (0, 0))
    specs = [
        pl.BlockSpec((CONV_HALO, 2 * C_CHANNELS), lambda i: (jnp.maximum(i * halo_per_tile - 1, 0), 0)),
        pl.BlockSpec((n_rows, 2 * C_CHANNELS), lambda i: (i, 0)),
        pl.BlockSpec((CONV_HALO, 2 * C_CHANNELS), lambda i: (jnp.minimum((i + 1) * halo_per_tile, n_halo - 1), 0)),
        pl.BlockSpec((C_CONV_WIDTH, SUBLANES, C_CHANNELS), lambda i: (0, 0, 0)),
        vec, vec, vec,
    ]
    operands = [cu, cu, cu, jnp.broadcast_to(dw_w[:, None, :], (C_CONV_WIDTH, SUBLANES, C_CHANNELS)),
                dw_b.reshape(1, -1), ln_g.reshape(1, -1), ln_b.reshape(1, -1)]
    return operands, specs


def _conv_scratch(n_rows):
    return [pltpu.VMEM((n_rows + 2 * CONV_HALO, C_CHANNELS), F32),
            pltpu.VMEM((SUBLANES, CONV_SH_ROWS, C_CHANNELS), F32)]


def _conv_ctx_kernel(prev_ref, cur_ref, next_ref, w_ref, b_ref, g_ref, beta_ref, o_ref, ext_ref, sh_ref):
    _conv_fill(TC, True, True, prev_ref, cur_ref, next_ref, ext_ref)
    _conv_shift(0, ext_ref, sh_ref)
    for c in range(CONV_CHUNKS):
        _conv_chunk(0, c, sh_ref, w_ref, b_ref, g_ref, beta_ref, o_ref)


def _conv_ctx(cu_ctx, dw_w, dw_b, ln_g, ln_b):
    assert CTX_LEN == TC
    operands, specs = _conv_operands(cu_ctx, TC, dw_w, dw_b, ln_g, ln_b)
    return pl.pallas_call(
        _conv_ctx_kernel,
        grid=(N_CTX // TC,),
        in_specs=specs,
        out_specs=pl.BlockSpec((TC, C_CHANNELS), lambda i: (i, 0)),
        out_shape=jax.ShapeDtypeStruct((N_CTX, C_CHANNELS), BF16),
        scratch_shapes=_conv_scratch(TC),
        compiler_params=pltpu.CompilerParams(dimension_semantics=("arbitrary",)),
        name="conv_context",
    )(*operands)


N_CONV_OPERANDS = 7
OUTPROJ_SLABS = 8
assert ((TM // TC) * CONV_CHUNKS) % OUTPROJ_SLABS == 0


def _outproj_kernel(widths, fused_conv, x_ref, mod_ref, w_ref, *refs):
    if fused_conv:
        pieces = list(refs[:len(widths) - 1])
        conv_refs = refs[len(widths) - 1:len(widths) - 1 + N_CONV_OPERANDS]
        o_ref, ext_ref, sh_ref, oc_ref = refs[len(widths) - 1 + N_CONV_OPERANDS:]
    else:
        pieces, o_ref = list(refs[:-1]), refs[-1]
    def projected(cols):
        y = None
        k0 = 0
        for piece, width in zip(pieces, widths):
            t = jnp.dot(piece[...], w_ref[k0:k0 + width, cols], preferred_element_type=F32)
            y = t if y is None else y + t
            k0 += width
        return y

    if not fused_conv:
        o_ref[...] = x_ref[...] + mod_ref[2:3, :] * projected(slice(None))
        return

    prev_ref, cur_ref, next_ref, cw_ref, cb_ref, cg_ref, cbeta_ref = conv_refs
    i = pl.program_id(0)
    _conv_fill(TM, i % TILES_PER_BATCH == 0, i % TILES_PER_BATCH == TILES_PER_BATCH - 1,
               prev_ref, cur_ref, next_ref, ext_ref)
    n_chunks = (TM // TC) * CONV_CHUNKS
    slab = D_MODEL // OUTPROJ_SLABS
    chunks_per_slab = n_chunks // OUTPROJ_SLABS
    for s in range(OUTPROJ_SLABS):
        cols = slice(s * slab, (s + 1) * slab)
        o_ref[:, cols] = projected(cols)
        for chunk in range(s * chunks_per_slab, (s + 1) * chunks_per_slab):
            sub, c = divmod(chunk, CONV_CHUNKS)
            if c == 0:
                _conv_shift(sub * TC, ext_ref, sh_ref)
            _conv_chunk(sub * TC, c, sh_ref, cw_ref, cb_ref, cg_ref, cbeta_ref, oc_ref)
    k_conv = MIX_WIDTH - widths[-1]
    y = o_ref[...] + jnp.dot(oc_ref[...], w_ref[k_conv:, :], preferred_element_type=F32)
    o_ref[...] = x_ref[...] + mod_ref[2:3, :] * y


def _outproj(xs, is_ctx, mod_l, w_out, layer, pieces, conv=None):
    rows = xs.shape[0]
    widths = tuple(int(p.shape[1]) for p in pieces) + ((C_CHANNELS,) if conv is not None else ())
    assert sum(widths) == MIX_WIDTH and all(p.shape[0] == rows for p in pieces)
    blocks = 4 * TM * D_MODEL * 4 + MIX_WIDTH * D_MODEL * 2 + 2 * TM * MIX_WIDTH * 2
    group = _group_of_tile(is_ctx, TM)
    operands = [_in_hbm(xs), mod_l, w_out] + list(pieces)
    in_specs = [
        pl.BlockSpec((TM, D_MODEL), lambda i: (i, 0)),
        pl.BlockSpec((None, None, 3, D_MODEL), lambda i: (group(i), 1, 0, 0)),
        pl.BlockSpec((None, MIX_WIDTH, D_MODEL), lambda i: (layer, 0, 0), pipeline_mode=pl.Buffered(1)),
    ] + [pl.BlockSpec((TM, w), lambda i: (i, 0)) for w in widths[:len(pieces)]]
    scratch = []
    if conv is not None:
        assert not is_ctx and SEQ % TM == 0 and TM % TC == 0 and conv[0].shape[0] == rows
        conv_operands, conv_specs = _conv_operands(conv[0], TM, *conv[1:])
        operands += conv_operands
        in_specs += conv_specs
        scratch = _conv_scratch(TM) + [pltpu.VMEM((TM, C_CHANNELS), BF16)]
        blocks += 4 * TM * C_CHANNELS * 4 + (TM + SUBLANES * CONV_SH_ROWS) * C_CHANNELS * 4
    return pl.pallas_call(
        functools.partial(_outproj_kernel, widths, conv is not None),
        grid=(rows // TM,),
        in_specs=in_specs,
        out_specs=pl.BlockSpec((TM, D_MODEL), lambda i: (i, 0)),
        out_shape=jax.ShapeDtypeStruct((rows, D_MODEL), F32),
        scratch_shapes=scratch,
        input_output_aliases={0: 0},
        compiler_params=pltpu.CompilerParams(
            dimension_semantics=("arbitrary",),
            vmem_limit_bytes=_vmem_limit(blocks)),
        name="outproj",
    )(*operands)


def kernel(x, c, ctx, c_ctx, w_mod, b_mod, norm_ffn1, norm_mix, norm_ffn2, ffn1_w_gate, ffn1_w_up,
           ffn1_w_down, ffn2_w_gate, ffn2_w_up, ffn2_w_down, w_in, w_out, a_q_norm, a_k_norm, a_sink,
           b_q_norm, b_k_norm, b_rpb, c_dw_w, c_dw_b, c_ln_g, c_ln_b):
    mods = _mod_vectors(c, c_ctx, w_mod, b_mod)
    rope = _rope_tables()
    b_bias = _b_bias_tables(b_rpb)
    ffn1_w = [w.astype(BF16) for w in (ffn1_w_gate, ffn1_w_up, ffn1_w_down)]
    ffn2_w = [w.astype(BF16) for w in (ffn2_w_gate, ffn2_w_up, ffn2_w_down)]
    w_in_bf = w_in.astype(BF16)
    w_out_bf = w_out.astype(BF16)
    q_scale = HEAD_DIM ** -0.5 * LOG2_E
    xs = x.reshape(N_LAT, D_MODEL)
    cs = ctx.reshape(N_CTX, D_MODEL)
    for l in range(DEPTH):
        last = l == DEPTH - 1
        mod_l = mods[l]
        owned = l > 0
        xs = _ffn(xs, False, mod_l, 0, norm_ffn1[l], *ffn1_w, l, in_place=owned)
        cs = _ffn(cs, True, mod_l, 0, norm_ffn1[l], *ffn1_w, l, in_place=owned)
        head_gains = jnp.concatenate([
            (a_q_norm[l] * q_scale)[None], a_k_norm[l][None], (b_q_norm[l] * q_scale)[None],
            b_k_norm[l][None], jnp.zeros((SUBLANES - 4, HEAD_DIM), F32)], axis=0)
        qkv, cu = _inproj(xs, False, mod_l, norm_mix[l], w_in_bf, l, head_gains, rope)
        qkv_ctx, cu_ctx = _inproj(cs, True, mod_l, norm_mix[l], w_in_bf, l, head_gains, rope)
        sink_l = a_sink[l] * LOG2_E
        o_a = _attn_a(qkv, qkv_ctx, sink_l)
        o_b = _attn_b(qkv, qkv_ctx, b_bias, l)
        conv_params = (c_dw_w[l], c_dw_b[l], c_ln_g[l], c_ln_b[l])
        xs = _outproj(xs, False, mod_l, w_out_bf, l, [o_a, o_b], conv=(cu,) + conv_params)
        xs = _ffn(xs, False, mod_l, 2, norm_ffn2[l], *ffn2_w, l, in_place=True)
        if not last:
            sinks = jnp.concatenate([sink_l, jnp.full((B_HEADS,), NEG_INF, F32)])
            o_ctx = _attn_ctx(qkv_ctx, sinks)
            o_c_ctx = _conv_ctx(cu_ctx, *conv_params)
            cs = _outproj(cs, True, mod_l, w_out_bf, l, [o_ctx, o_c_ctx])
            cs = _ffn(cs, True, mod_l, 2, norm_ffn2[l], *ffn2_w, l, in_place=True)
    return xs.reshape(BATCH, SEQ, D_MODEL)
```

```python
import functools

import numpy as np
import jax
import jax.numpy as jnp
from jax import lax
from jax.experimental import pallas as pl
from jax.experimental.pallas import tpu as pltpu

D_MODEL = 2048
BATCH = 2
SEQ = 16384
DEPTH = 2
GRID_W = 64
GRID_ROWS = SEQ // GRID_W
CTX_LEN = 256
HEAD_DIM = 128
A_Q_HEADS = 6
A_KV_HEADS = 2
A_GROUP = A_Q_HEADS // A_KV_HEADS
A_WINDOW = 128
B_HEADS = 6
B_WIN_H = 8
B_WIN_W = 16
C_CHANNELS = 512
C_CONV_WIDTH = 31
C_PAD = (C_CONV_WIDTH - 1) // 2
D_FF = 5632
ROPE_THETA = 10000.0
NORM_EPS = 1e-6
NEG_INF = -1e30
N_MOD = 9
IN_COLS = 4608
MIX_WIDTH = 2048

HEAD_AQ = 0
HEAD_AK = HEAD_AQ + A_Q_HEADS
HEAD_AV = HEAD_AK + A_KV_HEADS
HEAD_BQ = HEAD_AV + A_KV_HEADS
HEAD_BK = HEAD_BQ + B_HEADS
HEAD_BV = HEAD_BK + B_HEADS
N_QKV_HEADS = HEAD_BV + B_HEADS
C_COL0 = N_QKV_HEADS * HEAD_DIM

N_LAT = BATCH * SEQ
N_CTX = BATCH * CTX_LEN

V7X_VMEM_BYTES = 64 * 1024 * 1024
SUBLANES = 8
LANES = 128

TM = 512
TILES_PER_BATCH = SEQ // TM
TF = 512
TC = 256
CONV_HALO = 16

F32 = jnp.float32
BF16 = jnp.bfloat16


MIB = 1024 * 1024
VMEM_TEMPORARIES_BYTES = 12 * MIB
VMEM_UNREQUESTED_BYTES = 4 * MIB


def _vmem_limit(block_bytes):
    return int(min(V7X_VMEM_BYTES - VMEM_UNREQUESTED_BYTES, block_bytes + VMEM_TEMPORARIES_BYTES))


def _group_of_tile(is_ctx, tile_rows):
    return (lambda i: BATCH) if is_ctx else (lambda i: i * tile_rows // SEQ)


def _in_hbm(stream):
    return pltpu.with_memory_space_constraint(stream, pltpu.HBM)


def _modulated(x, gain, shift, scale):
    ms = jnp.mean(x * x, axis=-1, keepdims=True)
    return (x * lax.rsqrt(ms + NORM_EPS) * gain) * (1.0 + scale) + shift


MOD_TN = 1024
MOD_GROUPS = BATCH + 1
MOD_UNROLL = 8


def _mod_kernel(c_ref, w_ref, b_ref, o_ref, a_ref):
    @pl.when(jnp.logical_and(pl.program_id(0) == 0, pl.program_id(1) == 0))
    def _():
        c = c_ref[...]
        a_ref[...] = c * jax.nn.sigmoid(c)

    lane_tiles = MOD_TN // LANES

    def body(step, accs):
        accs = list(accs)
        for u in range(MOD_UNROLL):
            r = pl.multiple_of((step * MOD_UNROLL + u) * SUBLANES, SUBLANES)
            for m in range(MOD_GROUPS):
                a = a_ref[m, pl.ds(r, SUBLANES), :]
                for t in range(lane_tiles):
                    idx = m * lane_tiles + t
                    accs[idx] = accs[idx] + w_ref[pl.ds(r, SUBLANES), t * LANES:(t + 1) * LANES] * a
        return tuple(accs)

    zero = jnp.zeros((SUBLANES, LANES), F32)
    accs = lax.fori_loop(0, D_MODEL // (SUBLANES * MOD_UNROLL), body, (zero,) * (MOD_GROUPS * lane_tiles))
    o_ref[...] = jnp.zeros_like(o_ref)
    for m in range(MOD_GROUPS):
        row = jnp.concatenate([jnp.sum(accs[m * lane_tiles + t], axis=0, keepdims=True)
                               for t in range(lane_tiles)], axis=1)
        o_ref[m:m + 1, :] = row + b_ref[...]


def _mod_vectors(c, c_ctx, w_mod, b_mod):
    rows = jnp.concatenate([c, c_ctx[None, :]], axis=0)
    c_cols = jnp.broadcast_to(rows[:, :, None], (MOD_GROUPS, D_MODEL, LANES))
    n = N_MOD * D_MODEL
    out = pl.pallas_call(
        _mod_kernel,
        grid=(DEPTH, n // MOD_TN),
        in_specs=[
            pl.BlockSpec((MOD_GROUPS, D_MODEL, LANES), lambda l, j: (0, 0, 0)),
            pl.BlockSpec((None, D_MODEL, MOD_TN), lambda l, j: (l, 0, j)),
            pl.BlockSpec((None, 1, MOD_TN), lambda l, j: (l, 0, j)),
        ],
        out_specs=pl.BlockSpec((None, SUBLANES, MOD_TN), lambda l, j: (l, 0, j)),
        out_shape=jax.ShapeDtypeStruct((DEPTH, SUBLANES, n), F32),
        scratch_shapes=[pltpu.VMEM((MOD_GROUPS, D_MODEL, LANES), F32)],
        compiler_params=pltpu.CompilerParams(
            dimension_semantics=("arbitrary", "arbitrary"),
            vmem_limit_bytes=_vmem_limit(2 * D_MODEL * MOD_TN * 4 + 3 * MOD_GROUPS * D_MODEL * LANES * 4)),
        name="mod_vectors",
    )(c_cols, w_mod, b_mod.reshape(DEPTH, 1, n))
    return out[:, :MOD_GROUPS].reshape(DEPTH, MOD_GROUPS, 3, 3, D_MODEL)


FFN_SUB = 512
FFN_TM_LAT = 1024


def _ffn_kernel(tm, x_ref, mod_ref, gain_ref, wg_ref, wu_ref, wd_ref, o_ref, h_ref):
    j = pl.program_id(1)
    last = pl.num_programs(1) - 1

    def chunk(first, final):
        for r0 in range(0, tm, FFN_SUB):
            rows = slice(r0, r0 + FFN_SUB)
            if first:
                h = _modulated(x_ref[rows, :], gain_ref[...], mod_ref[0:1, :], mod_ref[1:2, :])
                h_ref[rows, :] = h.astype(BF16)
            h = h_ref[rows, :]
            g = jnp.dot(h, wg_ref[...], preferred_element_type=F32)
            u = jnp.dot(h, wu_ref[...], preferred_element_type=F32)
            a = (g * jax.nn.sigmoid(g)) * u
            d = jnp.dot(a.astype(BF16), wd_ref[...], preferred_element_type=F32)
            if first:
                o_ref[rows, :] = d
            elif final:
                o_ref[rows, :] = x_ref[rows, :] + (0.5 * mod_ref[2:3, :]) * (o_ref[rows, :] + d)
            else:
                o_ref[rows, :] += d

    pl.when(j == 0)(functools.partial(chunk, True, False))
    pl.when(jnp.logical_and(j > 0, j < last))(functools.partial(chunk, False, False))
    pl.when(j == last)(functools.partial(chunk, False, True))


def _ffn(x, is_ctx, mod_l, sub, gain, wg, wu, wd, layer, in_place):
    tm = TM if is_ctx else FFN_TM_LAT
    rows = x.shape[0]
    assert tm % FFN_SUB == 0 and rows % tm == 0 and SEQ % tm == 0 and D_FF // TF >= 2
    blocks = 4 * tm * D_MODEL * 4 + 6 * D_MODEL * TF * 2 + tm * D_MODEL * 2
    group = _group_of_tile(is_ctx, tm)
    return pl.pallas_call(
        functools.partial(_ffn_kernel, tm),
        grid=(rows // tm, D_FF // TF),
        in_specs=[
            pl.BlockSpec((tm, D_MODEL), lambda i, j: (i, 0)),
            pl.BlockSpec((None, None, 3, D_MODEL), lambda i, j: (group(i), sub, 0, 0)),
            pl.BlockSpec((1, D_MODEL), lambda i, j: (0, 0)),
            pl.BlockSpec((None, D_MODEL, TF), lambda i, j: (layer, 0, j)),
            pl.BlockSpec((None, D_MODEL, TF), lambda i, j: (layer, 0, j)),
            pl.BlockSpec((None, TF, D_MODEL), lambda i, j: (layer, j, 0)),
        ],
        out_specs=pl.BlockSpec((tm, D_MODEL), lambda i, j: (i, 0)),
        out_shape=jax.ShapeDtypeStruct((rows, D_MODEL), F32),
        scratch_shapes=[pltpu.VMEM((tm, D_MODEL), BF16)],
        input_output_aliases={0: 0} if in_place else {},
        compiler_params=pltpu.CompilerParams(
            dimension_semantics=("arbitrary", "arbitrary"),
            vmem_limit_bytes=_vmem_limit(blocks)),
        name="ffn",
    )(_in_hbm(x), mod_l, gain.reshape(1, D_MODEL), wg, wu, wd)


GAIN_AQ, GAIN_AK, GAIN_BQ, GAIN_BK = 0, 1, 2, 3
N_HEAD_GAINS = 4


def _head_kind(h):
    if h < HEAD_AK:
        return GAIN_AQ, True
    if h < HEAD_AV:
        return GAIN_AK, True
    if h < HEAD_BQ:
        return None, False
    if h < HEAD_BK:
        return GAIN_BQ, False
    if h < HEAD_BV:
        return GAIN_BK, False
    return None, False


def _inproj_kernel(positioned, x_ref, mod_ref, gain_ref, w_ref, hg_ref, *refs):
    if positioned:
        cos_ref, sin_ref, qkv_ref, cu_ref, h_ref = refs
    else:
        qkv_ref, cu_ref, h_ref = refs
    h = _modulated(x_ref[...], gain_ref[...], mod_ref[0:1, :], mod_ref[1:2, :])
    h_ref[...] = h.astype(BF16)
    lane = lax.broadcasted_iota(jnp.int32, (TM, HEAD_DIM), 1)
    even_quarter = ((lane // (HEAD_DIM // 4)) % 2) == 0
    for pair in range(N_QKV_HEADS // 2):
        c0 = pair * 2 * HEAD_DIM
        y2 = jnp.dot(h_ref[...], w_ref[:, c0:c0 + 2 * HEAD_DIM], preferred_element_type=F32)
        for half in range(2):
            hd = 2 * pair + half
            y = y2[:, half * HEAD_DIM:(half + 1) * HEAD_DIM]
            gain_row, rotary = _head_kind(hd)
            if gain_row is not None:
                ms = jnp.mean(y * y, axis=-1, keepdims=True)
                y = y * lax.rsqrt(ms + NORM_EPS) * hg_ref[gain_row:gain_row + 1, :]
            if rotary and positioned:
                swapped = jnp.where(even_quarter,
                                    pltpu.roll(y, HEAD_DIM - HEAD_DIM // 4, 1),
                                    pltpu.roll(y, HEAD_DIM // 4, 1))
                y = y * cos_ref[...] + swapped * sin_ref[...]
            qkv_ref[hd] = y.astype(BF16)
    for blk in range((IN_COLS - C_COL0) // (2 * HEAD_DIM)):
        c0 = C_COL0 + blk * 2 * HEAD_DIM
        cu_ref[:, blk * 2 * HEAD_DIM:(blk + 1) * 2 * HEAD_DIM] = jnp.dot(
            h_ref[...], w_ref[:, c0:c0 + 2 * HEAD_DIM], preferred_element_type=F32)


def _inproj(xs, is_ctx, mod_l, gain, w_in, layer, head_gains, rope):
    rows = xs.shape[0]
    n_cu = IN_COLS - C_COL0
    blocks = (2 * TM * D_MODEL * 4 + D_MODEL * IN_COLS * 2 + 2 * N_QKV_HEADS * TM * HEAD_DIM * 2
              + 2 * TM * n_cu * 4 + 4 * TM * HEAD_DIM * 4 + TM * D_MODEL * 2)
    group = _group_of_tile(is_ctx, TM)
    operands = [xs, mod_l, gain.reshape(1, D_MODEL), w_in, head_gains]
    in_specs = [
        pl.BlockSpec((TM, D_MODEL), lambda i: (i, 0)),
        pl.BlockSpec((None, None, 3, D_MODEL), lambda i: (group(i), 1, 0, 0)),
        pl.BlockSpec((1, D_MODEL), lambda i: (0, 0)),
        pl.BlockSpec((None, D_MODEL, IN_COLS), lambda i: (layer, 0, 0), pipeline_mode=pl.Buffered(1)),
        pl.BlockSpec((SUBLANES, HEAD_DIM), lambda i: (0, 0)),
    ]
    if not is_ctx:
        operands += list(rope)
        in_specs += [pl.BlockSpec((TM, HEAD_DIM), lambda i: (i % TILES_PER_BATCH, 0))] * 2
    return pl.pallas_call(
        functools.partial(_inproj_kernel, not is_ctx),
        grid=(rows // TM,),
        in_specs=in_specs,
        out_specs=[
            pl.BlockSpec((N_QKV_HEADS, TM, HEAD_DIM), lambda i: (0, i, 0)),
            pl.BlockSpec((TM, n_cu), lambda i: (i, 0)),
        ],
        out_shape=[
            jax.ShapeDtypeStruct((N_QKV_HEADS, rows, HEAD_DIM), BF16),
            jax.ShapeDtypeStruct((rows, n_cu), F32),
        ],
        scratch_shapes=[pltpu.VMEM((TM, D_MODEL), BF16)],
        compiler_params=pltpu.CompilerParams(
            dimension_semantics=("arbitrary",),
            vmem_limit_bytes=_vmem_limit(blocks)),
        name="inproj",
    )(*operands)


def _rope_tables():
    t = np.arange(SEQ)
    n_freq = HEAD_DIM // 4
    inv_freq = ROPE_THETA ** (-np.arange(n_freq, dtype=np.float64) / n_freq)
    ang_r = (t // GRID_W)[:, None] * inv_freq[None, :]
    ang_c = (t % GRID_W)[:, None] * inv_freq[None, :]
    cos_t = np.concatenate([np.cos(ang_r), np.cos(ang_r), np.cos(ang_c), np.cos(ang_c)], axis=-1)
    sin_t = np.concatenate([-np.sin(ang_r), np.sin(ang_r), -np.sin(ang_c), np.sin(ang_c)], axis=-1)
    return jnp.asarray(cos_t, F32), jnp.asarray(sin_t, F32)


LOG2_E = float(np.log2(np.e))
ATTN_TQ = 256
ATTN_BLOCKS = SEQ // ATTN_TQ
ATTN_UNROLL = 8
PLACE_INTERIOR, PLACE_FIRST, PLACE_LAST = 0, 1, 2
N_PLACEMENTS = 3
PLACEMENT_BLOCKS = (1, 0, ATTN_BLOCKS - 1)

A_LEAD = A_WINDOW
A_KEYS = ATTN_TQ + 2 * A_WINDOW
B_ROWS_PER_BLOCK = ATTN_TQ // GRID_W
B_UNION_ROWS = B_ROWS_PER_BLOCK + B_WIN_H
B_LEAD = (B_WIN_H // 2) * GRID_W
B_KEYS = B_UNION_ROWS * GRID_W
assert ATTN_BLOCKS % ATTN_UNROLL == 0 and ATTN_BLOCKS >= 3
assert A_LEAD % LANES == 0 and B_LEAD % LANES == 0 and A_KEYS % LANES == 0 and B_KEYS % LANES == 0


def _dot_nt(a, b):
    return lax.dot_general(a, b, (((1,), (1,)), ((), ())), preferred_element_type=F32)


def _window_start(first_query, lead, n_keys):
    return int(np.clip(first_query - lead, 0, SEQ - n_keys))


def _local_attn_kernel(lead, n_keys, has_sink, *refs):
    if has_sink:
        sink_ref, refs = refs[0], refs[1:]
    q_ref, k_ref, v_ref, kc_ref, vc_ref, bias_ref, o_ref = refs
    sink = sink_ref[pl.program_id(1) * A_GROUP + pl.program_id(2)] if has_sink else None
    kc = kc_ref[...]
    vc = vc_ref[...]

    def scores(i):
        q0 = pl.multiple_of(i * ATTN_TQ, ATTN_TQ)
        k0 = pl.multiple_of(jnp.clip(q0 - lead, 0, SEQ - n_keys), LANES)
        placement = jnp.where(i == 0, PLACE_FIRST, jnp.where(i == ATTN_BLOCKS - 1, PLACE_LAST, PLACE_INTERIOR))
        q = q_ref[pl.ds(q0, ATTN_TQ), :]
        s = _dot_nt(q, k_ref[pl.ds(k0, n_keys), :]) + bias_ref[placement]
        return q0, k0, s, _dot_nt(q, kc)

    def softmax(q0, k0, s, sc):
        m = jnp.maximum(jnp.max(s, axis=-1, keepdims=True), jnp.max(sc, axis=-1, keepdims=True))
        if has_sink:
            m = jnp.maximum(m, sink)
        p = jnp.exp2(s - m)
        pc = jnp.exp2(sc - m)
        den = jnp.sum(p, axis=-1, keepdims=True) + jnp.sum(pc, axis=-1, keepdims=True)
        if has_sink:
            den = den + jnp.exp2(sink - m)
        return q0, k0, p.astype(BF16), pc.astype(BF16), den

    def output(q0, k0, p, pc, den):
        o = (jnp.dot(p, v_ref[pl.ds(k0, n_keys), :], preferred_element_type=F32)
             + jnp.dot(pc, vc, preferred_element_type=F32))
        o_ref[pl.ds(q0, ATTN_TQ), :] = (o / den).astype(BF16)

    def body(it, carry):
        scored, weighted = {}, {}
        for step in range(ATTN_UNROLL + 2):
            if step < ATTN_UNROLL:
                scored[step] = scores(it * ATTN_UNROLL + step)
            if 0 <= step - 2 < ATTN_UNROLL:
                output(*weighted.pop(step - 2))
            if 0 <= step - 1 < ATTN_UNROLL:
                weighted[step - 1] = softmax(*scored.pop(step - 1))
        return carry

    lax.fori_loop(0, ATTN_BLOCKS // ATTN_UNROLL, body, 0)


def _local_attn_vmem(n_keys):
    return 2 * 4 * SEQ * HEAD_DIM * 2 + 4 * CTX_LEN * HEAD_DIM * 2 + 2 * N_PLACEMENTS * ATTN_TQ * n_keys * 4


SEQ_BLOCK = (None, SEQ, HEAD_DIM)
CTX_BLOCK = (None, CTX_LEN, HEAD_DIM)


def _a_band_table():
    tables = []
    for blk in PLACEMENT_BLOCKS:
        q0 = blk * ATTN_TQ
        kpos = _window_start(q0, A_LEAD, A_KEYS) + np.arange(A_KEYS)[None, :]
        qpos = q0 + np.arange(ATTN_TQ)[:, None]
        tables.append(np.where(np.abs(kpos - qpos) <= A_WINDOW, 0.0, NEG_INF))
    return jnp.asarray(np.stack(tables), F32)


def _attn_a(qkv, qkv_ctx, sink):
    return pl.pallas_call(
        functools.partial(_local_attn_kernel, A_LEAD, A_KEYS, True),
        grid=(BATCH, A_KV_HEADS, A_GROUP),
        in_specs=[
            pl.BlockSpec(memory_space=pltpu.SMEM),
            pl.BlockSpec(SEQ_BLOCK, lambda b, kv, g: (HEAD_AQ + kv * A_GROUP + g, b, 0)),
            pl.BlockSpec(SEQ_BLOCK, lambda b, kv, g: (HEAD_AK + kv, b, 0)),
            pl.BlockSpec(SEQ_BLOCK, lambda b, kv, g: (HEAD_AV + kv, b, 0)),
            pl.BlockSpec(CTX_BLOCK, lambda b, kv, g: (HEAD_AK + kv, b, 0)),
            pl.BlockSpec(CTX_BLOCK, lambda b, kv, g: (HEAD_AV + kv, b, 0)),
            pl.BlockSpec((N_PLACEMENTS, ATTN_TQ, A_KEYS), lambda b, kv, g: (0, 0, 0)),
        ],
        out_specs=pl.BlockSpec((SEQ, HEAD_DIM), lambda b, kv, g: (b, kv * A_GROUP + g)),
        out_shape=jax.ShapeDtypeStruct((N_LAT, A_Q_HEADS * HEAD_DIM), BF16),
        compiler_params=pltpu.CompilerParams(
            dimension_semantics=("arbitrary", "arbitrary", "arbitrary"),
            vmem_limit_bytes=_vmem_limit(_local_attn_vmem(A_KEYS))),
        name="attn_window",
    )(sink, qkv, qkv, qkv, qkv_ctx, qkv_ctx, _a_band_table())


def _b_bias_tables(rpb):
    j = np.arange(GRID_W)[:, None]
    jk = np.arange(GRID_W)[None, :]
    col_start = np.clip(j - B_WIN_W // 2, 0, GRID_W - B_WIN_W)
    inside = (jk >= col_start) & (jk < col_start + B_WIN_W)
    n_col = 2 * B_WIN_W - 1
    onehot = (inside[:, :, None] & ((jk - j + (B_WIN_W - 1))[:, :, None] == np.arange(n_col))).astype(np.float32)
    by_row = jnp.einsum('lhac,jkc->lhajk', rpb, jnp.asarray(onehot), precision=lax.Precision.HIGHEST)
    by_row = by_row * LOG2_E + jnp.asarray(np.where(inside, 0.0, NEG_INF), F32)
    masked = jnp.full((DEPTH, B_HEADS, GRID_W, GRID_W), NEG_INF, F32)
    query_rows = []
    for blk in PLACEMENT_BLOCKS:
        r0 = blk * B_ROWS_PER_BLOCK
        u0 = _window_start(r0 * GRID_W, B_LEAD, B_KEYS) // GRID_W
        for t in range(B_ROWS_PER_BLOCK):
            r = r0 + t
            rs = int(np.clip(r - B_WIN_H // 2, 0, GRID_ROWS - B_WIN_H))
            before = rs - u0
            assert 0 <= before <= B_UNION_ROWS - B_WIN_H
            bias_row0 = rs - r + (B_WIN_H - 1)
            pieces = [by_row[:, :, bias_row0 + u - before] if before <= u < before + B_WIN_H else masked
                      for u in range(B_UNION_ROWS)]
            query_rows.append(jnp.concatenate(pieces, axis=-1))
    return jnp.stack(query_rows, axis=2).reshape(DEPTH, B_HEADS, N_PLACEMENTS, ATTN_TQ, B_KEYS)


def _attn_b(qkv, qkv_ctx, bias, layer):
    return pl.pallas_call(
        functools.partial(_local_attn_kernel, B_LEAD, B_KEYS, False),
        grid=(BATCH, B_HEADS),
        in_specs=[
            pl.BlockSpec(SEQ_BLOCK, lambda b, h: (HEAD_BQ + h, b, 0)),
            pl.BlockSpec(SEQ_BLOCK, lambda b, h: (HEAD_BK + h, b, 0)),
            pl.BlockSpec(SEQ_BLOCK, lambda b, h: (HEAD_BV + h, b, 0)),
            pl.BlockSpec(CTX_BLOCK, lambda b, h: (HEAD_BK + h, b, 0)),
            pl.BlockSpec(CTX_BLOCK, lambda b, h: (HEAD_BV + h, b, 0)),
            pl.BlockSpec((None, None, N_PLACEMENTS, ATTN_TQ, B_KEYS), lambda b, h: (layer, h, 0, 0, 0)),
        ],
        out_specs=pl.BlockSpec((SEQ, HEAD_DIM), lambda b, h: (b, h)),
        out_shape=jax.ShapeDtypeStruct((N_LAT, B_HEADS * HEAD_DIM), BF16),
        compiler_params=pltpu.CompilerParams(
            dimension_semantics=("arbitrary", "arbitrary"),
            vmem_limit_bytes=_vmem_limit(_local_attn_vmem(B_KEYS))),
        name="attn_neighbourhood",
    )(qkv, qkv, qkv, qkv_ctx, qkv_ctx, bias)


N_CTX_HEADS = A_Q_HEADS + B_HEADS


def _attn_ctx_kernel(sink_ref, q_ref, k_ref, v_ref, o_ref):
    sink = sink_ref[pl.program_id(1)]
    q = q_ref[...]
    s = _dot_nt(q, k_ref[...])
    m = jnp.maximum(jnp.max(s, axis=-1, keepdims=True), sink)
    p = jnp.exp2(s - m)
    den = jnp.sum(p, axis=-1, keepdims=True) + jnp.exp2(sink - m)
    o = jnp.dot(p.astype(BF16), v_ref[...], preferred_element_type=F32)
    o_ref[...] = (o / den).astype(BF16)


def _attn_ctx(qkv_ctx, sinks):
    is_b = lambda h: h >= A_Q_HEADS
    q_head = lambda h: jnp.where(is_b(h), HEAD_BQ + h - A_Q_HEADS, HEAD_AQ + h)
    k_head = lambda h: jnp.where(is_b(h), HEAD_BK + h - A_Q_HEADS, HEAD_AK + h // A_GROUP)
    v_head = lambda h: jnp.where(is_b(h), HEAD_BV + h - A_Q_HEADS, HEAD_AV + h // A_GROUP)
    return pl.pallas_call(
        _attn_ctx_kernel,
        grid=(BATCH, N_CTX_HEADS),
        in_specs=[
            pl.BlockSpec(memory_space=pltpu.SMEM),
            pl.BlockSpec(CTX_BLOCK, lambda b, h: (q_head(h), b, 0)),
            pl.BlockSpec(CTX_BLOCK, lambda b, h: (k_head(h), b, 0)),
            pl.BlockSpec(CTX_BLOCK, lambda b, h: (v_head(h), b, 0)),
        ],
        out_specs=pl.BlockSpec((CTX_LEN, HEAD_DIM), lambda b, h: (b, h)),
        out_shape=jax.ShapeDtypeStruct((N_CTX, N_CTX_HEADS * HEAD_DIM), BF16),
        compiler_params=pltpu.CompilerParams(dimension_semantics=("arbitrary", "arbitrary")),
        name="attn_context",
    )(sinks, qkv_ctx, qkv_ctx, qkv_ctx)


CONV_ROWS = 32
CONV_SH_ROWS = TC + SUBLANES * ((C_CONV_WIDTH - 1) // SUBLANES)
assert SUBLANES - 1 + CONV_HALO - C_PAD + CONV_SH_ROWS <= TC + 2 * CONV_HALO
CONV_CHUNKS = TC // CONV_ROWS


def _glu(u):
    return u[:, :C_CHANNELS] * jax.nn.sigmoid(u[:, C_CHANNELS:])


def _conv_fill(n_rows, first, last, prev_ref, cur_ref, next_ref, ext_ref):
    ext_ref[0:CONV_HALO, :] = jnp.where(first, 0.0, _glu(prev_ref[...]))
    ext_ref[CONV_HALO:CONV_HALO + n_rows, :] = _glu(cur_ref[...])
    ext_ref[CONV_HALO + n_rows:, :] = jnp.where(last, 0.0, _glu(next_ref[...]))


def _conv_shift(row0, ext_ref, sh_ref):
    for b in range(SUBLANES):
        lo = row0 + b + CONV_HALO - C_PAD
        sh_ref[b] = ext_ref[lo:lo + CONV_SH_ROWS, :]


def _conv_chunk(row0, c, sh_ref, w_ref, b_ref, g_ref, beta_ref, out_ref):
    groups = CONV_ROWS // SUBLANES
    r0 = c * CONV_ROWS
    accs = [jnp.zeros((SUBLANES, C_CHANNELS), F32) for _ in range(groups)]
    for k in range(C_CONV_WIDTH):
        a, b = divmod(k, SUBLANES)
        w_k = w_ref[k]
        for g in range(groups):
            lo = r0 + SUBLANES * (a + g)
            accs[g] = accs[g] + sh_ref[b, lo:lo + SUBLANES, :] * w_k
    acc = jnp.concatenate(accs, axis=0) + b_ref[...]
    mu = jnp.mean(acc, axis=-1, keepdims=True)
    xc = acc - mu
    y = xc * lax.rsqrt(jnp.mean(xc * xc, axis=-1, keepdims=True) + NORM_EPS)
    y = y * g_ref[...] + beta_ref[...]
    out_ref[row0 + r0:row0 + r0 + CONV_ROWS, :] = (y * jax.nn.sigmoid(y)).astype(BF16)


def _conv_operands(cu, n_rows, dw_w, dw_b, ln_g, ln_b):
    halo_per_tile = n_rows // CONV_HALO
    n_halo = cu.shape[0] // CONV_HALO
    vec = pl.BlockSpec((1, C_CHANNELS), lambda i: (0, 0))
    specs = [
        pl.BlockSpec((CONV_HALO, 2 * C_CHANNELS), lambda i: (jnp.maximum(i * halo_per_tile - 1, 0), 0)),
        pl.BlockSpec((n_rows, 2 * C_CHANNELS), lambda i: (i, 0)),
        pl.BlockSpec((CONV_HALO, 2 * C_CHANNELS), lambda i: (jnp.minimum((i + 1) * halo_per_tile, n_halo - 1), 0)),
        pl.BlockSpec((C_CONV_WIDTH, SUBLANES, C_CHANNELS), lambda i: (0, 0, 0)),
        vec, vec, vec,
    ]
    operands = [cu, cu, cu, jnp.broadcast_to(dw_w[:, None, :], (C_CONV_WIDTH, SUBLANES, C_CHANNELS)),
                dw_b.reshape(1, -1), ln_g.reshape(1, -1), ln_b.reshape(1, -1)]
    return operands, specs


def _conv_scratch(n_rows):
    return [pltpu.VMEM((n_rows + 2 * CONV_HALO, C_CHANNELS), F32),
            pltpu.VMEM((SUBLANES, CONV_SH_ROWS, C_CHANNELS), F32)]


def _conv_ctx_kernel(prev_ref, cur_ref, next_ref, w_ref, b_ref, g_ref, beta_ref, o_ref, ext_ref, sh_ref):
    _conv_fill(TC, True, True, prev_ref, cur_ref, next_ref, ext_ref)
    _conv_shift(0, ext_ref, sh_ref)
    for c in range(CONV_CHUNKS):
        _conv_chunk(0, c, sh_ref, w_ref, b_ref, g_ref, beta_ref, o_ref)


def _conv_ctx(cu_ctx, dw_w, dw_b, ln_g, ln_b):
    assert CTX_LEN == TC
    operands, specs = _conv_operands(cu_ctx, TC, dw_w, dw_b, ln_g, ln_b)
    return pl.pallas_call(
        _conv_ctx_kernel,
        grid=(N_CTX // TC,),
        in_specs=specs,
        out_specs=pl.BlockSpec((TC, C_CHANNELS), lambda i: (i, 0)),
        out_shape=jax.ShapeDtypeStruct((N_CTX, C_CHANNELS), BF16),
        scratch_shapes=_conv_scratch(TC),
        compiler_params=pltpu.CompilerParams(dimension_semantics=("arbitrary",)),
        name="conv_context",
    )(*operands)


N_CONV_OPERANDS = 7


def _outproj_kernel(widths, fused_conv, x_ref, mod_ref, w_ref, *refs):
    if fused_conv:
        pieces = list(refs[:len(widths) - 1])
        conv_refs = refs[len(widths) - 1:len(widths) - 1 + N_CONV_OPERANDS]
        o_ref, ext_ref, sh_ref, oc_ref = refs[len(widths) - 1 + N_CONV_OPERANDS:]
    else:
        pieces, o_ref = list(refs[:-1]), refs[-1]
    y = None
    k0 = 0
    for piece, width in zip(pieces, widths):
        t = jnp.dot(piece[...], w_ref[k0:k0 + width, :], preferred_element_type=F32)
        y = t if y is None else y + t
        k0 += width
    if fused_conv:
        prev_ref, cur_ref, next_ref, cw_ref, cb_ref, cg_ref, cbeta_ref = conv_refs
        i = pl.program_id(0)
        _conv_fill(TM, i % TILES_PER_BATCH == 0, i % TILES_PER_BATCH == TILES_PER_BATCH - 1,
                   prev_ref, cur_ref, next_ref, ext_ref)
        for row0 in range(0, TM, TC):
            _conv_shift(row0, ext_ref, sh_ref)
            for c in range(CONV_CHUNKS):
                _conv_chunk(row0, c, sh_ref, cw_ref, cb_ref, cg_ref, cbeta_ref, oc_ref)
        y = y + jnp.dot(oc_ref[...], w_ref[k0:, :], preferred_element_type=F32)
    o_ref[...] = x_ref[...] + mod_ref[2:3, :] * y


def _outproj(xs, is_ctx, mod_l, w_out, layer, pieces, conv=None):
    rows = xs.shape[0]
    widths = tuple(int(p.shape[1]) for p in pieces) + ((C_CHANNELS,) if conv is not None else ())
    assert sum(widths) == MIX_WIDTH and all(p.shape[0] == rows for p in pieces)
    blocks = 4 * TM * D_MODEL * 4 + MIX_WIDTH * D_MODEL * 2 + 2 * TM * MIX_WIDTH * 2
    group = _group_of_tile(is_ctx, TM)
    operands = [_in_hbm(xs), mod_l, w_out] + list(pieces)
    in_specs = [
        pl.BlockSpec((TM, D_MODEL), lambda i: (i, 0)),
        pl.BlockSpec((None, None, 3, D_MODEL), lambda i: (group(i), 1, 0, 0)),
        pl.BlockSpec((None, MIX_WIDTH, D_MODEL), lambda i: (layer, 0, 0), pipeline_mode=pl.Buffered(1)),
    ] + [pl.BlockSpec((TM, w), lambda i: (i, 0)) for w in widths[:len(pieces)]]
    scratch = []
    if conv is not None:
        assert not is_ctx and SEQ % TM == 0 and TM % TC == 0 and conv[0].shape[0] == rows
        conv_operands, conv_specs = _conv_operands(conv[0], TM, *conv[1:])
        operands += conv_operands
        in_specs += conv_specs
        scratch = _conv_scratch(TM) + [pltpu.VMEM((TM, C_CHANNELS), BF16)]
        blocks += 4 * TM * C_CHANNELS * 4 + (TM + SUBLANES * CONV_SH_ROWS) * C_CHANNELS * 4
    return pl.pallas_call(
        functools.partial(_outproj_kernel, widths, conv is not None),
        grid=(rows // TM,),
        in_specs=in_specs,
        out_specs=pl.BlockSpec((TM, D_MODEL), lambda i: (i, 0)),
        out_shape=jax.ShapeDtypeStruct((rows, D_MODEL), F32),
        scratch_shapes=scratch,
        input_output_aliases={0: 0},
        compiler_params=pltpu.CompilerParams(
            dimension_semantics=("arbitrary",),
            vmem_limit_bytes=_vmem_limit(blocks)),
        name="outproj",
    )(*operands)


def kernel(x, c, ctx, c_ctx, w_mod, b_mod, norm_ffn1, norm_mix, norm_ffn2, ffn1_w_gate, ffn1_w_up,
           ffn1_w_down, ffn2_w_gate, ffn2_w_up, ffn2_w_down, w_in, w_out, a_q_norm, a_k_norm, a_sink,
           b_q_norm, b_k_norm, b_rpb, c_dw_w, c_dw_b, c_ln_g, c_ln_b):
    mods = _mod_vectors(c, c_ctx, w_mod, b_mod)
    rope = _rope_tables()
    b_bias = _b_bias_tables(b_rpb)
    ffn1_w = [w.astype(BF16) for w in (ffn1_w_gate, ffn1_w_up, ffn1_w_down)]
    ffn2_w = [w.astype(BF16) for w in (ffn2_w_gate, ffn2_w_up, ffn2_w_down)]
    w_in_bf = w_in.astype(BF16)
    w_out_bf = w_out.astype(BF16)
    q_scale = HEAD_DIM ** -0.5 * LOG2_E
    xs = x.reshape(N_LAT, D_MODEL)
    cs = ctx.reshape(N_CTX, D_MODEL)
    for l in range(DEPTH):
        last = l == DEPTH - 1
        mod_l = mods[l]
        owned = l > 0
        xs = _ffn(xs, False, mod_l, 0, norm_ffn1[l], *ffn1_w, l, in_place=owned)
        cs = _ffn(cs, True, mod_l, 0, norm_ffn1[l], *ffn1_w, l, in_place=owned)
        head_gains = jnp.concatenate([
            (a_q_norm[l] * q_scale)[None], a_k_norm[l][None], (b_q_norm[l] * q_scale)[None],
            b_k_norm[l][None], jnp.zeros((SUBLANES - N_HEAD_GAINS, HEAD_DIM), F32)], axis=0)
        qkv, cu = _inproj(xs, False, mod_l, norm_mix[l], w_in_bf, l, head_gains, rope)
        qkv_ctx, cu_ctx = _inproj(cs, True, mod_l, norm_mix[l], w_in_bf, l, head_gains, rope)
        sink_l = a_sink[l] * LOG2_E
        o_a = _attn_a(qkv, qkv_ctx, sink_l)
        o_b = _attn_b(qkv, qkv_ctx, b_bias, l)
        conv_params = (c_dw_w[l], c_dw_b[l], c_ln_g[l], c_ln_b[l])
        xs = _outproj(xs, False, mod_l, w_out_bf, l, [o_a, o_b], conv=(cu,) + conv_params)
        xs = _ffn(xs, False, mod_l, 2, norm_ffn2[l], *ffn2_w, l, in_place=True)
        if not last:
            sinks = jnp.concatenate([sink_l, jnp.full((B_HEADS,), NEG_INF, F32)])
            o_ctx = _attn_ctx(qkv_ctx, sinks)
            o_c_ctx = _conv_ctx(cu_ctx, *conv_params)
            cs = _outproj(cs, True, mod_l, w_out_bf, l, [o_ctx, o_c_ctx])
            cs = _ffn(cs, True, mod_l, 2, norm_ffn2[l], *ffn2_w, l, in_place=True)
    return xs.reshape(BATCH, SEQ, D_MODEL)
```

```python
import functools

import numpy as np
import jax
import jax.numpy as jnp
from jax import lax
from jax.experimental import pallas as pl
from jax.experimental.pallas import tpu as pltpu

D_MODEL = 2048
BATCH = 2
SEQ = 16384
DEPTH = 2
GRID_W = 64
GRID_ROWS = SEQ // GRID_W
CTX_LEN = 256
HEAD_DIM = 128
A_Q_HEADS = 6
A_KV_HEADS = 2
A_GROUP = A_Q_HEADS // A_KV_HEADS
A_WINDOW = 128
B_HEADS = 6
B_WIN_H = 8
B_WIN_W = 16
C_CHANNELS = 512
C_CONV_WIDTH = 31
C_PAD = (C_CONV_WIDTH - 1) // 2
D_FF = 5632
ROPE_THETA = 10000.0
NORM_EPS = 1e-6
NEG_INF = -1e30
N_MOD = 9
IN_COLS = 4608
MIX_WIDTH = 2048

HEAD_AQ = 0
HEAD_AK = HEAD_AQ + A_Q_HEADS
HEAD_AV = HEAD_AK + A_KV_HEADS
HEAD_BQ = HEAD_AV + A_KV_HEADS
HEAD_BK = HEAD_BQ + B_HEADS
HEAD_BV = HEAD_BK + B_HEADS
N_QKV_HEADS = HEAD_BV + B_HEADS
C_COL0 = N_QKV_HEADS * HEAD_DIM

N_LAT = BATCH * SEQ
N_CTX = BATCH * CTX_LEN

V7X_VMEM_BYTES = 64 * 1024 * 1024
SUBLANES = 8
LANES = 128

TM = 512
TILES_PER_BATCH = SEQ // TM
TF = 512
TC = 256
CONV_HALO = 16

F32 = jnp.float32
BF16 = jnp.bfloat16


MIB = 1024 * 1024
VMEM_TEMPORARIES_BYTES = 12 * MIB
VMEM_UNREQUESTED_BYTES = 4 * MIB


def _vmem_limit(block_bytes):
    return int(min(V7X_VMEM_BYTES - VMEM_UNREQUESTED_BYTES, block_bytes + VMEM_TEMPORARIES_BYTES))


def _group_of_tile(is_ctx, tile_rows):
    return (lambda i: BATCH) if is_ctx else (lambda i: i * tile_rows // SEQ)


def _in_hbm(stream):
    return pltpu.with_memory_space_constraint(stream, pltpu.HBM)


def _modulated(x, gain, shift, scale):
    ms = jnp.mean(x * x, axis=-1, keepdims=True)
    return (x * lax.rsqrt(ms + NORM_EPS) * gain) * (1.0 + scale) + shift


MOD_TN = 1024
MOD_GROUPS = BATCH + 1
MOD_UNROLL = 8


def _mod_kernel(c_ref, w_ref, b_ref, o_ref, a_ref):
    @pl.when(jnp.logical_and(pl.program_id(0) == 0, pl.program_id(1) == 0))
    def _():
        c = c_ref[...]
        a_ref[...] = c * jax.nn.sigmoid(c)

    lane_tiles = MOD_TN // LANES

    def body(step, accs):
        accs = list(accs)
        for u in range(MOD_UNROLL):
            r = pl.multiple_of((step * MOD_UNROLL + u) * SUBLANES, SUBLANES)
            for m in range(MOD_GROUPS):
                a = a_ref[m, pl.ds(r, SUBLANES), :]
                for t in range(lane_tiles):
                    idx = m * lane_tiles + t
                    accs[idx] = accs[idx] + w_ref[pl.ds(r, SUBLANES), t * LANES:(t + 1) * LANES] * a
        return tuple(accs)

    zero = jnp.zeros((SUBLANES, LANES), F32)
    accs = lax.fori_loop(0, D_MODEL // (SUBLANES * MOD_UNROLL), body, (zero,) * (MOD_GROUPS * lane_tiles))
    o_ref[...] = jnp.zeros_like(o_ref)
    for m in range(MOD_GROUPS):
        row = jnp.concatenate([jnp.sum(accs[m * lane_tiles + t], axis=0, keepdims=True)
                               for t in range(lane_tiles)], axis=1)
        o_ref[m:m + 1, :] = row + b_ref[...]


def _mod_vectors(c, c_ctx, w_mod, b_mod):
    rows = jnp.concatenate([c, c_ctx[None, :]], axis=0)
    c_cols = jnp.broadcast_to(rows[:, :, None], (MOD_GROUPS, D_MODEL, LANES))
    n = N_MOD * D_MODEL
    out = pl.pallas_call(
        _mod_kernel,
        grid=(DEPTH, n // MOD_TN),
        in_specs=[
            pl.BlockSpec((MOD_GROUPS, D_MODEL, LANES), lambda l, j: (0, 0, 0)),
            pl.BlockSpec((None, D_MODEL, MOD_TN), lambda l, j: (l, 0, j)),
            pl.BlockSpec((None, 1, MOD_TN), lambda l, j: (l, 0, j)),
        ],
        out_specs=pl.BlockSpec((None, SUBLANES, MOD_TN), lambda l, j: (l, 0, j)),
        out_shape=jax.ShapeDtypeStruct((DEPTH, SUBLANES, n), F32),
        scratch_shapes=[pltpu.VMEM((MOD_GROUPS, D_MODEL, LANES), F32)],
        compiler_params=pltpu.CompilerParams(
            dimension_semantics=("arbitrary", "arbitrary"),
            vmem_limit_bytes=_vmem_limit(2 * D_MODEL * MOD_TN * 4 + 3 * MOD_GROUPS * D_MODEL * LANES * 4)),
        name="mod_vectors",
    )(c_cols, w_mod, b_mod.reshape(DEPTH, 1, n))
    return out[:, :MOD_GROUPS].reshape(DEPTH, MOD_GROUPS, 3, 3, D_MODEL)


FFN_SUB = 512
FFN_TM_LAT = 1024


def _ffn_kernel(tm, x_ref, mod_ref, gain_ref, wg_ref, wu_ref, wd_ref, o_ref, h_ref):
    j = pl.program_id(1)
    last = pl.num_programs(1) - 1

    def chunk(first, final):
        for r0 in range(0, tm, FFN_SUB):
            rows = slice(r0, r0 + FFN_SUB)
            if first:
                h = _modulated(x_ref[rows, :], gain_ref[...], mod_ref[0:1, :], mod_ref[1:2, :])
                h_ref[rows, :] = h.astype(BF16)
            h = h_ref[rows, :]
            g = jnp.dot(h, wg_ref[...], preferred_element_type=F32)
            u = jnp.dot(h, wu_ref[...], preferred_element_type=F32)
            a = (g * jax.nn.sigmoid(g)) * u
            d = jnp.dot(a.astype(BF16), wd_ref[...], preferred_element_type=F32)
            if first:
                o_ref[rows, :] = d
            elif final:
                o_ref[rows, :] = x_ref[rows, :] + (0.5 * mod_ref[2:3, :]) * (o_ref[rows, :] + d)
            else:
                o_ref[rows, :] += d

    pl.when(j == 0)(functools.partial(chunk, True, False))
    pl.when(jnp.logical_and(j > 0, j < last))(functools.partial(chunk, False, False))
    pl.when(j == last)(functools.partial(chunk, False, True))


def _ffn(x, is_ctx, mod_l, sub, gain, wg, wu, wd, layer, in_place):
    tm = TM if is_ctx else FFN_TM_LAT
    rows = x.shape[0]
    assert tm % FFN_SUB == 0 and rows % tm == 0 and SEQ % tm == 0 and D_FF // TF >= 2
    blocks = 4 * tm * D_MODEL * 4 + 6 * D_MODEL * TF * 2 + tm * D_MODEL * 2
    group = _group_of_tile(is_ctx, tm)
    return pl.pallas_call(
        functools.partial(_ffn_kernel, tm),
        grid=(rows // tm, D_FF // TF),
        in_specs=[
            pl.BlockSpec((tm, D_MODEL), lambda i, j: (i, 0)),
            pl.BlockSpec((None, None, 3, D_MODEL), lambda i, j: (group(i), sub, 0, 0)),
            pl.BlockSpec((1, D_MODEL), lambda i, j: (0, 0)),
            pl.BlockSpec((None, D_MODEL, TF), lambda i, j: (layer, 0, j)),
            pl.BlockSpec((None, D_MODEL, TF), lambda i, j: (layer, 0, j)),
            pl.BlockSpec((None, TF, D_MODEL), lambda i, j: (layer, j, 0)),
        ],
        out_specs=pl.BlockSpec((tm, D_MODEL), lambda i, j: (i, 0)),
        out_shape=jax.ShapeDtypeStruct((rows, D_MODEL), F32),
        scratch_shapes=[pltpu.VMEM((tm, D_MODEL), BF16)],
        input_output_aliases={0: 0} if in_place else {},
        compiler_params=pltpu.CompilerParams(
            dimension_semantics=("arbitrary", "arbitrary"),
            vmem_limit_bytes=_vmem_limit(blocks)),
        name="ffn",
    )(_in_hbm(x), mod_l, gain.reshape(1, D_MODEL), wg, wu, wd)


GAIN_AQ, GAIN_AK, GAIN_BQ, GAIN_BK = 0, 1, 2, 3
N_HEAD_GAINS = 4


def _head_kind(h):
    if h < HEAD_AK:
        return GAIN_AQ, True
    if h < HEAD_AV:
        return GAIN_AK, True
    if h < HEAD_BQ:
        return None, False
    if h < HEAD_BK:
        return GAIN_BQ, False
    if h < HEAD_BV:
        return GAIN_BK, False
    return None, False


def _inproj_kernel(positioned, x_ref, mod_ref, gain_ref, w_ref, hg_ref, *refs):
    if positioned:
        cos_ref, sin_ref, qkv_ref, cu_ref, h_ref = refs
    else:
        qkv_ref, cu_ref, h_ref = refs
    h = _modulated(x_ref[...], gain_ref[...], mod_ref[0:1, :], mod_ref[1:2, :])
    h_ref[...] = h.astype(BF16)
    lane = lax.broadcasted_iota(jnp.int32, (TM, HEAD_DIM), 1)
    even_quarter = ((lane // (HEAD_DIM // 4)) % 2) == 0
    for pair in range(N_QKV_HEADS // 2):
        c0 = pair * 2 * HEAD_DIM
        y2 = jnp.dot(h_ref[...], w_ref[:, c0:c0 + 2 * HEAD_DIM], preferred_element_type=F32)
        for half in range(2):
            hd = 2 * pair + half
            y = y2[:, half * HEAD_DIM:(half + 1) * HEAD_DIM]
            gain_row, rotary = _head_kind(hd)
            if gain_row is not None:
                ms = jnp.mean(y * y, axis=-1, keepdims=True)
                y = y * lax.rsqrt(ms + NORM_EPS) * hg_ref[gain_row:gain_row + 1, :]
            if rotary and positioned:
                swapped = jnp.where(even_quarter,
                                    pltpu.roll(y, HEAD_DIM - HEAD_DIM // 4, 1),
                                    pltpu.roll(y, HEAD_DIM // 4, 1))
                y = y * cos_ref[...] + swapped * sin_ref[...]
            qkv_ref[hd] = y.astype(BF16)
    for blk in range((IN_COLS - C_COL0) // (2 * HEAD_DIM)):
        c0 = C_COL0 + blk * 2 * HEAD_DIM
        cu_ref[:, blk * 2 * HEAD_DIM:(blk + 1) * 2 * HEAD_DIM] = jnp.dot(
            h_ref[...], w_ref[:, c0:c0 + 2 * HEAD_DIM], preferred_element_type=F32)


def _inproj(xs, is_ctx, mod_l, gain, w_in, layer, head_gains, rope):
    rows = xs.shape[0]
    n_cu = IN_COLS - C_COL0
    blocks = (2 * TM * D_MODEL * 4 + D_MODEL * IN_COLS * 2 + 2 * N_QKV_HEADS * TM * HEAD_DIM * 2
              + 2 * TM * n_cu * 4 + 4 * TM * HEAD_DIM * 4 + TM * D_MODEL * 2)
    group = _group_of_tile(is_ctx, TM)
    operands = [xs, mod_l, gain.reshape(1, D_MODEL), w_in, head_gains]
    in_specs = [
        pl.BlockSpec((TM, D_MODEL), lambda i: (i, 0)),
        pl.BlockSpec((None, None, 3, D_MODEL), lambda i: (group(i), 1, 0, 0)),
        pl.BlockSpec((1, D_MODEL), lambda i: (0, 0)),
        pl.BlockSpec((None, D_MODEL, IN_COLS), lambda i: (layer, 0, 0), pipeline_mode=pl.Buffered(1)),
        pl.BlockSpec((SUBLANES, HEAD_DIM), lambda i: (0, 0)),
    ]
    if not is_ctx:
        operands += list(rope)
        in_specs += [pl.BlockSpec((TM, HEAD_DIM), lambda i: (i % TILES_PER_BATCH, 0))] * 2
    return pl.pallas_call(
        functools.partial(_inproj_kernel, not is_ctx),
        grid=(rows // TM,),
        in_specs=in_specs,
        out_specs=[
            pl.BlockSpec((N_QKV_HEADS, TM, HEAD_DIM), lambda i: (0, i, 0)),
            pl.BlockSpec((TM, n_cu), lambda i: (i, 0)),
        ],
        out_shape=[
            jax.ShapeDtypeStruct((N_QKV_HEADS, rows, HEAD_DIM), BF16),
            jax.ShapeDtypeStruct((rows, n_cu), F32),
        ],
        scratch_shapes=[pltpu.VMEM((TM, D_MODEL), BF16)],
        compiler_params=pltpu.CompilerParams(
            dimension_semantics=("arbitrary",),
            vmem_limit_bytes=_vmem_limit(blocks)),
        name="inproj",
    )(*operands)


def _rope_tables():
    t = np.arange(SEQ)
    n_freq = HEAD_DIM // 4
    inv_freq = ROPE_THETA ** (-np.arange(n_freq, dtype=np.float64) / n_freq)
    ang_r = (t // GRID_W)[:, None] * inv_freq[None, :]
    ang_c = (t % GRID_W)[:, None] * inv_freq[None, :]
    cos_t = np.concatenate([np.cos(ang_r), np.cos(ang_r), np.cos(ang_c), np.cos(ang_c)], axis=-1)
    sin_t = np.concatenate([-np.sin(ang_r), np.sin(ang_r), -np.sin(ang_c), np.sin(ang_c)], axis=-1)
    return jnp.asarray(cos_t, F32), jnp.asarray(sin_t, F32)


LOG2_E = float(np.log2(np.e))
ATTN_TQ = 256
ATTN_BLOCKS = SEQ // ATTN_TQ
ATTN_UNROLL = 16
PLACE_INTERIOR, PLACE_FIRST, PLACE_LAST = 0, 1, 2
N_PLACEMENTS = 3
PLACEMENT_BLOCKS = (1, 0, ATTN_BLOCKS - 1)

A_LEAD = A_WINDOW
A_KEYS = ATTN_TQ + 2 * A_WINDOW
B_ROWS_PER_BLOCK = ATTN_TQ // GRID_W
B_UNION_ROWS = B_ROWS_PER_BLOCK + B_WIN_H
B_LEAD = (B_WIN_H // 2) * GRID_W
B_KEYS = B_UNION_ROWS * GRID_W
assert ATTN_BLOCKS % ATTN_UNROLL == 0 and ATTN_BLOCKS >= 3
assert A_LEAD % LANES == 0 and B_LEAD % LANES == 0 and A_KEYS % LANES == 0 and B_KEYS % LANES == 0


def _dot_nt(a, b):
    return lax.dot_general(a, b, (((1,), (1,)), ((), ())), preferred_element_type=F32)


def _window_start(first_query, lead, n_keys):
    return int(np.clip(first_query - lead, 0, SEQ - n_keys))


def _local_attn_kernel(lead, n_keys, has_sink, *refs):
    if has_sink:
        sink_ref, refs = refs[0], refs[1:]
    q_ref, k_ref, v_ref, kc_ref, vc_ref, bias_ref, o_ref = refs
    sink = sink_ref[pl.program_id(1) * A_GROUP + pl.program_id(2)] if has_sink else None
    kc = kc_ref[...]
    vc = vc_ref[...]

    def scores(i):
        q0 = pl.multiple_of(i * ATTN_TQ, ATTN_TQ)
        k0 = pl.multiple_of(jnp.clip(q0 - lead, 0, SEQ - n_keys), LANES)
        placement = jnp.where(i == 0, PLACE_FIRST, jnp.where(i == ATTN_BLOCKS - 1, PLACE_LAST, PLACE_INTERIOR))
        q = q_ref[pl.ds(q0, ATTN_TQ), :]
        s = _dot_nt(q, k_ref[pl.ds(k0, n_keys), :]) + bias_ref[placement]
        return q0, k0, s, _dot_nt(q, kc)

    def softmax(q0, k0, s, sc):
        m = jnp.maximum(jnp.max(s, axis=-1, keepdims=True), jnp.max(sc, axis=-1, keepdims=True))
        if has_sink:
            m = jnp.maximum(m, sink)
        p = jnp.exp2(s - m)
        pc = jnp.exp2(sc - m)
        den = jnp.sum(p, axis=-1, keepdims=True) + jnp.sum(pc, axis=-1, keepdims=True)
        if has_sink:
            den = den + jnp.exp2(sink - m)
        return q0, k0, p.astype(BF16), pc.astype(BF16), den

    def output(q0, k0, p, pc, den):
        o = (jnp.dot(p, v_ref[pl.ds(k0, n_keys), :], preferred_element_type=F32)
             + jnp.dot(pc, vc, preferred_element_type=F32))
        o_ref[pl.ds(q0, ATTN_TQ), :] = (o / den).astype(BF16)

    def body(it, carry):
        scored, weighted = {}, {}
        for step in range(ATTN_UNROLL + 2):
            if step < ATTN_UNROLL:
                scored[step] = scores(it * ATTN_UNROLL + step)
            if 0 <= step - 2 < ATTN_UNROLL:
                output(*weighted.pop(step - 2))
            if 0 <= step - 1 < ATTN_UNROLL:
                weighted[step - 1] = softmax(*scored.pop(step - 1))
        return carry

    lax.fori_loop(0, ATTN_BLOCKS // ATTN_UNROLL, body, 0)


def _local_attn_vmem(n_keys):
    return 2 * 4 * SEQ * HEAD_DIM * 2 + 4 * CTX_LEN * HEAD_DIM * 2 + 2 * N_PLACEMENTS * ATTN_TQ * n_keys * 4


SEQ_BLOCK = (None, SEQ, HEAD_DIM)
CTX_BLOCK = (None, CTX_LEN, HEAD_DIM)


def _a_band_table():
    tables = []
    for blk in PLACEMENT_BLOCKS:
        q0 = blk * ATTN_TQ
        kpos = _window_start(q0, A_LEAD, A_KEYS) + np.arange(A_KEYS)[None, :]
        qpos = q0 + np.arange(ATTN_TQ)[:, None]
        tables.append(np.where(np.abs(kpos - qpos) <= A_WINDOW, 0.0, NEG_INF))
    return jnp.asarray(np.stack(tables), F32)


def _attn_a(qkv, qkv_ctx, sink):
    return pl.pallas_call(
        functools.partial(_local_attn_kernel, A_LEAD, A_KEYS, True),
        grid=(BATCH, A_KV_HEADS, A_GROUP),
        in_specs=[
            pl.BlockSpec(memory_space=pltpu.SMEM),
            pl.BlockSpec(SEQ_BLOCK, lambda b, kv, g: (HEAD_AQ + kv * A_GROUP + g, b, 0)),
            pl.BlockSpec(SEQ_BLOCK, lambda b, kv, g: (HEAD_AK + kv, b, 0)),
            pl.BlockSpec(SEQ_BLOCK, lambda b, kv, g: (HEAD_AV + kv, b, 0)),
            pl.BlockSpec(CTX_BLOCK, lambda b, kv, g: (HEAD_AK + kv, b, 0)),
            pl.BlockSpec(CTX_BLOCK, lambda b, kv, g: (HEAD_AV + kv, b, 0)),
            pl.BlockSpec((N_PLACEMENTS, ATTN_TQ, A_KEYS), lambda b, kv, g: (0, 0, 0)),
        ],
        out_specs=pl.BlockSpec((SEQ, HEAD_DIM), lambda b, kv, g: (b, kv * A_GROUP + g)),
        out_shape=jax.ShapeDtypeStruct((N_LAT, A_Q_HEADS * HEAD_DIM), BF16),
        compiler_params=pltpu.CompilerParams(
            dimension_semantics=("arbitrary", "arbitrary", "arbitrary"),
            vmem_limit_bytes=_vmem_limit(_local_attn_vmem(A_KEYS))),
        name="attn_window",
    )(sink, qkv, qkv, qkv, qkv_ctx, qkv_ctx, _a_band_table())


def _b_bias_tables(rpb):
    j = np.arange(GRID_W)[:, None]
    jk = np.arange(GRID_W)[None, :]
    col_start = np.clip(j - B_WIN_W // 2, 0, GRID_W - B_WIN_W)
    inside = (jk >= col_start) & (jk < col_start + B_WIN_W)
    n_col = 2 * B_WIN_W - 1
    onehot = (inside[:, :, None] & ((jk - j + (B_WIN_W - 1))[:, :, None] == np.arange(n_col))).astype(np.float32)
    by_row = jnp.einsum('lhac,jkc->lhajk', rpb, jnp.asarray(onehot), precision=lax.Precision.HIGHEST)
    by_row = by_row * LOG2_E + jnp.asarray(np.where(inside, 0.0, NEG_INF), F32)
    masked = jnp.full((DEPTH, B_HEADS, GRID_W, GRID_W), NEG_INF, F32)
    query_rows = []
    for blk in PLACEMENT_BLOCKS:
        r0 = blk * B_ROWS_PER_BLOCK
        u0 = _window_start(r0 * GRID_W, B_LEAD, B_KEYS) // GRID_W
        for t in range(B_ROWS_PER_BLOCK):
            r = r0 + t
            rs = int(np.clip(r - B_WIN_H // 2, 0, GRID_ROWS - B_WIN_H))
            before = rs - u0
            assert 0 <= before <= B_UNION_ROWS - B_WIN_H
            bias_row0 = rs - r + (B_WIN_H - 1)
            pieces = [by_row[:, :, bias_row0 + u - before] if before <= u < before + B_WIN_H else masked
                      for u in range(B_UNION_ROWS)]
            query_rows.append(jnp.concatenate(pieces, axis=-1))
    return jnp.stack(query_rows, axis=2).reshape(DEPTH, B_HEADS, N_PLACEMENTS, ATTN_TQ, B_KEYS)


def _attn_b(qkv, qkv_ctx, bias, layer):
    return pl.pallas_call(
        functools.partial(_local_attn_kernel, B_LEAD, B_KEYS, False),
        grid=(BATCH, B_HEADS),
        in_specs=[
            pl.BlockSpec(SEQ_BLOCK, lambda b, h: (HEAD_BQ + h, b, 0)),
            pl.BlockSpec(SEQ_BLOCK, lambda b, h: (HEAD_BK + h, b, 0)),
            pl.BlockSpec(SEQ_BLOCK, lambda b, h: (HEAD_BV + h, b, 0)),
            pl.BlockSpec(CTX_BLOCK, lambda b, h: (HEAD_BK + h, b, 0)),
            pl.BlockSpec(CTX_BLOCK, lambda b, h: (HEAD_BV + h, b, 0)),
            pl.BlockSpec((None, None, N_PLACEMENTS, ATTN_TQ, B_KEYS), lambda b, h: (layer, h, 0, 0, 0)),
        ],
        out_specs=pl.BlockSpec((SEQ, HEAD_DIM), lambda b, h: (b, h)),
        out_shape=jax.ShapeDtypeStruct((N_LAT, B_HEADS * HEAD_DIM), BF16),
        compiler_params=pltpu.CompilerParams(
            dimension_semantics=("arbitrary", "arbitrary"),
            vmem_limit_bytes=_vmem_limit(_local_attn_vmem(B_KEYS))),
        name="attn_neighbourhood",
    )(qkv, qkv, qkv, qkv_ctx, qkv_ctx, bias)


N_CTX_HEADS = A_Q_HEADS + B_HEADS


def _attn_ctx_kernel(sink_ref, q_ref, k_ref, v_ref, o_ref):
    sink = sink_ref[pl.program_id(1)]
    q = q_ref[...]
    s = _dot_nt(q, k_ref[...])
    m = jnp.maximum(jnp.max(s, axis=-1, keepdims=True), sink)
    p = jnp.exp2(s - m)
    den = jnp.sum(p, axis=-1, keepdims=True) + jnp.exp2(sink - m)
    o = jnp.dot(p.astype(BF16), v_ref[...], preferred_element_type=F32)
    o_ref[...] = (o / den).astype(BF16)


def _attn_ctx(qkv_ctx, sinks):
    is_b = lambda h: h >= A_Q_HEADS
    q_head = lambda h: jnp.where(is_b(h), HEAD_BQ + h - A_Q_HEADS, HEAD_AQ + h)
    k_head = lambda h: jnp.where(is_b(h), HEAD_BK + h - A_Q_HEADS, HEAD_AK + h // A_GROUP)
    v_head = lambda h: jnp.where(is_b(h), HEAD_BV + h - A_Q_HEADS, HEAD_AV + h // A_GROUP)
    return pl.pallas_call(
        _attn_ctx_kernel,
        grid=(BATCH, N_CTX_HEADS),
        in_specs=[
            pl.BlockSpec(memory_space=pltpu.SMEM),
            pl.BlockSpec(CTX_BLOCK, lambda b, h: (q_head(h), b, 0)),
            pl.BlockSpec(CTX_BLOCK, lambda b, h: (k_head(h), b, 0)),
            pl.BlockSpec(CTX_BLOCK, lambda b, h: (v_head(h), b, 0)),
        ],
        out_specs=pl.BlockSpec((CTX_LEN, HEAD_DIM), lambda b, h: (b, h)),
        out_shape=jax.ShapeDtypeStruct((N_CTX, N_CTX_HEADS * HEAD_DIM), BF16),
        compiler_params=pltpu.CompilerParams(dimension_semantics=("arbitrary", "arbitrary")),
        name="attn_context",
    )(sinks, qkv_ctx, qkv_ctx, qkv_ctx)


CONV_ROWS = 32
CONV_SH_ROWS = TC + SUBLANES * ((C_CONV_WIDTH - 1) // SUBLANES)
assert SUBLANES - 1 + CONV_HALO - C_PAD + CONV_SH_ROWS <= TC + 2 * CONV_HALO
CONV_CHUNKS = TC // CONV_ROWS


def _glu(u):
    return u[:, :C_CHANNELS] * jax.nn.sigmoid(u[:, C_CHANNELS:])


def _conv_fill(n_rows, first, last, prev_ref, cur_ref, next_ref, ext_ref):
    ext_ref[0:CONV_HALO, :] = jnp.where(first, 0.0, _glu(prev_ref[...]))
    ext_ref[CONV_HALO:CONV_HALO + n_rows, :] = _glu(cur_ref[...])
    ext_ref[CONV_HALO + n_rows:, :] = jnp.where(last, 0.0, _glu(next_ref[...]))


def _conv_shift(row0, ext_ref, sh_ref):
    for b in range(SUBLANES):
        lo = row0 + b + CONV_HALO - C_PAD
        sh_ref[b] = ext_ref[lo:lo + CONV_SH_ROWS, :]


def _conv_chunk(row0, c, sh_ref, w_ref, b_ref, g_ref, beta_ref, out_ref):
    groups = CONV_ROWS // SUBLANES
    r0 = c * CONV_ROWS
    accs = [jnp.zeros((SUBLANES, C_CHANNELS), F32) for _ in range(groups)]
    for k in range(C_CONV_WIDTH):
        a, b = divmod(k, SUBLANES)
        w_k = w_ref[k]
        for g in range(groups):
            lo = r0 + SUBLANES * (a + g)
            accs[g] = accs[g] + sh_ref[b, lo:lo + SUBLANES, :] * w_k
    acc = jnp.concatenate(accs, axis=0) + b_ref[...]
    mu = jnp.mean(acc, axis=-1, keepdims=True)
    xc = acc - mu
    y = xc * lax.rsqrt(jnp.mean(xc * xc, axis=-1, keepdims=True) + NORM_EPS)
    y = y * g_ref[...] + beta_ref[...]
    out_ref[row0 + r0:row0 + r0 + CONV_ROWS, :] = (y * jax.nn.sigmoid(y)).astype(BF16)


def _conv_operands(cu, n_rows, dw_w, dw_b, ln_g, ln_b):
    halo_per_tile = n_rows // CONV_HALO
    n_halo = cu.shape[0] // CONV_HALO
    vec = pl.BlockSpec((1, C_CHANNELS), lambda i: (0, 0))
    specs = [
        pl.BlockSpec((CONV_HALO, 2 * C_CHANNELS), lambda i: (jnp.maximum(i * halo_per_tile - 1, 0), 0)),
        pl.BlockSpec((n_rows, 2 * C_CHANNELS), lambda i: (i, 0)),
        pl.BlockSpec((CONV_HALO, 2 * C_CHANNELS), lambda i: (jnp.minimum((i + 1) * halo_per_tile, n_halo - 1), 0)),
        pl.BlockSpec((C_CONV_WIDTH, SUBLANES, C_CHANNELS), lambda i: (0, 0, 0)),
        vec, vec, vec,
    ]
    operands = [cu, cu, cu, jnp.broadcast_to(dw_w[:, None, :], (C_CONV_WIDTH, SUBLANES, C_CHANNELS)),
                dw_b.reshape(1, -1), ln_g.reshape(1, -1), ln_b.reshape(1, -1)]
    return operands, specs


def _conv_scratch(n_rows):
    return [pltpu.VMEM((n_rows + 2 * CONV_HALO, C_CHANNELS), F32),
            pltpu.VMEM((SUBLANES, CONV_SH_ROWS, C_CHANNELS), F32)]


def _conv_ctx_kernel(prev_ref, cur_ref, next_ref, w_ref, b_ref, g_ref, beta_ref, o_ref, ext_ref, sh_ref):
    _conv_fill(TC, True, True, prev_ref, cur_ref, next_ref, ext_ref)
    _conv_shift(0, ext_ref, sh_ref)
    for c in range(CONV_CHUNKS):
        _conv_chunk(0, c, sh_ref, w_ref, b_ref, g_ref, beta_ref, o_ref)


def _conv_ctx(cu_ctx, dw_w, dw_b, ln_g, ln_b):
    assert CTX_LEN == TC
    operands, specs = _conv_operands(cu_ctx, TC, dw_w, dw_b, ln_g, ln_b)
    return pl.pallas_call(
        _conv_ctx_kernel,
        grid=(N_CTX // TC,),
        in_specs=specs,
        out_specs=pl.BlockSpec((TC, C_CHANNELS), lambda i: (i, 0)),
        out_shape=jax.ShapeDtypeStruct((N_CTX, C_CHANNELS), BF16),
        scratch_shapes=_conv_scratch(TC),
        compiler_params=pltpu.CompilerParams(dimension_semantics=("arbitrary",)),
        name="conv_context",
    )(*operands)


N_CONV_OPERANDS = 7


def _outproj_kernel(widths, fused_conv, x_ref, mod_ref, w_ref, *refs):
    if fused_conv:
        pieces = list(refs[:len(widths) - 1])
        conv_refs = refs[len(widths) - 1:len(widths) - 1 + N_CONV_OPERANDS]
        o_ref, ext_ref, sh_ref, oc_ref = refs[len(widths) - 1 + N_CONV_OPERANDS:]
    else:
        pieces, o_ref = list(refs[:-1]), refs[-1]
    y = None
    k0 = 0
    for piece, width in zip(pieces, widths):
        t = jnp.dot(piece[...], w_ref[k0:k0 + width, :], preferred_element_type=F32)
        y = t if y is None else y + t
        k0 += width
    if fused_conv:
        prev_ref, cur_ref, next_ref, cw_ref, cb_ref, cg_ref, cbeta_ref = conv_refs
        i = pl.program_id(0)
        _conv_fill(TM, i % TILES_PER_BATCH == 0, i % TILES_PER_BATCH == TILES_PER_BATCH - 1,
                   prev_ref, cur_ref, next_ref, ext_ref)
        for row0 in range(0, TM, TC):
            _conv_shift(row0, ext_ref, sh_ref)
            for c in range(CONV_CHUNKS):
                _conv_chunk(row0, c, sh_ref, cw_ref, cb_ref, cg_ref, cbeta_ref, oc_ref)
        y = y + jnp.dot(oc_ref[...], w_ref[k0:, :], preferred_element_type=F32)
    o_ref[...] = x_ref[...] + mod_ref[2:3, :] * y


def _outproj(xs, is_ctx, mod_l, w_out, layer, pieces, conv=None):
    rows = xs.shape[0]
    widths = tuple(int(p.shape[1]) for p in pieces) + ((C_CHANNELS,) if conv is not None else ())
    assert sum(widths) == MIX_WIDTH and all(p.shape[0] == rows for p in pieces)
    blocks = 4 * TM * D_MODEL * 4 + MIX_WIDTH * D_MODEL * 2 + 2 * TM * MIX_WIDTH * 2
    group = _group_of_tile(is_ctx, TM)
    operands = [_in_hbm(xs), mod_l, w_out] + list(pieces)
    in_specs = [
        pl.BlockSpec((TM, D_MODEL), lambda i: (i, 0)),
        pl.BlockSpec((None, None, 3, D_MODEL), lambda i: (group(i), 1, 0, 0)),
        pl.BlockSpec((None, MIX_WIDTH, D_MODEL), lambda i: (layer, 0, 0), pipeline_mode=pl.Buffered(1)),
    ] + [pl.BlockSpec((TM, w), lambda i: (i, 0)) for w in widths[:len(pieces)]]
    scratch = []
    if conv is not None:
        assert not is_ctx and SEQ % TM == 0 and TM % TC == 0 and conv[0].shape[0] == rows
        conv_operands, conv_specs = _conv_operands(conv[0], TM, *conv[1:])
        operands += conv_operands
        in_specs += conv_specs
        scratch = _conv_scratch(TM) + [pltpu.VMEM((TM, C_CHANNELS), BF16)]
        blocks += 4 * TM * C_CHANNELS * 4 + (TM + SUBLANES * CONV_SH_ROWS) * C_CHANNELS * 4
    return pl.pallas_call(
        functools.partial(_outproj_kernel, widths, conv is not None),
        grid=(rows // TM,),
        in_specs=in_specs,
        out_specs=pl.BlockSpec((TM, D_MODEL), lambda i: (i, 0)),
        out_shape=jax.ShapeDtypeStruct((rows, D_MODEL), F32),
        scratch_shapes=scratch,
        input_output_aliases={0: 0},
        compiler_params=pltpu.CompilerParams(
            dimension_semantics=("arbitrary",),
            vmem_limit_bytes=_vmem_limit(blocks)),
        name="outproj",
    )(*operands)


def kernel(x, c, ctx, c_ctx, w_mod, b_mod, norm_ffn1, norm_mix, norm_ffn2, ffn1_w_gate, ffn1_w_up,
           ffn1_w_down, ffn2_w_gate, ffn2_w_up, ffn2_w_down, w_in, w_out, a_q_norm, a_k_norm, a_sink,
           b_q_norm, b_k_norm, b_rpb, c_dw_w, c_dw_b, c_ln_g, c_ln_b):
    mods = _mod_vectors(c, c_ctx, w_mod, b_mod)
    rope = _rope_tables()
    b_bias = _b_bias_tables(b_rpb)
    ffn1_w = [w.astype(BF16) for w in (ffn1_w_gate, ffn1_w_up, ffn1_w_down)]
    ffn2_w = [w.astype(BF16) for w in (ffn2_w_gate, ffn2_w_up, ffn2_w_down)]
    w_in_bf = w_in.astype(BF16)
    w_out_bf = w_out.astype(BF16)
    q_scale = HEAD_DIM ** -0.5 * LOG2_E
    xs = x.reshape(N_LAT, D_MODEL)
    cs = ctx.reshape(N_CTX, D_MODEL)
    for l in range(DEPTH):
        last = l == DEPTH - 1
        mod_l = mods[l]
        owned = l > 0
        xs = _ffn(xs, False, mod_l, 0, norm_ffn1[l], *ffn1_w, l, in_place=owned)
        cs = _ffn(cs, True, mod_l, 0, norm_ffn1[l], *ffn1_w, l, in_place=owned)
        head_gains = jnp.concatenate([
            (a_q_norm[l] * q_scale)[None], a_k_norm[l][None], (b_q_norm[l] * q_scale)[None],
            b_k_norm[l][None], jnp.zeros((SUBLANES - N_HEAD_GAINS, HEAD_DIM), F32)], axis=0)
        qkv, cu = _inproj(xs, False, mod_l, norm_mix[l], w_in_bf, l, head_gains, rope)
        qkv_ctx, cu_ctx = _inproj(cs, True, mod_l, norm_mix[l], w_in_bf, l, head_gains, rope)
        sink_l = a_sink[l] * LOG2_E
        o_a = _attn_a(qkv, qkv_ctx, sink_l)
        o_b = _attn_b(qkv, qkv_ctx, b_bias, l)
        conv_params = (c_dw_w[l], c_dw_b[l], c_ln_g[l], c_ln_b[l])
        xs = _outproj(xs, False, mod_l, w_out_bf, l, [o_a, o_b], conv=(cu,) + conv_params)
        xs = _ffn(xs, False, mod_l, 2, norm_ffn2[l], *ffn2_w, l, in_place=True)
        if not last:
            sinks = jnp.concatenate([sink_l, jnp.full((B_HEADS,), NEG_INF, F32)])
            o_ctx = _attn_ctx(qkv_ctx, sinks)
            o_c_ctx = _conv_ctx(cu_ctx, *conv_params)
            cs = _outproj(cs, True, mod_l, w_out_bf, l, [o_ctx, o_c_ctx])
            cs = _ffn(cs, True, mod_l, 2, norm_ffn2[l], *ffn2_w, l, in_place=True)
    return xs.reshape(BATCH, SEQ, D_MODEL)
```

```python
import functools

import numpy as np
import jax
import jax.numpy as jnp
from jax import lax
from jax.experimental import pallas as pl
from jax.experimental.pallas import tpu as pltpu

D_MODEL = 2048
BATCH = 2
SEQ = 16384
DEPTH = 2
GRID_W = 64
GRID_ROWS = SEQ // GRID_W
CTX_LEN = 256
HEAD_DIM = 128
A_Q_HEADS = 6
A_KV_HEADS = 2
A_GROUP = A_Q_HEADS // A_KV_HEADS
A_WINDOW = 128
B_HEADS = 6
B_WIN_H = 8
B_WIN_W = 16
C_CHANNELS = 512
C_CONV_WIDTH = 31
C_PAD = (C_CONV_WIDTH - 1) // 2
D_FF = 5632
ROPE_THETA = 10000.0
NORM_EPS = 1e-6
NEG_INF = -1e30
N_MOD = 9
IN_COLS = 4608
MIX_WIDTH = 2048

HEAD_AQ = 0
HEAD_AK = HEAD_AQ + A_Q_HEADS
HEAD_AV = HEAD_AK + A_KV_HEADS
HEAD_BQ = HEAD_AV + A_KV_HEADS
HEAD_BK = HEAD_BQ + B_HEADS
HEAD_BV = HEAD_BK + B_HEADS
N_QKV_HEADS = HEAD_BV + B_HEADS
C_COL0 = N_QKV_HEADS * HEAD_DIM

N_LAT = BATCH * SEQ
N_CTX = BATCH * CTX_LEN

V7X_VMEM_BYTES = 64 * 1024 * 1024
SUBLANES = 8
LANES = 128

TM = 512
TILES_PER_BATCH = SEQ // TM
TF = 512
TC = 256
CONV_HALO = 16

F32 = jnp.float32
BF16 = jnp.bfloat16


MIB = 1024 * 1024
VMEM_TEMPORARIES_BYTES = 12 * MIB
VMEM_UNREQUESTED_BYTES = 4 * MIB


def _vmem_limit(block_bytes):
    return int(min(V7X_VMEM_BYTES - VMEM_UNREQUESTED_BYTES, block_bytes + VMEM_TEMPORARIES_BYTES))


def _group_of_tile(is_ctx, tile_rows):
    return (lambda i: BATCH) if is_ctx else (lambda i: i * tile_rows // SEQ)


def _in_hbm(stream):
    return pltpu.with_memory_space_constraint(stream, pltpu.HBM)


def _modulated(x, gain, shift, scale):
    ms = jnp.mean(x * x, axis=-1, keepdims=True)
    return (x * lax.rsqrt(ms + NORM_EPS) * gain) * (1.0 + scale) + shift


MOD_TN = 1024
MOD_GROUPS = BATCH + 1
MOD_UNROLL = 8


def _mod_kernel(c_ref, w_ref, b_ref, o_ref, a_ref):
    @pl.when(jnp.logical_and(pl.program_id(0) == 0, pl.program_id(1) == 0))
    def _():
        c = c_ref[...]
        a_ref[...] = c * jax.nn.sigmoid(c)

    lane_tiles = MOD_TN // LANES

    def body(step, accs):
        accs = list(accs)
        for u in range(MOD_UNROLL):
            r = pl.multiple_of((step * MOD_UNROLL + u) * SUBLANES, SUBLANES)
            for m in range(MOD_GROUPS):
                a = a_ref[m, pl.ds(r, SUBLANES), :]
                for t in range(lane_tiles):
                    idx = m * lane_tiles + t
                    accs[idx] = accs[idx] + w_ref[pl.ds(r, SUBLANES), t * LANES:(t + 1) * LANES] * a
        return tuple(accs)

    zero = jnp.zeros((SUBLANES, LANES), F32)
    accs = lax.fori_loop(0, D_MODEL // (SUBLANES * MOD_UNROLL), body, (zero,) * (MOD_GROUPS * lane_tiles))
    o_ref[...] = jnp.zeros_like(o_ref)
    for m in range(MOD_GROUPS):
        row = jnp.concatenate([jnp.sum(accs[m * lane_tiles + t], axis=0, keepdims=True)
                               for t in range(lane_tiles)], axis=1)
        o_ref[m:m + 1, :] = row + b_ref[...]


def _mod_vectors(c, c_ctx, w_mod, b_mod):
    rows = jnp.concatenate([c, c_ctx[None, :]], axis=0)
    c_cols = jnp.broadcast_to(rows[:, :, None], (MOD_GROUPS, D_MODEL, LANES))
    n = N_MOD * D_MODEL
    out = pl.pallas_call(
        _mod_kernel,
        grid=(DEPTH, n // MOD_TN),
        in_specs=[
            pl.BlockSpec((MOD_GROUPS, D_MODEL, LANES), lambda l, j: (0, 0, 0)),
            pl.BlockSpec((None, D_MODEL, MOD_TN), lambda l, j: (l, 0, j)),
            pl.BlockSpec((None, 1, MOD_TN), lambda l, j: (l, 0, j)),
        ],
        out_specs=pl.BlockSpec((None, SUBLANES, MOD_TN), lambda l, j: (l, 0, j)),
        out_shape=jax.ShapeDtypeStruct((DEPTH, SUBLANES, n), F32),
        scratch_shapes=[pltpu.VMEM((MOD_GROUPS, D_MODEL, LANES), F32)],
        compiler_params=pltpu.CompilerParams(
            dimension_semantics=("arbitrary", "arbitrary"),
            vmem_limit_bytes=_vmem_limit(2 * D_MODEL * MOD_TN * 4 + 3 * MOD_GROUPS * D_MODEL * LANES * 4)),
        name="mod_vectors",
    )(c_cols, w_mod, b_mod.reshape(DEPTH, 1, n))
    return out[:, :MOD_GROUPS].reshape(DEPTH, MOD_GROUPS, 3, 3, D_MODEL)


FFN_SUB = 512
FFN_TM_LAT = 1024


def _ffn_kernel(tm, x_ref, mod_ref, gain_ref, wgu_ref, wd_ref, o_ref, h_ref):
    j = pl.program_id(1)
    last = pl.num_programs(1) - 1

    def chunk(first, final):
        for r0 in range(0, tm, FFN_SUB):
            rows = slice(r0, r0 + FFN_SUB)
            if first:
                h = _modulated(x_ref[rows, :], gain_ref[...], mod_ref[0:1, :], mod_ref[1:2, :])
                h_ref[rows, :] = h.astype(BF16)
            h = h_ref[rows, :]
            gu = jnp.dot(h, wgu_ref[...], preferred_element_type=F32)
            g = gu[:, :TF]
            a = (g * jax.nn.sigmoid(g)) * gu[:, TF:]
            d = jnp.dot(a.astype(BF16), wd_ref[...], preferred_element_type=F32)
            if first:
                o_ref[rows, :] = d
            elif final:
                o_ref[rows, :] = x_ref[rows, :] + (0.5 * mod_ref[2:3, :]) * (o_ref[rows, :] + d)
            else:
                o_ref[rows, :] += d

    pl.when(j == 0)(functools.partial(chunk, True, False))
    pl.when(jnp.logical_and(j > 0, j < last))(functools.partial(chunk, False, False))
    pl.when(j == last)(functools.partial(chunk, False, True))


def _gate_up_chunks(w_gate, w_up):
    chunks = lambda w: w.reshape(DEPTH, D_MODEL, D_FF // TF, TF)
    return jnp.concatenate([chunks(w_gate), chunks(w_up)], axis=3).astype(BF16).reshape(DEPTH, D_MODEL, 2 * D_FF)


def _ffn(x, is_ctx, mod_l, sub, gain, wgu, wd, layer, in_place):
    tm = TM if is_ctx else FFN_TM_LAT
    rows = x.shape[0]
    assert tm % FFN_SUB == 0 and rows % tm == 0 and SEQ % tm == 0 and D_FF // TF >= 2
    blocks = 4 * tm * D_MODEL * 4 + 6 * D_MODEL * TF * 2 + tm * D_MODEL * 2
    group = _group_of_tile(is_ctx, tm)
    return pl.pallas_call(
        functools.partial(_ffn_kernel, tm),
        grid=(rows // tm, D_FF // TF),
        in_specs=[
            pl.BlockSpec((tm, D_MODEL), lambda i, j: (i, 0)),
            pl.BlockSpec((None, None, 3, D_MODEL), lambda i, j: (group(i), sub, 0, 0)),
            pl.BlockSpec((1, D_MODEL), lambda i, j: (0, 0)),
            pl.BlockSpec((None, D_MODEL, 2 * TF), lambda i, j: (layer, 0, j)),
            pl.BlockSpec((None, TF, D_MODEL), lambda i, j: (layer, j, 0)),
        ],
        out_specs=pl.BlockSpec((tm, D_MODEL), lambda i, j: (i, 0)),
        out_shape=jax.ShapeDtypeStruct((rows, D_MODEL), F32),
        scratch_shapes=[pltpu.VMEM((tm, D_MODEL), BF16)],
        input_output_aliases={0: 0} if in_place else {},
        compiler_params=pltpu.CompilerParams(
            dimension_semantics=("arbitrary", "arbitrary"),
            vmem_limit_bytes=_vmem_limit(blocks)),
        name="ffn",
    )(_in_hbm(x), mod_l, gain.reshape(1, D_MODEL), wgu, wd)


GAIN_AQ, GAIN_AK, GAIN_BQ, GAIN_BK = 0, 1, 2, 3
N_HEAD_GAINS = 4


def _head_kind(h):
    if h < HEAD_AK:
        return GAIN_AQ, True
    if h < HEAD_AV:
        return GAIN_AK, True
    if h < HEAD_BQ:
        return None, False
    if h < HEAD_BK:
        return GAIN_BQ, False
    if h < HEAD_BV:
        return GAIN_BK, False
    return None, False


def _inproj_kernel(positioned, x_ref, mod_ref, gain_ref, w_ref, hg_ref, *refs):
    if positioned:
        cos_ref, sin_ref, qkv_ref, cu_ref, h_ref = refs
    else:
        qkv_ref, cu_ref, h_ref = refs
    h = _modulated(x_ref[...], gain_ref[...], mod_ref[0:1, :], mod_ref[1:2, :])
    h_ref[...] = h.astype(BF16)
    lane = lax.broadcasted_iota(jnp.int32, (TM, HEAD_DIM), 1)
    even_quarter = ((lane // (HEAD_DIM // 4)) % 2) == 0
    for pair in range(N_QKV_HEADS // 2):
        c0 = pair * 2 * HEAD_DIM
        y2 = jnp.dot(h_ref[...], w_ref[:, c0:c0 + 2 * HEAD_DIM], preferred_element_type=F32)
        for half in range(2):
            hd = 2 * pair + half
            y = y2[:, half * HEAD_DIM:(half + 1) * HEAD_DIM]
            gain_row, rotary = _head_kind(hd)
            if gain_row is not None:
                ms = jnp.mean(y * y, axis=-1, keepdims=True)
                y = y * lax.rsqrt(ms + NORM_EPS) * hg_ref[gain_row:gain_row + 1, :]
            if rotary and positioned:
                swapped = jnp.where(even_quarter,
                                    pltpu.roll(y, HEAD_DIM - HEAD_DIM // 4, 1),
                                    pltpu.roll(y, HEAD_DIM // 4, 1))
                y = y * cos_ref[...] + swapped * sin_ref[...]
            qkv_ref[hd] = y.astype(BF16)
    for blk in range((IN_COLS - C_COL0) // (2 * HEAD_DIM)):
        c0 = C_COL0 + blk * 2 * HEAD_DIM
        cu_ref[:, blk * 2 * HEAD_DIM:(blk + 1) * 2 * HEAD_DIM] = jnp.dot(
            h_ref[...], w_ref[:, c0:c0 + 2 * HEAD_DIM], preferred_element_type=F32)


def _inproj(xs, is_ctx, mod_l, gain, w_in, layer, head_gains, rope):
    rows = xs.shape[0]
    n_cu = IN_COLS - C_COL0
    blocks = (2 * TM * D_MODEL * 4 + D_MODEL * IN_COLS * 2 + 2 * N_QKV_HEADS * TM * HEAD_DIM * 2
              + 2 * TM * n_cu * 4 + 4 * TM * HEAD_DIM * 4 + TM * D_MODEL * 2)
    group = _group_of_tile(is_ctx, TM)
    operands = [xs, mod_l, gain.reshape(1, D_MODEL), w_in, head_gains]
    in_specs = [
        pl.BlockSpec((TM, D_MODEL), lambda i: (i, 0)),
        pl.BlockSpec((None, None, 3, D_MODEL), lambda i: (group(i), 1, 0, 0)),
        pl.BlockSpec((1, D_MODEL), lambda i: (0, 0)),
        pl.BlockSpec((None, D_MODEL, IN_COLS), lambda i: (layer, 0, 0), pipeline_mode=pl.Buffered(1)),
        pl.BlockSpec((SUBLANES, HEAD_DIM), lambda i: (0, 0)),
    ]
    if not is_ctx:
        operands += list(rope)
        in_specs += [pl.BlockSpec((TM, HEAD_DIM), lambda i: (i % TILES_PER_BATCH, 0))] * 2
    return pl.pallas_call(
        functools.partial(_inproj_kernel, not is_ctx),
        grid=(rows // TM,),
        in_specs=in_specs,
        out_specs=[
            pl.BlockSpec((N_QKV_HEADS, TM, HEAD_DIM), lambda i: (0, i, 0)),
            pl.BlockSpec((TM, n_cu), lambda i: (i, 0)),
        ],
        out_shape=[
            jax.ShapeDtypeStruct((N_QKV_HEADS, rows, HEAD_DIM), BF16),
            jax.ShapeDtypeStruct((rows, n_cu), F32),
        ],
        scratch_shapes=[pltpu.VMEM((TM, D_MODEL), BF16)],
        compiler_params=pltpu.CompilerParams(
            dimension_semantics=("arbitrary",),
            vmem_limit_bytes=_vmem_limit(blocks)),
        name="inproj",
    )(*operands)


def _rope_tables():
    t = np.arange(SEQ)
    n_freq = HEAD_DIM // 4
    inv_freq = ROPE_THETA ** (-np.arange(n_freq, dtype=np.float64) / n_freq)
    ang_r = (t // GRID_W)[:, None] * inv_freq[None, :]
    ang_c = (t % GRID_W)[:, None] * inv_freq[None, :]
    cos_t = np.concatenate([np.cos(ang_r), np.cos(ang_r), np.cos(ang_c), np.cos(ang_c)], axis=-1)
    sin_t = np.concatenate([-np.sin(ang_r), np.sin(ang_r), -np.sin(ang_c), np.sin(ang_c)], axis=-1)
    return jnp.asarray(cos_t, F32), jnp.asarray(sin_t, F32)


LOG2_E = float(np.log2(np.e))
ATTN_TQ = 256
ATTN_BLOCKS = SEQ // ATTN_TQ
ATTN_UNROLL = 16
PLACE_INTERIOR, PLACE_FIRST, PLACE_LAST = 0, 1, 2
N_PLACEMENTS = 3
PLACEMENT_BLOCKS = (1, 0, ATTN_BLOCKS - 1)

A_LEAD = A_WINDOW
A_KEYS = ATTN_TQ + 2 * A_WINDOW
B_ROWS_PER_BLOCK = ATTN_TQ // GRID_W
B_UNION_ROWS = B_ROWS_PER_BLOCK + B_WIN_H
B_LEAD = (B_WIN_H // 2) * GRID_W
B_KEYS = B_UNION_ROWS * GRID_W
assert ATTN_BLOCKS % ATTN_UNROLL == 0 and ATTN_BLOCKS >= 3
assert A_LEAD % LANES == 0 and B_LEAD % LANES == 0 and A_KEYS % LANES == 0 and B_KEYS % LANES == 0


def _dot_nt(a, b):
    return lax.dot_general(a, b, (((1,), (1,)), ((), ())), preferred_element_type=F32)


def _window_start(first_query, lead, n_keys):
    return int(np.clip(first_query - lead, 0, SEQ - n_keys))


def _local_attn_kernel(lead, n_keys, has_sink, *refs):
    if has_sink:
        sink_ref, refs = refs[0], refs[1:]
    q_ref, k_ref, v_ref, kc_ref, vc_ref, bias_ref, o_ref = refs
    sink = sink_ref[pl.program_id(1) * A_GROUP + pl.program_id(2)] if has_sink else None
    kc = kc_ref[...]
    vc = vc_ref[...]

    def scores(i):
        q0 = pl.multiple_of(i * ATTN_TQ, ATTN_TQ)
        k0 = pl.multiple_of(jnp.clip(q0 - lead, 0, SEQ - n_keys), LANES)
        placement = jnp.where(i == 0, PLACE_FIRST, jnp.where(i == ATTN_BLOCKS - 1, PLACE_LAST, PLACE_INTERIOR))
        q = q_ref[pl.ds(q0, ATTN_TQ), :]
        s = _dot_nt(q, k_ref[pl.ds(k0, n_keys), :]) + bias_ref[placement]
        return q0, k0, s, _dot_nt(q, kc)

    def softmax(q0, k0, s, sc):
        m = jnp.maximum(jnp.max(s, axis=-1, keepdims=True), jnp.max(sc, axis=-1, keepdims=True))
        if has_sink:
            m = jnp.maximum(m, sink)
        p = jnp.exp2(s - m)
        pc = jnp.exp2(sc - m)
        den = jnp.sum(p, axis=-1, keepdims=True) + jnp.sum(pc, axis=-1, keepdims=True)
        if has_sink:
            den = den + jnp.exp2(sink - m)
        return q0, k0, p.astype(BF16), pc.astype(BF16), den

    def output(q0, k0, p, pc, den):
        o = (jnp.dot(p, v_ref[pl.ds(k0, n_keys), :], preferred_element_type=F32)
             + jnp.dot(pc, vc, preferred_element_type=F32))
        o_ref[pl.ds(q0, ATTN_TQ), :] = (o / den).astype(BF16)

    def body(it, carry):
        scored, weighted = {}, {}
        for step in range(ATTN_UNROLL + 2):
            if step < ATTN_UNROLL:
                scored[step] = scores(it * ATTN_UNROLL + step)
            if 0 <= step - 2 < ATTN_UNROLL:
                output(*weighted.pop(step - 2))
            if 0 <= step - 1 < ATTN_UNROLL:
                weighted[step - 1] = softmax(*scored.pop(step - 1))
        return carry

    lax.fori_loop(0, ATTN_BLOCKS // ATTN_UNROLL, body, 0)


def _local_attn_vmem(n_keys):
    return 2 * 4 * SEQ * HEAD_DIM * 2 + 4 * CTX_LEN * HEAD_DIM * 2 + 2 * N_PLACEMENTS * ATTN_TQ * n_keys * 4


SEQ_BLOCK = (None, SEQ, HEAD_DIM)
CTX_BLOCK = (None, CTX_LEN, HEAD_DIM)


def _a_band_table():
    tables = []
    for blk in PLACEMENT_BLOCKS:
        q0 = blk * ATTN_TQ
        kpos = _window_start(q0, A_LEAD, A_KEYS) + np.arange(A_KEYS)[None, :]
        qpos = q0 + np.arange(ATTN_TQ)[:, None]
        tables.append(np.where(np.abs(kpos - qpos) <= A_WINDOW, 0.0, NEG_INF))
    return jnp.asarray(np.stack(tables), F32)


def _attn_a(qkv, qkv_ctx, sink):
    return pl.pallas_call(
        functools.partial(_local_attn_kernel, A_LEAD, A_KEYS, True),
        grid=(BATCH, A_KV_HEADS, A_GROUP),
        in_specs=[
            pl.BlockSpec(memory_space=pltpu.SMEM),
            pl.BlockSpec(SEQ_BLOCK, lambda b, kv, g: (HEAD_AQ + kv * A_GROUP + g, b, 0)),
            pl.BlockSpec(SEQ_BLOCK, lambda b, kv, g: (HEAD_AK + kv, b, 0)),
            pl.BlockSpec(SEQ_BLOCK, lambda b, kv, g: (HEAD_AV + kv, b, 0)),
            pl.BlockSpec(CTX_BLOCK, lambda b, kv, g: (HEAD_AK + kv, b, 0)),
            pl.BlockSpec(CTX_BLOCK, lambda b, kv, g: (HEAD_AV + kv, b, 0)),
            pl.BlockSpec((N_PLACEMENTS, ATTN_TQ, A_KEYS), lambda b, kv, g: (0, 0, 0)),
        ],
        out_specs=pl.BlockSpec((SEQ, HEAD_DIM), lambda b, kv, g: (b, kv * A_GROUP + g)),
        out_shape=jax.ShapeDtypeStruct((N_LAT, A_Q_HEADS * HEAD_DIM), BF16),
        compiler_params=pltpu.CompilerParams(
            dimension_semantics=("arbitrary", "arbitrary", "arbitrary"),
            vmem_limit_bytes=_vmem_limit(_local_attn_vmem(A_KEYS))),
        name="attn_window",
    )(sink, qkv, qkv, qkv, qkv_ctx, qkv_ctx, _a_band_table())


def _b_bias_tables(rpb):
    j = np.arange(GRID_W)[:, None]
    jk = np.arange(GRID_W)[None, :]
    col_start = np.clip(j - B_WIN_W // 2, 0, GRID_W - B_WIN_W)
    inside = (jk >= col_start) & (jk < col_start + B_WIN_W)
    n_col = 2 * B_WIN_W - 1
    onehot = (inside[:, :, None] & ((jk - j + (B_WIN_W - 1))[:, :, None] == np.arange(n_col))).astype(np.float32)
    by_row = jnp.einsum('lhac,jkc->lhajk', rpb, jnp.asarray(onehot), precision=lax.Precision.HIGHEST)
    by_row = by_row * LOG2_E + jnp.asarray(np.where(inside, 0.0, NEG_INF), F32)
    masked = jnp.full((DEPTH, B_HEADS, GRID_W, GRID_W), NEG_INF, F32)
    query_rows = []
    for blk in PLACEMENT_BLOCKS:
        r0 = blk * B_ROWS_PER_BLOCK
        u0 = _window_start(r0 * GRID_W, B_LEAD, B_KEYS) // GRID_W
        for t in range(B_ROWS_PER_BLOCK):
            r = r0 + t
            rs = int(np.clip(r - B_WIN_H // 2, 0, GRID_ROWS - B_WIN_H))
            before = rs - u0
            assert 0 <= before <= B_UNION_ROWS - B_WIN_H
            bias_row0 = rs - r + (B_WIN_H - 1)
            pieces = [by_row[:, :, bias_row0 + u - before] if before <= u < before + B_WIN_H else masked
                      for u in range(B_UNION_ROWS)]
            query_rows.append(jnp.concatenate(pieces, axis=-1))
    return jnp.stack(query_rows, axis=2).reshape(DEPTH, B_HEADS, N_PLACEMENTS, ATTN_TQ, B_KEYS)


def _attn_b(qkv, qkv_ctx, bias, layer):
    return pl.pallas_call(
        functools.partial(_local_attn_kernel, B_LEAD, B_KEYS, False),
        grid=(BATCH, B_HEADS),
        in_specs=[
            pl.BlockSpec(SEQ_BLOCK, lambda b, h: (HEAD_BQ + h, b, 0)),
            pl.BlockSpec(SEQ_BLOCK, lambda b, h: (HEAD_BK + h, b, 0)),
            pl.BlockSpec(SEQ_BLOCK, lambda b, h: (HEAD_BV + h, b, 0)),
            pl.BlockSpec(CTX_BLOCK, lambda b, h: (HEAD_BK + h, b, 0)),
            pl.BlockSpec(CTX_BLOCK, lambda b, h: (HEAD_BV + h, b, 0)),
            pl.BlockSpec((None, None, N_PLACEMENTS, ATTN_TQ, B_KEYS), lambda b, h: (layer, h, 0, 0, 0)),
        ],
        out_specs=pl.BlockSpec((SEQ, HEAD_DIM), lambda b, h: (b, h)),
        out_shape=jax.ShapeDtypeStruct((N_LAT, B_HEADS * HEAD_DIM), BF16),
        compiler_params=pltpu.CompilerParams(
            dimension_semantics=("arbitrary", "arbitrary"),
            vmem_limit_bytes=_vmem_limit(_local_attn_vmem(B_KEYS))),
        name="attn_neighbourhood",
    )(qkv, qkv, qkv, qkv_ctx, qkv_ctx, bias)


N_CTX_HEADS = A_Q_HEADS + B_HEADS


def _attn_ctx_kernel(sink_ref, q_ref, k_ref, v_ref, o_ref):
    sink = sink_ref[pl.program_id(1)]
    q = q_ref[...]
    s = _dot_nt(q, k_ref[...])
    m = jnp.maximum(jnp.max(s, axis=-1, keepdims=True), sink)
    p = jnp.exp2(s - m)
    den = jnp.sum(p, axis=-1, keepdims=True) + jnp.exp2(sink - m)
    o = jnp.dot(p.astype(BF16), v_ref[...], preferred_element_type=F32)
    o_ref[...] = (o / den).astype(BF16)


def _attn_ctx(qkv_ctx, sinks):
    is_b = lambda h: h >= A_Q_HEADS
    q_head = lambda h: jnp.where(is_b(h), HEAD_BQ + h - A_Q_HEADS, HEAD_AQ + h)
    k_head = lambda h: jnp.where(is_b(h), HEAD_BK + h - A_Q_HEADS, HEAD_AK + h // A_GROUP)
    v_head = lambda h: jnp.where(is_b(h), HEAD_BV + h - A_Q_HEADS, HEAD_AV + h // A_GROUP)
    return pl.pallas_call(
        _attn_ctx_kernel,
        grid=(BATCH, N_CTX_HEADS),
        in_specs=[
            pl.BlockSpec(memory_space=pltpu.SMEM),
            pl.BlockSpec(CTX_BLOCK, lambda b, h: (q_head(h), b, 0)),
            pl.BlockSpec(CTX_BLOCK, lambda b, h: (k_head(h), b, 0)),
            pl.BlockSpec(CTX_BLOCK, lambda b, h: (v_head(h), b, 0)),
        ],
        out_specs=pl.BlockSpec((CTX_LEN, HEAD_DIM), lambda b, h: (b, h)),
        out_shape=jax.ShapeDtypeStruct((N_CTX, N_CTX_HEADS * HEAD_DIM), BF16),
        compiler_params=pltpu.CompilerParams(dimension_semantics=("arbitrary", "arbitrary")),
        name="attn_context",
    )(sinks, qkv_ctx, qkv_ctx, qkv_ctx)


CONV_ROWS = 32
CONV_SH_ROWS = TC + SUBLANES * ((C_CONV_WIDTH - 1) // SUBLANES)
assert SUBLANES - 1 + CONV_HALO - C_PAD + CONV_SH_ROWS <= TC + 2 * CONV_HALO
CONV_CHUNKS = TC // CONV_ROWS


def _glu(u):
    return u[:, :C_CHANNELS] * jax.nn.sigmoid(u[:, C_CHANNELS:])


def _conv_fill(n_rows, first, last, prev_ref, cur_ref, next_ref, ext_ref):
    ext_ref[0:CONV_HALO, :] = jnp.where(first, 0.0, _glu(prev_ref[...]))
    ext_ref[CONV_HALO:CONV_HALO + n_rows, :] = _glu(cur_ref[...])
    ext_ref[CONV_HALO + n_rows:, :] = jnp.where(last, 0.0, _glu(next_ref[...]))


def _conv_shift(row0, ext_ref, sh_ref):
    for b in range(SUBLANES):
        lo = row0 + b + CONV_HALO - C_PAD
        sh_ref[b] = ext_ref[lo:lo + CONV_SH_ROWS, :]


def _conv_chunk(row0, c, sh_ref, w_ref, b_ref, g_ref, beta_ref, out_ref):
    groups = CONV_ROWS // SUBLANES
    r0 = c * CONV_ROWS
    accs = [jnp.zeros((SUBLANES, C_CHANNELS), F32) for _ in range(groups)]
    for k in range(C_CONV_WIDTH):
        a, b = divmod(k, SUBLANES)
        w_k = w_ref[k]
        for g in range(groups):
            lo = r0 + SUBLANES * (a + g)
            accs[g] = accs[g] + sh_ref[b, lo:lo + SUBLANES, :] * w_k
    acc = jnp.concatenate(accs, axis=0) + b_ref[...]
    mu = jnp.mean(acc, axis=-1, keepdims=True)
    xc = acc - mu
    y = xc * lax.rsqrt(jnp.mean(xc * xc, axis=-1, keepdims=True) + NORM_EPS)
    y = y * g_ref[...] + beta_ref[...]
    out_ref[row0 + r0:row0 + r0 + CONV_ROWS, :] = (y * jax.nn.sigmoid(y)).astype(BF16)


def _conv_operands(cu, n_rows, dw_w, dw_b, ln_g, ln_b):
    halo_per_tile = n_rows // CONV_HALO
    n_halo = cu.shape[0] // CONV_HALO
    vec = pl.BlockSpec((1, C_CHANNELS), lambda i: (0, 0))
    specs = [
        pl.BlockSpec((CONV_HALO, 2 * C_CHANNELS), lambda i: (jnp.maximum(i * halo_per_tile - 1, 0), 0)),
        pl.BlockSpec((n_rows, 2 * C_CHANNELS), lambda i: (i, 0)),
        pl.BlockSpec((CONV_HALO, 2 * C_CHANNELS), lambda i: (jnp.minimum((i + 1) * halo_per_tile, n_halo - 1), 0)),
        pl.BlockSpec((C_CONV_WIDTH, SUBLANES, C_CHANNELS), lambda i: (0, 0, 0)),
        vec, vec, vec,
    ]
    operands = [cu, cu, cu, jnp.broadcast_to(dw_w[:, None, :], (C_CONV_WIDTH, SUBLANES, C_CHANNELS)),
                dw_b.reshape(1, -1), ln_g.reshape(1, -1), ln_b.reshape(1, -1)]
    return operands, specs


def _conv_scratch(n_rows):
    return [pltpu.VMEM((n_rows + 2 * CONV_HALO, C_CHANNELS), F32),
            pltpu.VMEM((SUBLANES, CONV_SH_ROWS, C_CHANNELS), F32)]


def _conv_ctx_kernel(prev_ref, cur_ref, next_ref, w_ref, b_ref, g_ref, beta_ref, o_ref, ext_ref, sh_ref):
    _conv_fill(TC, True, True, prev_ref, cur_ref, next_ref, ext_ref)
    _conv_shift(0, ext_ref, sh_ref)
    for c in range(CONV_CHUNKS):
        _conv_chunk(0, c, sh_ref, w_ref, b_ref, g_ref, beta_ref, o_ref)


def _conv_ctx(cu_ctx, dw_w, dw_b, ln_g, ln_b):
    assert CTX_LEN == TC
    operands, specs = _conv_operands(cu_ctx, TC, dw_w, dw_b, ln_g, ln_b)
    return pl.pallas_call(
        _conv_ctx_kernel,
        grid=(N_CTX // TC,),
        in_specs=specs,
        out_specs=pl.BlockSpec((TC, C_CHANNELS), lambda i: (i, 0)),
        out_shape=jax.ShapeDtypeStruct((N_CTX, C_CHANNELS), BF16),
        scratch_shapes=_conv_scratch(TC),
        compiler_params=pltpu.CompilerParams(dimension_semantics=("arbitrary",)),
        name="conv_context",
    )(*operands)


N_CONV_OPERANDS = 7


def _outproj_kernel(widths, fused_conv, x_ref, mod_ref, w_ref, *refs):
    if fused_conv:
        pieces = list(refs[:len(widths) - 1])
        conv_refs = refs[len(widths) - 1:len(widths) - 1 + N_CONV_OPERANDS]
        o_ref, ext_ref, sh_ref, oc_ref = refs[len(widths) - 1 + N_CONV_OPERANDS:]
    else:
        pieces, o_ref = list(refs[:-1]), refs[-1]
    y = None
    k0 = 0
    for piece, width in zip(pieces, widths):
        t = jnp.dot(piece[...], w_ref[k0:k0 + width, :], preferred_element_type=F32)
        y = t if y is None else y + t
        k0 += width
    if fused_conv:
        prev_ref, cur_ref, next_ref, cw_ref, cb_ref, cg_ref, cbeta_ref = conv_refs
        i = pl.program_id(0)
        _conv_fill(TM, i % TILES_PER_BATCH == 0, i % TILES_PER_BATCH == TILES_PER_BATCH - 1,
                   prev_ref, cur_ref, next_ref, ext_ref)
        for row0 in range(0, TM, TC):
            _conv_shift(row0, ext_ref, sh_ref)
            for c in range(CONV_CHUNKS):
                _conv_chunk(row0, c, sh_ref, cw_ref, cb_ref, cg_ref, cbeta_ref, oc_ref)
        y = y + jnp.dot(oc_ref[...], w_ref[k0:, :], preferred_element_type=F32)
    o_ref[...] = x_ref[...] + mod_ref[2:3, :] * y


def _outproj(xs, is_ctx, mod_l, w_out, layer, pieces, conv=None):
    rows = xs.shape[0]
    widths = tuple(int(p.shape[1]) for p in pieces) + ((C_CHANNELS,) if conv is not None else ())
    assert sum(widths) == MIX_WIDTH and all(p.shape[0] == rows for p in pieces)
    blocks = 4 * TM * D_MODEL * 4 + MIX_WIDTH * D_MODEL * 2 + 2 * TM * MIX_WIDTH * 2
    group = _group_of_tile(is_ctx, TM)
    operands = [_in_hbm(xs), mod_l, w_out] + list(pieces)
    in_specs = [
        pl.BlockSpec((TM, D_MODEL), lambda i: (i, 0)),
        pl.BlockSpec((None, None, 3, D_MODEL), lambda i: (group(i), 1, 0, 0)),
        pl.BlockSpec((None, MIX_WIDTH, D_MODEL), lambda i: (layer, 0, 0), pipeline_mode=pl.Buffered(1)),
    ] + [pl.BlockSpec((TM, w), lambda i: (i, 0)) for w in widths[:len(pieces)]]
    scratch = []
    if conv is not None:
        assert not is_ctx and SEQ % TM == 0 and TM % TC == 0 and conv[0].shape[0] == rows
        conv_operands, conv_specs = _conv_operands(conv[0], TM, *conv[1:])
        operands += conv_operands
        in_specs += conv_specs
        scratch = _conv_scratch(TM) + [pltpu.VMEM((TM, C_CHANNELS), BF16)]
        blocks += 4 * TM * C_CHANNELS * 4 + (TM + SUBLANES * CONV_SH_ROWS) * C_CHANNELS * 4
    return pl.pallas_call(
        functools.partial(_outproj_kernel, widths, conv is not None),
        grid=(rows // TM,),
        in_specs=in_specs,
        out_specs=pl.BlockSpec((TM, D_MODEL), lambda i: (i, 0)),
        out_shape=jax.ShapeDtypeStruct((rows, D_MODEL), F32),
        scratch_shapes=scratch,
        input_output_aliases={0: 0},
        compiler_params=pltpu.CompilerParams(
            dimension_semantics=("arbitrary",),
            vmem_limit_bytes=_vmem_limit(blocks)),
        name="outproj",
    )(*operands)


def kernel(x, c, ctx, c_ctx, w_mod, b_mod, norm_ffn1, norm_mix, norm_ffn2, ffn1_w_gate, ffn1_w_up,
           ffn1_w_down, ffn2_w_gate, ffn2_w_up, ffn2_w_down, w_in, w_out, a_q_norm, a_k_norm, a_sink,
           b_q_norm, b_k_norm, b_rpb, c_dw_w, c_dw_b, c_ln_g, c_ln_b):
    mods = _mod_vectors(c, c_ctx, w_mod, b_mod)
    rope = _rope_tables()
    b_bias = _b_bias_tables(b_rpb)
    ffn1_w = [_gate_up_chunks(ffn1_w_gate, ffn1_w_up), ffn1_w_down.astype(BF16)]
    ffn2_w = [_gate_up_chunks(ffn2_w_gate, ffn2_w_up), ffn2_w_down.astype(BF16)]
    w_in_bf = w_in.astype(BF16)
    w_out_bf = w_out.astype(BF16)
    q_scale = HEAD_DIM ** -0.5 * LOG2_E
    xs = x.reshape(N_LAT, D_MODEL)
    cs = ctx.reshape(N_CTX, D_MODEL)
    for l in range(DEPTH):
        last = l == DEPTH - 1
        mod_l = mods[l]
        owned = l > 0
        xs = _ffn(xs, False, mod_l, 0, norm_ffn1[l], *ffn1_w, l, in_place=owned)
        cs = _ffn(cs, True, mod_l, 0, norm_ffn1[l], *ffn1_w, l, in_place=owned)
        head_gains = jnp.concatenate([
            (a_q_norm[l] * q_scale)[None], a_k_norm[l][None], (b_q_norm[l] * q_scale)[None],
            b_k_norm[l][None], jnp.zeros((SUBLANES - N_HEAD_GAINS, HEAD_DIM), F32)], axis=0)
        qkv, cu = _inproj(xs, False, mod_l, norm_mix[l], w_in_bf, l, head_gains, rope)
        qkv_ctx, cu_ctx = _inproj(cs, True, mod_l, norm_mix[l], w_in_bf, l, head_gains, rope)
        sink_l = a_sink[l] * LOG2_E
        o_a = _attn_a(qkv, qkv_ctx, sink_l)
        o_b = _attn_b(qkv, qkv_ctx, b_bias, l)
        conv_params = (c_dw_w[l], c_dw_b[l], c_ln_g[l], c_ln_b[l])
        xs = _outproj(xs, False, mod_l, w_out_bf, l, [o_a, o_b], conv=(cu,) + conv_params)
        xs = _ffn(xs, False, mod_l, 2, norm_ffn2[l], *ffn2_w, l, in_place=True)
        if not last:
            sinks = jnp.concatenate([sink_l, jnp.full((B_HEADS,), NEG_INF, F32)])
            o_ctx = _attn_ctx(qkv_ctx, sinks)
            o_c_ctx = _conv_ctx(cu_ctx, *conv_params)
            cs = _outproj(cs, True, mod_l, w_out_bf, l, [o_ctx, o_c_ctx])
            cs = _ffn(cs, True, mod_l, 2, norm_ffn2[l], *ffn2_w, l, in_place=True)
    return xs.reshape(BATCH, SEQ, D_MODEL)
```

```python
import functools

import numpy as np
import jax
import jax.numpy as jnp
from jax import lax
from jax.experimental import pallas as pl
from jax.experimental.pallas import tpu as pltpu

D_MODEL = 2048
BATCH = 2
SEQ = 16384
DEPTH = 2
GRID_W = 64
GRID_ROWS = SEQ // GRID_W
CTX_LEN = 256
HEAD_DIM = 128
A_Q_HEADS = 6
A_KV_HEADS = 2
A_GROUP = A_Q_HEADS // A_KV_HEADS
A_WINDOW = 128
B_HEADS = 6
B_WIN_H = 8
B_WIN_W = 16
C_CHANNELS = 512
C_CONV_WIDTH = 31
C_PAD = (C_CONV_WIDTH - 1) // 2
D_FF = 5632
ROPE_THETA = 10000.0
NORM_EPS = 1e-6
NEG_INF = -1e30
N_MOD = 9
IN_COLS = 4608
MIX_WIDTH = 2048

HEAD_AQ = 0
HEAD_AK = HEAD_AQ + A_Q_HEADS
HEAD_AV = HEAD_AK + A_KV_HEADS
HEAD_BQ = HEAD_AV + A_KV_HEADS
HEAD_BK = HEAD_BQ + B_HEADS
HEAD_BV = HEAD_BK + B_HEADS
N_QKV_HEADS = HEAD_BV + B_HEADS
C_COL0 = N_QKV_HEADS * HEAD_DIM

N_LAT = BATCH * SEQ
N_CTX = BATCH * CTX_LEN

V7X_VMEM_BYTES = 64 * 1024 * 1024
SUBLANES = 8
LANES = 128

TM = 512
TILES_PER_BATCH = SEQ // TM
TF = 512
TC = 256
CONV_HALO = 16

F32 = jnp.float32
BF16 = jnp.bfloat16


MIB = 1024 * 1024
VMEM_TEMPORARIES_BYTES = 12 * MIB
VMEM_UNREQUESTED_BYTES = 4 * MIB


def _vmem_limit(block_bytes):
    return int(min(V7X_VMEM_BYTES - VMEM_UNREQUESTED_BYTES, block_bytes + VMEM_TEMPORARIES_BYTES))


def _group_of_tile(is_ctx, tile_rows):
    return (lambda i: BATCH) if is_ctx else (lambda i: i * tile_rows // SEQ)


def _in_hbm(stream):
    return pltpu.with_memory_space_constraint(stream, pltpu.HBM)


def _modulated(x, gain, shift, scale):
    ms = jnp.mean(x * x, axis=-1, keepdims=True)
    return (x * lax.rsqrt(ms + NORM_EPS) * gain) * (1.0 + scale) + shift


MOD_TN = 1024
MOD_GROUPS = BATCH + 1
MOD_UNROLL = 8


def _mod_kernel(c_ref, w_ref, b_ref, o_ref, a_ref):
    @pl.when(jnp.logical_and(pl.program_id(0) == 0, pl.program_id(1) == 0))
    def _():
        c = c_ref[...]
        a_ref[...] = c * jax.nn.sigmoid(c)

    lane_tiles = MOD_TN // LANES

    def body(step, accs):
        accs = list(accs)
        for u in range(MOD_UNROLL):
            r = pl.multiple_of((step * MOD_UNROLL + u) * SUBLANES, SUBLANES)
            for m in range(MOD_GROUPS):
                a = a_ref[m, pl.ds(r, SUBLANES), :]
                for t in range(lane_tiles):
                    idx = m * lane_tiles + t
                    accs[idx] = accs[idx] + w_ref[pl.ds(r, SUBLANES), t * LANES:(t + 1) * LANES] * a
        return tuple(accs)

    zero = jnp.zeros((SUBLANES, LANES), F32)
    accs = lax.fori_loop(0, D_MODEL // (SUBLANES * MOD_UNROLL), body, (zero,) * (MOD_GROUPS * lane_tiles))
    o_ref[...] = jnp.zeros_like(o_ref)
    for m in range(MOD_GROUPS):
        row = jnp.concatenate([jnp.sum(accs[m * lane_tiles + t], axis=0, keepdims=True)
                               for t in range(lane_tiles)], axis=1)
        o_ref[m:m + 1, :] = row + b_ref[...]


def _mod_vectors(c, c_ctx, w_mod, b_mod):
    rows = jnp.concatenate([c, c_ctx[None, :]], axis=0)
    c_cols = jnp.broadcast_to(rows[:, :, None], (MOD_GROUPS, D_MODEL, LANES))
    n = N_MOD * D_MODEL
    out = pl.pallas_call(
        _mod_kernel,
        grid=(DEPTH, n // MOD_TN),
        in_specs=[
            pl.BlockSpec((MOD_GROUPS, D_MODEL, LANES), lambda l, j: (0, 0, 0)),
            pl.BlockSpec((None, D_MODEL, MOD_TN), lambda l, j: (l, 0, j)),
            pl.BlockSpec((None, 1, MOD_TN), lambda l, j: (l, 0, j)),
        ],
        out_specs=pl.BlockSpec((None, SUBLANES, MOD_TN), lambda l, j: (l, 0, j)),
        out_shape=jax.ShapeDtypeStruct((DEPTH, SUBLANES, n), F32),
        scratch_shapes=[pltpu.VMEM((MOD_GROUPS, D_MODEL, LANES), F32)],
        compiler_params=pltpu.CompilerParams(
            dimension_semantics=("arbitrary", "arbitrary"),
            vmem_limit_bytes=_vmem_limit(2 * D_MODEL * MOD_TN * 4 + 3 * MOD_GROUPS * D_MODEL * LANES * 4)),
        name="mod_vectors",
    )(c_cols, w_mod, b_mod.reshape(DEPTH, 1, n))
    return out[:, :MOD_GROUPS].reshape(DEPTH, MOD_GROUPS, 3, 3, D_MODEL)


FFN_SUB = 512
FFN_TM_LAT = 1024


def _ffn_kernel(tm, x_ref, mod_ref, gain_ref, wg_ref, wu_ref, wd_ref, o_ref, h_ref):
    j = pl.program_id(1)
    last = pl.num_programs(1) - 1

    def chunk(first, final):
        for r0 in range(0, tm, FFN_SUB):
            rows = slice(r0, r0 + FFN_SUB)
            if first:
                h = _modulated(x_ref[rows, :], gain_ref[...], mod_ref[0:1, :], mod_ref[1:2, :])
                h_ref[rows, :] = h.astype(BF16)
            h = h_ref[rows, :]
            g = jnp.dot(h, wg_ref[...], preferred_element_type=F32)
            u = jnp.dot(h, wu_ref[...], preferred_element_type=F32)
            a = (g * jax.nn.sigmoid(g)) * u
            d = jnp.dot(a.astype(BF16), wd_ref[...], preferred_element_type=F32)
            if first:
                o_ref[rows, :] = d
            elif final:
                o_ref[rows, :] = x_ref[rows, :] + (0.5 * mod_ref[2:3, :]) * (o_ref[rows, :] + d)
            else:
                o_ref[rows, :] += d

    pl.when(j == 0)(functools.partial(chunk, True, False))
    pl.when(jnp.logical_and(j > 0, j < last))(functools.partial(chunk, False, False))
    pl.when(j == last)(functools.partial(chunk, False, True))


def _ffn(x, is_ctx, mod_l, sub, gain, wg, wu, wd, layer, in_place):
    tm = TM if is_ctx else FFN_TM_LAT
    rows = x.shape[0]
    assert tm % FFN_SUB == 0 and rows % tm == 0 and SEQ % tm == 0 and D_FF // TF >= 2
    blocks = 4 * tm * D_MODEL * 4 + 6 * D_MODEL * TF * 2 + tm * D_MODEL * 2
    group = _group_of_tile(is_ctx, tm)
    return pl.pallas_call(
        functools.partial(_ffn_kernel, tm),
        grid=(rows // tm, D_FF // TF),
        in_specs=[
            pl.BlockSpec((tm, D_MODEL), lambda i, j: (i, 0)),
            pl.BlockSpec((None, None, 3, D_MODEL), lambda i, j: (group(i), sub, 0, 0)),
            pl.BlockSpec((1, D_MODEL), lambda i, j: (0, 0)),
            pl.BlockSpec((None, D_MODEL, TF), lambda i, j: (layer, 0, j)),
            pl.BlockSpec((None, D_MODEL, TF), lambda i, j: (layer, 0, j)),
            pl.BlockSpec((None, TF, D_MODEL), lambda i, j: (layer, j, 0)),
        ],
        out_specs=pl.BlockSpec((tm, D_MODEL), lambda i, j: (i, 0)),
        out_shape=jax.ShapeDtypeStruct((rows, D_MODEL), F32),
        scratch_shapes=[pltpu.VMEM((tm, D_MODEL), BF16)],
        input_output_aliases={0: 0} if in_place else {},
        compiler_params=pltpu.CompilerParams(
            dimension_semantics=("arbitrary", "arbitrary"),
            vmem_limit_bytes=_vmem_limit(blocks)),
        name="ffn",
    )(_in_hbm(x), mod_l, gain.reshape(1, D_MODEL), wg, wu, wd)


GAIN_AQ, GAIN_AK, GAIN_BQ, GAIN_BK = 0, 1, 2, 3
N_HEAD_GAINS = 4


def _head_kind(h):
    if h < HEAD_AK:
        return GAIN_AQ, True
    if h < HEAD_AV:
        return GAIN_AK, True
    if h < HEAD_BQ:
        return None, False
    if h < HEAD_BK:
        return GAIN_BQ, False
    if h < HEAD_BV:
        return GAIN_BK, False
    return None, False


def _inproj_kernel(positioned, x_ref, mod_ref, gain_ref, w_ref, hg_ref, *refs):
    if positioned:
        cos_ref, sin_ref, qkv_ref, cu_ref, h_ref = refs
    else:
        qkv_ref, cu_ref, h_ref = refs
    h = _modulated(x_ref[...], gain_ref[...], mod_ref[0:1, :], mod_ref[1:2, :])
    h_ref[...] = h.astype(BF16)
    lane = lax.broadcasted_iota(jnp.int32, (TM, HEAD_DIM), 1)
    even_quarter = ((lane // (HEAD_DIM // 4)) % 2) == 0
    for pair in range(N_QKV_HEADS // 2):
        c0 = pair * 2 * HEAD_DIM
        y2 = jnp.dot(h_ref[...], w_ref[:, c0:c0 + 2 * HEAD_DIM], preferred_element_type=F32)
        for half in range(2):
            hd = 2 * pair + half
            y = y2[:, half * HEAD_DIM:(half + 1) * HEAD_DIM]
            gain_row, rotary = _head_kind(hd)
            if gain_row is not None:
                ms = jnp.mean(y * y, axis=-1, keepdims=True)
                y = y * lax.rsqrt(ms + NORM_EPS) * hg_ref[gain_row:gain_row + 1, :]
            if rotary and positioned:
                swapped = jnp.where(even_quarter,
                                    pltpu.roll(y, HEAD_DIM - HEAD_DIM // 4, 1),
                                    pltpu.roll(y, HEAD_DIM // 4, 1))
                y = y * cos_ref[...] + swapped * sin_ref[...]
            qkv_ref[hd] = y.astype(BF16)
    for blk in range((IN_COLS - C_COL0) // (2 * HEAD_DIM)):
        c0 = C_COL0 + blk * 2 * HEAD_DIM
        cu_ref[:, blk * 2 * HEAD_DIM:(blk + 1) * 2 * HEAD_DIM] = jnp.dot(
            h_ref[...], w_ref[:, c0:c0 + 2 * HEAD_DIM], preferred_element_type=F32)


def _inproj(xs, is_ctx, mod_l, gain, w_in, layer, head_gains, rope):
    rows = xs.shape[0]
    n_cu = IN_COLS - C_COL0
    blocks = (2 * TM * D_MODEL * 4 + D_MODEL * IN_COLS * 2 + 2 * N_QKV_HEADS * TM * HEAD_DIM * 2
              + 2 * TM * n_cu * 4 + 4 * TM * HEAD_DIM * 4 + TM * D_MODEL * 2)
    group = _group_of_tile(is_ctx, TM)
    operands = [xs, mod_l, gain.reshape(1, D_MODEL), w_in, head_gains]
    in_specs = [
        pl.BlockSpec((TM, D_MODEL), lambda i: (i, 0)),
        pl.BlockSpec((None, None, 3, D_MODEL), lambda i: (group(i), 1, 0, 0)),
        pl.BlockSpec((1, D_MODEL), lambda i: (0, 0)),
        pl.BlockSpec((None, D_MODEL, IN_COLS), lambda i: (layer, 0, 0), pipeline_mode=pl.Buffered(1)),
        pl.BlockSpec((SUBLANES, HEAD_DIM), lambda i: (0, 0)),
    ]
    if not is_ctx:
        operands += list(rope)
        in_specs += [pl.BlockSpec((TM, HEAD_DIM), lambda i: (i % TILES_PER_BATCH, 0))] * 2
    return pl.pallas_call(
        functools.partial(_inproj_kernel, not is_ctx),
        grid=(rows // TM,),
        in_specs=in_specs,
        out_specs=[
            pl.BlockSpec((N_QKV_HEADS, TM, HEAD_DIM), lambda i: (0, i, 0)),
            pl.BlockSpec((TM, n_cu), lambda i: (i, 0)),
        ],
        out_shape=[
            jax.ShapeDtypeStruct((N_QKV_HEADS, rows, HEAD_DIM), BF16),
            jax.ShapeDtypeStruct((rows, n_cu), F32),
        ],
        scratch_shapes=[pltpu.VMEM((TM, D_MODEL), BF16)],
        compiler_params=pltpu.CompilerParams(
            dimension_semantics=("arbitrary",),
            vmem_limit_bytes=_vmem_limit(blocks)),
        name="inproj",
    )(*operands)


def _rope_tables():
    t = np.arange(SEQ)
    n_freq = HEAD_DIM // 4
    inv_freq = ROPE_THETA ** (-np.arange(n_freq, dtype=np.float64) / n_freq)
    ang_r = (t // GRID_W)[:, None] * inv_freq[None, :]
    ang_c = (t % GRID_W)[:, None] * inv_freq[None, :]
    cos_t = np.concatenate([np.cos(ang_r), np.cos(ang_r), np.cos(ang_c), np.cos(ang_c)], axis=-1)
    sin_t = np.concatenate([-np.sin(ang_r), np.sin(ang_r), -np.sin(ang_c), np.sin(ang_c)], axis=-1)
    return jnp.asarray(cos_t, F32), jnp.asarray(sin_t, F32)


LOG2_E = float(np.log2(np.e))
ATTN_TQ = 256
ATTN_BLOCKS = SEQ // ATTN_TQ
ATTN_UNROLL = 16
PLACE_INTERIOR, PLACE_FIRST, PLACE_LAST = 0, 1, 2
N_PLACEMENTS = 3
PLACEMENT_BLOCKS = (1, 0, ATTN_BLOCKS - 1)

A_LEAD = A_WINDOW
A_KEYS = ATTN_TQ + 2 * A_WINDOW
B_ROWS_PER_BLOCK = ATTN_TQ // GRID_W
B_UNION_ROWS = B_ROWS_PER_BLOCK + B_WIN_H
B_LEAD = (B_WIN_H // 2) * GRID_W
B_KEYS = B_UNION_ROWS * GRID_W
assert ATTN_BLOCKS % ATTN_UNROLL == 0 and ATTN_BLOCKS >= 3
assert A_LEAD % LANES == 0 and B_LEAD % LANES == 0 and A_KEYS % LANES == 0 and B_KEYS % LANES == 0


def _dot_nt(a, b):
    return lax.dot_general(a, b, (((1,), (1,)), ((), ())), preferred_element_type=F32)


def _window_start(first_query, lead, n_keys):
    return int(np.clip(first_query - lead, 0, SEQ - n_keys))


def _local_attn_kernel(lead, n_keys, has_sink, *refs):
    if has_sink:
        sink_ref, refs = refs[0], refs[1:]
    q_ref, k_ref, v_ref, kc_ref, vc_ref, bias_ref, o_ref = refs
    sink = sink_ref[pl.program_id(1) * A_GROUP + pl.program_id(2)] if has_sink else None
    kc = kc_ref[...]
    vc_aug = jnp.concatenate([vc_ref[...], jnp.ones((CTX_LEN, HEAD_DIM), BF16)], axis=1)

    def scores(i):
        q0 = pl.multiple_of(i * ATTN_TQ, ATTN_TQ)
        k0 = pl.multiple_of(jnp.clip(q0 - lead, 0, SEQ - n_keys), LANES)
        placement = jnp.where(i == 0, PLACE_FIRST, jnp.where(i == ATTN_BLOCKS - 1, PLACE_LAST, PLACE_INTERIOR))
        q = q_ref[pl.ds(q0, ATTN_TQ), :]
        s = _dot_nt(q, k_ref[pl.ds(k0, n_keys), :]) + bias_ref[placement]
        return q0, k0, s, _dot_nt(q, kc)

    def softmax(q0, k0, s, sc):
        m = jnp.maximum(jnp.max(s, axis=-1, keepdims=True), jnp.max(sc, axis=-1, keepdims=True))
        if has_sink:
            m = jnp.maximum(m, sink)
        p = jnp.exp2(s - m)
        pc = jnp.exp2(sc - m)
        extra = jnp.exp2(sink - m) if has_sink else None
        return q0, k0, p.astype(BF16), pc.astype(BF16), extra

    def output(q0, k0, p, pc, extra):
        v_aug = jnp.concatenate([v_ref[pl.ds(k0, n_keys), :], jnp.ones((n_keys, HEAD_DIM), BF16)], axis=1)
        o_aug = (jnp.dot(p, v_aug, preferred_element_type=F32)
                 + jnp.dot(pc, vc_aug, preferred_element_type=F32))
        den = o_aug[:, HEAD_DIM:HEAD_DIM + 1]
        if has_sink:
            den = den + extra
        o_ref[pl.ds(q0, ATTN_TQ), :] = (o_aug[:, :HEAD_DIM] / den).astype(BF16)

    def body(it, carry):
        scored, weighted = {}, {}
        for step in range(ATTN_UNROLL + 2):
            if step < ATTN_UNROLL:
                scored[step] = scores(it * ATTN_UNROLL + step)
            if 0 <= step - 2 < ATTN_UNROLL:
                output(*weighted.pop(step - 2))
            if 0 <= step - 1 < ATTN_UNROLL:
                weighted[step - 1] = softmax(*scored.pop(step - 1))
        return carry

    lax.fori_loop(0, ATTN_BLOCKS // ATTN_UNROLL, body, 0)


def _local_attn_vmem(n_keys):
    return 2 * 4 * SEQ * HEAD_DIM * 2 + 4 * CTX_LEN * HEAD_DIM * 2 + 2 * N_PLACEMENTS * ATTN_TQ * n_keys * 4


SEQ_BLOCK = (None, SEQ, HEAD_DIM)
CTX_BLOCK = (None, CTX_LEN, HEAD_DIM)


def _a_band_table():
    tables = []
    for blk in PLACEMENT_BLOCKS:
        q0 = blk * ATTN_TQ
        kpos = _window_start(q0, A_LEAD, A_KEYS) + np.arange(A_KEYS)[None, :]
        qpos = q0 + np.arange(ATTN_TQ)[:, None]
        tables.append(np.where(np.abs(kpos - qpos) <= A_WINDOW, 0.0, NEG_INF))
    return jnp.asarray(np.stack(tables), F32)


def _attn_a(qkv, qkv_ctx, sink):
    return pl.pallas_call(
        functools.partial(_local_attn_kernel, A_LEAD, A_KEYS, True),
        grid=(BATCH, A_KV_HEADS, A_GROUP),
        in_specs=[
            pl.BlockSpec(memory_space=pltpu.SMEM),
            pl.BlockSpec(SEQ_BLOCK, lambda b, kv, g: (HEAD_AQ + kv * A_GROUP + g, b, 0)),
            pl.BlockSpec(SEQ_BLOCK, lambda b, kv, g: (HEAD_AK + kv, b, 0)),
            pl.BlockSpec(SEQ_BLOCK, lambda b, kv, g: (HEAD_AV + kv, b, 0)),
            pl.BlockSpec(CTX_BLOCK, lambda b, kv, g: (HEAD_AK + kv, b, 0)),
            pl.BlockSpec(CTX_BLOCK, lambda b, kv, g: (HEAD_AV + kv, b, 0)),
            pl.BlockSpec((N_PLACEMENTS, ATTN_TQ, A_KEYS), lambda b, kv, g: (0, 0, 0)),
        ],
        out_specs=pl.BlockSpec((SEQ, HEAD_DIM), lambda b, kv, g: (b, kv * A_GROUP + g)),
        out_shape=jax.ShapeDtypeStruct((N_LAT, A_Q_HEADS * HEAD_DIM), BF16),
        compiler_params=pltpu.CompilerParams(
            dimension_semantics=("arbitrary", "arbitrary", "arbitrary"),
            vmem_limit_bytes=_vmem_limit(_local_attn_vmem(A_KEYS))),
        name="attn_window",
    )(sink, qkv, qkv, qkv, qkv_ctx, qkv_ctx, _a_band_table())


def _b_bias_tables(rpb):
    j = np.arange(GRID_W)[:, None]
    jk = np.arange(GRID_W)[None, :]
    col_start = np.clip(j - B_WIN_W // 2, 0, GRID_W - B_WIN_W)
    inside = (jk >= col_start) & (jk < col_start + B_WIN_W)
    n_col = 2 * B_WIN_W - 1
    onehot = (inside[:, :, None] & ((jk - j + (B_WIN_W - 1))[:, :, None] == np.arange(n_col))).astype(np.float32)
    by_row = jnp.einsum('lhac,jkc->lhajk', rpb, jnp.asarray(onehot), precision=lax.Precision.HIGHEST)
    by_row = by_row * LOG2_E + jnp.asarray(np.where(inside, 0.0, NEG_INF), F32)
    masked = jnp.full((DEPTH, B_HEADS, GRID_W, GRID_W), NEG_INF, F32)
    query_rows = []
    for blk in PLACEMENT_BLOCKS:
        r0 = blk * B_ROWS_PER_BLOCK
        u0 = _window_start(r0 * GRID_W, B_LEAD, B_KEYS) // GRID_W
        for t in range(B_ROWS_PER_BLOCK):
            r = r0 + t
            rs = int(np.clip(r - B_WIN_H // 2, 0, GRID_ROWS - B_WIN_H))
            before = rs - u0
            assert 0 <= before <= B_UNION_ROWS - B_WIN_H
            bias_row0 = rs - r + (B_WIN_H - 1)
            pieces = [by_row[:, :, bias_row0 + u - before] if before <= u < before + B_WIN_H else masked
                      for u in range(B_UNION_ROWS)]
            query_rows.append(jnp.concatenate(pieces, axis=-1))
    return jnp.stack(query_rows, axis=2).reshape(DEPTH, B_HEADS, N_PLACEMENTS, ATTN_TQ, B_KEYS)


def _attn_b(qkv, qkv_ctx, bias, layer):
    return pl.pallas_call(
        functools.partial(_local_attn_kernel, B_LEAD, B_KEYS, False),
        grid=(BATCH, B_HEADS),
        in_specs=[
            pl.BlockSpec(SEQ_BLOCK, lambda b, h: (HEAD_BQ + h, b, 0)),
            pl.BlockSpec(SEQ_BLOCK, lambda b, h: (HEAD_BK + h, b, 0)),
            pl.BlockSpec(SEQ_BLOCK, lambda b, h: (HEAD_BV + h, b, 0)),
            pl.BlockSpec(CTX_BLOCK, lambda b, h: (HEAD_BK + h, b, 0)),
            pl.BlockSpec(CTX_BLOCK, lambda b, h: (HEAD_BV + h, b, 0)),
            pl.BlockSpec((None, None, N_PLACEMENTS, ATTN_TQ, B_KEYS), lambda b, h: (layer, h, 0, 0, 0)),
        ],
        out_specs=pl.BlockSpec((SEQ, HEAD_DIM), lambda b, h: (b, h)),
        out_shape=jax.ShapeDtypeStruct((N_LAT, B_HEADS * HEAD_DIM), BF16),
        compiler_params=pltpu.CompilerParams(
            dimension_semantics=("arbitrary", "arbitrary"),
            vmem_limit_bytes=_vmem_limit(_local_attn_vmem(B_KEYS))),
        name="attn_neighbourhood",
    )(qkv, qkv, qkv, qkv_ctx, qkv_ctx, bias)


N_CTX_HEADS = A_Q_HEADS + B_HEADS


def _attn_ctx_kernel(sink_ref, q_ref, k_ref, v_ref, o_ref):
    sink = sink_ref[pl.program_id(1)]
    q = q_ref[...]
    s = _dot_nt(q, k_ref[...])
    m = jnp.maximum(jnp.max(s, axis=-1, keepdims=True), sink)
    p = jnp.exp2(s - m)
    den = jnp.sum(p, axis=-1, keepdims=True) + jnp.exp2(sink - m)
    o = jnp.dot(p.astype(BF16), v_ref[...], preferred_element_type=F32)
    o_ref[...] = (o / den).astype(BF16)


def _attn_ctx(qkv_ctx, sinks):
    is_b = lambda h: h >= A_Q_HEADS
    q_head = lambda h: jnp.where(is_b(h), HEAD_BQ + h - A_Q_HEADS, HEAD_AQ + h)
    k_head = lambda h: jnp.where(is_b(h), HEAD_BK + h - A_Q_HEADS, HEAD_AK + h // A_GROUP)
    v_head = lambda h: jnp.where(is_b(h), HEAD_BV + h - A_Q_HEADS, HEAD_AV + h // A_GROUP)
    return pl.pallas_call(
        _attn_ctx_kernel,
        grid=(BATCH, N_CTX_HEADS),
        in_specs=[
            pl.BlockSpec(memory_space=pltpu.SMEM),
            pl.BlockSpec(CTX_BLOCK, lambda b, h: (q_head(h), b, 0)),
            pl.BlockSpec(CTX_BLOCK, lambda b, h: (k_head(h), b, 0)),
            pl.BlockSpec(CTX_BLOCK, lambda b, h: (v_head(h), b, 0)),
        ],
        out_specs=pl.BlockSpec((CTX_LEN, HEAD_DIM), lambda b, h: (b, h)),
        out_shape=jax.ShapeDtypeStruct((N_CTX, N_CTX_HEADS * HEAD_DIM), BF16),
        compiler_params=pltpu.CompilerParams(dimension_semantics=("arbitrary", "arbitrary")),
        name="attn_context",
    )(sinks, qkv_ctx, qkv_ctx, qkv_ctx)


CONV_ROWS = 32
CONV_SH_ROWS = TC + SUBLANES * ((C_CONV_WIDTH - 1) // SUBLANES)
assert SUBLANES - 1 + CONV_HALO - C_PAD + CONV_SH_ROWS <= TC + 2 * CONV_HALO
CONV_CHUNKS = TC // CONV_ROWS


def _glu(u):
    return u[:, :C_CHANNELS] * jax.nn.sigmoid(u[:, C_CHANNELS:])


def _conv_fill(n_rows, first, last, prev_ref, cur_ref, next_ref, ext_ref):
    ext_ref[0:CONV_HALO, :] = jnp.where(first, 0.0, _glu(prev_ref[...]))
    ext_ref[CONV_HALO:CONV_HALO + n_rows, :] = _glu(cur_ref[...])
    ext_ref[CONV_HALO + n_rows:, :] = jnp.where(last, 0.0, _glu(next_ref[...]))


def _conv_shift(row0, ext_ref, sh_ref):
    for b in range(SUBLANES):
        lo = row0 + b + CONV_HALO - C_PAD
        sh_ref[b] = ext_ref[lo:lo + CONV_SH_ROWS, :]


def _conv_chunk(row0, c, sh_ref, w_ref, b_ref, g_ref, beta_ref, out_ref):
    groups = CONV_ROWS // SUBLANES
    r0 = c * CONV_ROWS
    accs = [jnp.zeros((SUBLANES, C_CHANNELS), F32) for _ in range(groups)]
    for k in range(C_CONV_WIDTH):
        a, b = divmod(k, SUBLANES)
        w_k = w_ref[k]
        for g in range(groups):
            lo = r0 + SUBLANES * (a + g)
            accs[g] = accs[g] + sh_ref[b, lo:lo + SUBLANES, :] * w_k
    acc = jnp.concatenate(accs, axis=0) + b_ref[...]
    mu = jnp.mean(acc, axis=-1, keepdims=True)
    xc = acc - mu
    y = xc * lax.rsqrt(jnp.mean(xc * xc, axis=-1, keepdims=True) + NORM_EPS)
    y = y * g_ref[...] + beta_ref[...]
    out_ref[row0 + r0:row0 + r0 + CONV_ROWS, :] = (y * jax.nn.sigmoid(y)).astype(BF16)


def _conv_operands(cu, n_rows, dw_w, dw_b, ln_g, ln_b):
    halo_per_tile = n_rows // CONV_HALO
    n_halo = cu.shape[0] // CONV_HALO
    vec = pl.BlockSpec((1, C_CHANNELS), lambda i: (0, 0))
    specs = [
        pl.BlockSpec((CONV_HALO, 2 * C_CHANNELS), lambda i: (jnp.maximum(i * halo_per_tile - 1, 0), 0)),
        pl.BlockSpec((n_rows, 2 * C_CHANNELS), lambda i: (i, 0)),
        pl.BlockSpec((CONV_HALO, 2 * C_CHANNELS), lambda i: (jnp.minimum((i + 1) * halo_per_tile, n_halo - 1), 0)),
        pl.BlockSpec((C_CONV_WIDTH, SUBLANES, C_CHANNELS), lambda i: (0, 0, 0)),
        vec, vec, vec,
    ]
    operands = [cu, cu, cu, jnp.broadcast_to(dw_w[:, None, :], (C_CONV_WIDTH, SUBLANES, C_CHANNELS)),
                dw_b.reshape(1, -1), ln_g.reshape(1, -1), ln_b.reshape(1, -1)]
    return operands, specs


def _conv_scratch(n_rows):
    return [pltpu.VMEM((n_rows + 2 * CONV_HALO, C_CHANNELS), F32),
            pltpu.VMEM((SUBLANES, CONV_SH_ROWS, C_CHANNELS), F32)]


def _conv_ctx_kernel(prev_ref, cur_ref, next_ref, w_ref, b_ref, g_ref, beta_ref, o_ref, ext_ref, sh_ref):
    _conv_fill(TC, True, True, prev_ref, cur_ref, next_ref, ext_ref)
    _conv_shift(0, ext_ref, sh_ref)
    for c in range(CONV_CHUNKS):
        _conv_chunk(0, c, sh_ref, w_ref, b_ref, g_ref, beta_ref, o_ref)


def _conv_ctx(cu_ctx, dw_w, dw_b, ln_g, ln_b):
    assert CTX_LEN == TC
    operands, specs = _conv_operands(cu_ctx, TC, dw_w, dw_b, ln_g, ln_b)
    return pl.pallas_call(
        _conv_ctx_kernel,
        grid=(N_CTX // TC,),
        in_specs=specs,
        out_specs=pl.BlockSpec((TC, C_CHANNELS), lambda i: (i, 0)),
        out_shape=jax.ShapeDtypeStruct((N_CTX, C_CHANNELS), BF16),
        scratch_shapes=_conv_scratch(TC),
        compiler_params=pltpu.CompilerParams(dimension_semantics=("arbitrary",)),
        name="conv_context",
    )(*operands)


N_CONV_OPERANDS = 7


def _outproj_kernel(widths, fused_conv, x_ref, mod_ref, w_ref, *refs):
    if fused_conv:
        pieces = list(refs[:len(widths) - 1])
        conv_refs = refs[len(widths) - 1:len(widths) - 1 + N_CONV_OPERANDS]
        o_ref, ext_ref, sh_ref, oc_ref = refs[len(widths) - 1 + N_CONV_OPERANDS:]
    else:
        pieces, o_ref = list(refs[:-1]), refs[-1]
    y = None
    k0 = 0
    for piece, width in zip(pieces, widths):
        t = jnp.dot(piece[...], w_ref[k0:k0 + width, :], preferred_element_type=F32)
        y = t if y is None else y + t
        k0 += width
    if fused_conv:
        prev_ref, cur_ref, next_ref, cw_ref, cb_ref, cg_ref, cbeta_ref = conv_refs
        i = pl.program_id(0)
        _conv_fill(TM, i % TILES_PER_BATCH == 0, i % TILES_PER_BATCH == TILES_PER_BATCH - 1,
                   prev_ref, cur_ref, next_ref, ext_ref)
        for row0 in range(0, TM, TC):
            _conv_shift(row0, ext_ref, sh_ref)
            for c in range(CONV_CHUNKS):
                _conv_chunk(row0, c, sh_ref, cw_ref, cb_ref, cg_ref, cbeta_ref, oc_ref)
        y = y + jnp.dot(oc_ref[...], w_ref[k0:, :], preferred_element_type=F32)
    o_ref[...] = x_ref[...] + mod_ref[2:3, :] * y


def _outproj(xs, is_ctx, mod_l, w_out, layer, pieces, conv=None):
    rows = xs.shape[0]
    widths = tuple(int(p.shape[1]) for p in pieces) + ((C_CHANNELS,) if conv is not None else ())
    assert sum(widths) == MIX_WIDTH and all(p.shape[0] == rows for p in pieces)
    blocks = 4 * TM * D_MODEL * 4 + MIX_WIDTH * D_MODEL * 2 + 2 * TM * MIX_WIDTH * 2
    group = _group_of_tile(is_ctx, TM)
    operands = [_in_hbm(xs), mod_l, w_out] + list(pieces)
    in_specs = [
        pl.BlockSpec((TM, D_MODEL), lambda i: (i, 0)),
        pl.BlockSpec((None, None, 3, D_MODEL), lambda i: (group(i), 1, 0, 0)),
        pl.BlockSpec((None, MIX_WIDTH, D_MODEL), lambda i: (layer, 0, 0), pipeline_mode=pl.Buffered(1)),
    ] + [pl.BlockSpec((TM, w), lambda i: (i, 0)) for w in widths[:len(pieces)]]
    scratch = []
    if conv is not None:
        assert not is_ctx and SEQ % TM == 0 and TM % TC == 0 and conv[0].shape[0] == rows
        conv_operands, conv_specs = _conv_operands(conv[0], TM, *conv[1:])
        operands += conv_operands
        in_specs += conv_specs
        scratch = _conv_scratch(TM) + [pltpu.VMEM((TM, C_CHANNELS), BF16)]
        blocks += 4 * TM * C_CHANNELS * 4 + (TM + SUBLANES * CONV_SH_ROWS) * C_CHANNELS * 4
    return pl.pallas_call(
        functools.partial(_outproj_kernel, widths, conv is not None),
        grid=(rows // TM,),
        in_specs=in_specs,
        out_specs=pl.BlockSpec((TM, D_MODEL), lambda i: (i, 0)),
        out_shape=jax.ShapeDtypeStruct((rows, D_MODEL), F32),
        scratch_shapes=scratch,
        input_output_aliases={0: 0},
        compiler_params=pltpu.CompilerParams(
            dimension_semantics=("arbitrary",),
            vmem_limit_bytes=_vmem_limit(blocks)),
        name="outproj",
    )(*operands)


def kernel(x, c, ctx, c_ctx, w_mod, b_mod, norm_ffn1, norm_mix, norm_ffn2, ffn1_w_gate, ffn1_w_up,
           ffn1_w_down, ffn2_w_gate, ffn2_w_up, ffn2_w_down, w_in, w_out, a_q_norm, a_k_norm, a_sink,
           b_q_norm, b_k_norm, b_rpb, c_dw_w, c_dw_b, c_ln_g, c_ln_b):
    mods = _mod_vectors(c, c_ctx, w_mod, b_mod)
    rope = _rope_tables()
    b_bias = _b_bias_tables(b_rpb)
    ffn1_w = [w.astype(BF16) for w in (ffn1_w_gate, ffn1_w_up, ffn1_w_down)]
    ffn2_w = [w.astype(BF16) for w in (ffn2_w_gate, ffn2_w_up, ffn2_w_down)]
    w_in_bf = w_in.astype(BF16)
    w_out_bf = w_out.astype(BF16)
    q_scale = HEAD_DIM ** -0.5 * LOG2_E
    xs = x.reshape(N_LAT, D_MODEL)
    cs = ctx.reshape(N_CTX, D_MODEL)
    for l in range(DEPTH):
        last = l == DEPTH - 1
        mod_l = mods[l]
        owned = l > 0
        xs = _ffn(xs, False, mod_l, 0, norm_ffn1[l], *ffn1_w, l, in_place=owned)
        cs = _ffn(cs, True, mod_l, 0, norm_ffn1[l], *ffn1_w, l, in_place=owned)
        head_gains = jnp.concatenate([
            (a_q_norm[l] * q_scale)[None], a_k_norm[l][None], (b_q_norm[l] * q_scale)[None],
            b_k_norm[l][None], jnp.zeros((SUBLANES - N_HEAD_GAINS, HEAD_DIM), F32)], axis=0)
        qkv, cu = _inproj(xs, False, mod_l, norm_mix[l], w_in_bf, l, head_gains, rope)
        qkv_ctx, cu_ctx = _inproj(cs, True, mod_l, norm_mix[l], w_in_bf, l, head_gains, rope)
        sink_l = a_sink[l] * LOG2_E
        o_a = _attn_a(qkv, qkv_ctx, sink_l)
        o_b = _attn_b(qkv, qkv_ctx, b_bias, l)
        conv_params = (c_dw_w[l], c_dw_b[l], c_ln_g[l], c_ln_b[l])
        xs = _outproj(xs, False, mod_l, w_out_bf, l, [o_a, o_b], conv=(cu,) + conv_params)
        xs = _ffn(xs, False, mod_l, 2, norm_ffn2[l], *ffn2_w, l, in_place=True)
        if not last:
            sinks = jnp.concatenate([sink_l, jnp.full((B_HEADS,), NEG_INF, F32)])
            o_ctx = _attn_ctx(qkv_ctx, sinks)
            o_c_ctx = _conv_ctx(cu_ctx, *conv_params)
            cs = _outproj(cs, True, mod_l, w_out_bf, l, [o_ctx, o_c_ctx])
            cs = _ffn(cs, True, mod_l, 2, norm_ffn2[l], *ffn2_w, l, in_place=True)
    return xs.reshape(BATCH, SEQ, D_MODEL)
```

```python
import functools

import numpy as np
import jax
import jax.numpy as jnp
from jax import lax
from jax.experimental import pallas as pl
from jax.experimental.pallas import tpu as pltpu

D_MODEL = 2048
BATCH = 2
SEQ = 16384
DEPTH = 2
GRID_W = 64
GRID_ROWS = SEQ // GRID_W
CTX_LEN = 256
HEAD_DIM = 128
A_Q_HEADS = 6
A_KV_HEADS = 2
A_GROUP = A_Q_HEADS // A_KV_HEADS
A_WINDOW = 128
B_HEADS = 6
B_WIN_H = 8
B_WIN_W = 16
C_CHANNELS = 512
C_CONV_WIDTH = 31
C_PAD = (C_CONV_WIDTH - 1) // 2
D_FF = 5632
ROPE_THETA = 10000.0
NORM_EPS = 1e-6
NEG_INF = -1e30
N_MOD = 9
IN_COLS = 4608
MIX_WIDTH = 2048

HEAD_AQ = 0
HEAD_AK = HEAD_AQ + A_Q_HEADS
HEAD_AV = HEAD_AK + A_KV_HEADS
HEAD_BQ = HEAD_AV + A_KV_HEADS
HEAD_BK = HEAD_BQ + B_HEADS
HEAD_BV = HEAD_BK + B_HEADS
N_QKV_HEADS = HEAD_BV + B_HEADS
C_COL0 = N_QKV_HEADS * HEAD_DIM

N_LAT = BATCH * SEQ
N_CTX = BATCH * CTX_LEN

V7X_VMEM_BYTES = 64 * 1024 * 1024
SUBLANES = 8
LANES = 128

TM = 512
TILES_PER_BATCH = SEQ // TM
TF = 512
TC = 256
CONV_HALO = 16

F32 = jnp.float32
BF16 = jnp.bfloat16


MIB = 1024 * 1024
VMEM_TEMPORARIES_BYTES = 12 * MIB
VMEM_UNREQUESTED_BYTES = 4 * MIB


def _vmem_limit(block_bytes):
    return int(min(V7X_VMEM_BYTES - VMEM_UNREQUESTED_BYTES, block_bytes + VMEM_TEMPORARIES_BYTES))


def _group_of_tile(is_ctx, tile_rows):
    return (lambda i: BATCH) if is_ctx else (lambda i: i * tile_rows // SEQ)


def _in_hbm(stream):
    return pltpu.with_memory_space_constraint(stream, pltpu.HBM)


def _modulated(x, gain, shift, scale):
    ms = jnp.mean(x * x, axis=-1, keepdims=True)
    return (x * lax.rsqrt(ms + NORM_EPS) * gain) * (1.0 + scale) + shift


MOD_TN = 1024
MOD_GROUPS = BATCH + 1
MOD_UNROLL = 8


def _mod_kernel(c_ref, w_ref, b_ref, o_ref, a_ref):
    @pl.when(jnp.logical_and(pl.program_id(0) == 0, pl.program_id(1) == 0))
    def _():
        c = c_ref[...]
        a_ref[...] = c * jax.nn.sigmoid(c)

    lane_tiles = MOD_TN // LANES

    def body(step, accs):
        accs = list(accs)
        for u in range(MOD_UNROLL):
            r = pl.multiple_of((step * MOD_UNROLL + u) * SUBLANES, SUBLANES)
            for m in range(MOD_GROUPS):
                a = a_ref[m, pl.ds(r, SUBLANES), :]
                for t in range(lane_tiles):
                    idx = m * lane_tiles + t
                    accs[idx] = accs[idx] + w_ref[pl.ds(r, SUBLANES), t * LANES:(t + 1) * LANES] * a
        return tuple(accs)

    zero = jnp.zeros((SUBLANES, LANES), F32)
    accs = lax.fori_loop(0, D_MODEL // (SUBLANES * MOD_UNROLL), body, (zero,) * (MOD_GROUPS * lane_tiles))
    o_ref[...] = jnp.zeros_like(o_ref)
    for m in range(MOD_GROUPS):
        row = jnp.concatenate([jnp.sum(accs[m * lane_tiles + t], axis=0, keepdims=True)
                               for t in range(lane_tiles)], axis=1)
        o_ref[m:m + 1, :] = row + b_ref[...]


def _mod_vectors(c, c_ctx, w_mod, b_mod):
    rows = jnp.concatenate([c, c_ctx[None, :]], axis=0)
    c_cols = jnp.broadcast_to(rows[:, :, None], (MOD_GROUPS, D_MODEL, LANES))
    n = N_MOD * D_MODEL
    out = pl.pallas_call(
        _mod_kernel,
        grid=(DEPTH, n // MOD_TN),
        in_specs=[
            pl.BlockSpec((MOD_GROUPS, D_MODEL, LANES), lambda l, j: (0, 0, 0)),
            pl.BlockSpec((None, D_MODEL, MOD_TN), lambda l, j: (l, 0, j)),
            pl.BlockSpec((None, 1, MOD_TN), lambda l, j: (l, 0, j)),
        ],
        out_specs=pl.BlockSpec((None, SUBLANES, MOD_TN), lambda l, j: (l, 0, j)),
        out_shape=jax.ShapeDtypeStruct((DEPTH, SUBLANES, n), F32),
        scratch_shapes=[pltpu.VMEM((MOD_GROUPS, D_MODEL, LANES), F32)],
        compiler_params=pltpu.CompilerParams(
            dimension_semantics=("arbitrary", "arbitrary"),
            vmem_limit_bytes=_vmem_limit(2 * D_MODEL * MOD_TN * 4 + 3 * MOD_GROUPS * D_MODEL * LANES * 4)),
        name="mod_vectors",
    )(c_cols, w_mod, b_mod.reshape(DEPTH, 1, n))
    return out[:, :MOD_GROUPS].reshape(DEPTH, MOD_GROUPS, 3, 3, D_MODEL)


FFN_SUB = 512
FFN_TM_LAT = 1024


def _ffn_kernel(tm, x_ref, mod_ref, gain_ref, wg_ref, wu_ref, wd_ref, o_ref, h_ref):
    j = pl.program_id(1)
    last = pl.num_programs(1) - 1

    def chunk(first, final):
        for r0 in range(0, tm, FFN_SUB):
            rows = slice(r0, r0 + FFN_SUB)
            if first:
                h = _modulated(x_ref[rows, :], gain_ref[...], mod_ref[0:1, :], mod_ref[1:2, :])
                h_ref[rows, :] = h.astype(BF16)
            h = h_ref[rows, :]
            g = jnp.dot(h, wg_ref[...], preferred_element_type=F32)
            u = jnp.dot(h, wu_ref[...], preferred_element_type=F32)
            a = (g * jax.nn.sigmoid(g)) * u
            d = jnp.dot(a.astype(BF16), wd_ref[...], preferred_element_type=F32)
            if first:
                o_ref[rows, :] = d
            elif final:
                o_ref[rows, :] = x_ref[rows, :] + (0.5 * mod_ref[2:3, :]) * (o_ref[rows, :] + d)
            else:
                o_ref[rows, :] += d

    pl.when(j == 0)(functools.partial(chunk, True, False))
    pl.when(jnp.logical_and(j > 0, j < last))(functools.partial(chunk, False, False))
    pl.when(j == last)(functools.partial(chunk, False, True))


def _ffn(x, is_ctx, mod_l, sub, gain, wg, wu, wd, layer, in_place):
    tm = TM if is_ctx else FFN_TM_LAT
    rows = x.shape[0]
    assert tm % FFN_SUB == 0 and rows % tm == 0 and SEQ % tm == 0 and D_FF // TF >= 2
    blocks = 4 * tm * D_MODEL * 4 + 6 * D_MODEL * TF * 2 + tm * D_MODEL * 2
    group = _group_of_tile(is_ctx, tm)
    return pl.pallas_call(
        functools.partial(_ffn_kernel, tm),
        grid=(rows // tm, D_FF // TF),
        in_specs=[
            pl.BlockSpec((tm, D_MODEL), lambda i, j: (i, 0)),
            pl.BlockSpec((None, None, 3, D_MODEL), lambda i, j: (group(i), sub, 0, 0)),
            pl.BlockSpec((1, D_MODEL), lambda i, j: (0, 0)),
            pl.BlockSpec((None, D_MODEL, TF), lambda i, j: (layer, 0, j)),
            pl.BlockSpec((None, D_MODEL, TF), lambda i, j: (layer, 0, j)),
            pl.BlockSpec((None, TF, D_MODEL), lambda i, j: (layer, j, 0)),
        ],
        out_specs=pl.BlockSpec((tm, D_MODEL), lambda i, j: (i, 0)),
        out_shape=jax.ShapeDtypeStruct((rows, D_MODEL), F32),
        scratch_shapes=[pltpu.VMEM((tm, D_MODEL), BF16)],
        input_output_aliases={0: 0} if in_place else {},
        compiler_params=pltpu.CompilerParams(
            dimension_semantics=("arbitrary", "arbitrary"),
            vmem_limit_bytes=_vmem_limit(blocks)),
        name="ffn",
    )(_in_hbm(x), mod_l, gain.reshape(1, D_MODEL), wg, wu, wd)


GAIN_AQ, GAIN_AK, GAIN_BQ, GAIN_BK = 0, 1, 2, 3
N_HEAD_GAINS = 4


def _head_kind(h):
    if h < HEAD_AK:
        return GAIN_AQ, True
    if h < HEAD_AV:
        return GAIN_AK, True
    if h < HEAD_BQ:
        return None, False
    if h < HEAD_BK:
        return GAIN_BQ, False
    if h < HEAD_BV:
        return GAIN_BK, False
    return None, False


def _inproj_kernel(positioned, x_ref, mod_ref, gain_ref, w_ref, hg_ref, *refs):
    if positioned:
        cos_ref, sin_ref, qkv_ref, cu_ref, h_ref = refs
    else:
        qkv_ref, cu_ref, h_ref = refs
    h = _modulated(x_ref[...], gain_ref[...], mod_ref[0:1, :], mod_ref[1:2, :])
    h_ref[...] = h.astype(BF16)
    lane = lax.broadcasted_iota(jnp.int32, (TM, HEAD_DIM), 1)
    even_quarter = ((lane // (HEAD_DIM // 4)) % 2) == 0
    for pair in range(N_QKV_HEADS // 2):
        c0 = pair * 2 * HEAD_DIM
        y2 = jnp.dot(h_ref[...], w_ref[:, c0:c0 + 2 * HEAD_DIM], preferred_element_type=F32)
        for half in range(2):
            hd = 2 * pair + half
            y = y2[:, half * HEAD_DIM:(half + 1) * HEAD_DIM]
            gain_row, rotary = _head_kind(hd)
            if gain_row is not None:
                ms = jnp.mean(y * y, axis=-1, keepdims=True)
                y = y * lax.rsqrt(ms + NORM_EPS) * hg_ref[gain_row:gain_row + 1, :]
            if rotary and positioned:
                swapped = jnp.where(even_quarter,
                                    pltpu.roll(y, HEAD_DIM - HEAD_DIM // 4, 1),
                                    pltpu.roll(y, HEAD_DIM // 4, 1))
                y = y * cos_ref[...] + swapped * sin_ref[...]
            qkv_ref[hd] = y.astype(BF16)
    for blk in range((IN_COLS - C_COL0) // (2 * HEAD_DIM)):
        c0 = C_COL0 + blk * 2 * HEAD_DIM
        cu_ref[:, blk * 2 * HEAD_DIM:(blk + 1) * 2 * HEAD_DIM] = jnp.dot(
            h_ref[...], w_ref[:, c0:c0 + 2 * HEAD_DIM], preferred_element_type=F32)


def _inproj(xs, is_ctx, mod_l, gain, w_in, layer, head_gains, rope):
    rows = xs.shape[0]
    n_cu = IN_COLS - C_COL0
    blocks = (2 * TM * D_MODEL * 4 + D_MODEL * IN_COLS * 2 + 2 * N_QKV_HEADS * TM * HEAD_DIM * 2
              + 2 * TM * n_cu * 4 + 4 * TM * HEAD_DIM * 4 + TM * D_MODEL * 2)
    group = _group_of_tile(is_ctx, TM)
    operands = [xs, mod_l, gain.reshape(1, D_MODEL), w_in, head_gains]
    in_specs = [
        pl.BlockSpec((TM, D_MODEL), lambda i: (i, 0)),
        pl.BlockSpec((None, None, 3, D_MODEL), lambda i: (group(i), 1, 0, 0)),
        pl.BlockSpec((1, D_MODEL), lambda i: (0, 0)),
        pl.BlockSpec((None, D_MODEL, IN_COLS), lambda i: (layer, 0, 0), pipeline_mode=pl.Buffered(1)),
        pl.BlockSpec((SUBLANES, HEAD_DIM), lambda i: (0, 0)),
    ]
    if not is_ctx:
        operands += list(rope)
        in_specs += [pl.BlockSpec((TM, HEAD_DIM), lambda i: (i % TILES_PER_BATCH, 0))] * 2
    return pl.pallas_call(
        functools.partial(_inproj_kernel, not is_ctx),
        grid=(rows // TM,),
        in_specs=in_specs,
        out_specs=[
            pl.BlockSpec((N_QKV_HEADS, TM, HEAD_DIM), lambda i: (0, i, 0)),
            pl.BlockSpec((TM, n_cu), lambda i: (i, 0)),
        ],
        out_shape=[
            jax.ShapeDtypeStruct((N_QKV_HEADS, rows, HEAD_DIM), BF16),
            jax.ShapeDtypeStruct((rows, n_cu), F32),
        ],
        scratch_shapes=[pltpu.VMEM((TM, D_MODEL), BF16)],
        compiler_params=pltpu.CompilerParams(
            dimension_semantics=("arbitrary",),
            vmem_limit_bytes=_vmem_limit(blocks)),
        name="inproj",
    )(*operands)


def _rope_tables():
    t = np.arange(SEQ)
    n_freq = HEAD_DIM // 4
    inv_freq = ROPE_THETA ** (-np.arange(n_freq, dtype=np.float64) / n_freq)
    ang_r = (t // GRID_W)[:, None] * inv_freq[None, :]
    ang_c = (t % GRID_W)[:, None] * inv_freq[None, :]
    cos_t = np.concatenate([np.cos(ang_r), np.cos(ang_r), np.cos(ang_c), np.cos(ang_c)], axis=-1)
    sin_t = np.concatenate([-np.sin(ang_r), np.sin(ang_r), -np.sin(ang_c), np.sin(ang_c)], axis=-1)
    return jnp.asarray(cos_t, F32), jnp.asarray(sin_t, F32)


LOG2_E = float(np.log2(np.e))
ATTN_TQ = 256
ATTN_BLOCKS = SEQ // ATTN_TQ
ATTN_UNROLL = 16
PLACE_INTERIOR, PLACE_FIRST, PLACE_LAST = 0, 1, 2
N_PLACEMENTS = 3
PLACEMENT_BLOCKS = (1, 0, ATTN_BLOCKS - 1)

A_LEAD = A_WINDOW
A_KEYS = ATTN_TQ + 2 * A_WINDOW
B_ROWS_PER_BLOCK = ATTN_TQ // GRID_W
B_UNION_ROWS = B_ROWS_PER_BLOCK + B_WIN_H
B_LEAD = (B_WIN_H // 2) * GRID_W
B_KEYS = B_UNION_ROWS * GRID_W
assert ATTN_BLOCKS % ATTN_UNROLL == 0 and ATTN_BLOCKS >= 3
assert A_LEAD % LANES == 0 and B_LEAD % LANES == 0 and A_KEYS % LANES == 0 and B_KEYS % LANES == 0


def _dot_nt(a, b):
    return lax.dot_general(a, b, (((1,), (1,)), ((), ())), preferred_element_type=F32)


def _window_start(first_query, lead, n_keys):
    return int(np.clip(first_query - lead, 0, SEQ - n_keys))


def _local_attn_kernel(lead, n_keys, has_sink, *refs):
    if has_sink:
        sink_ref, refs = refs[0], refs[1:]
    q_ref, k_ref, v_ref, kc_ref, vc_ref, bias_ref, o_ref = refs
    sink = sink_ref[pl.program_id(1) * A_GROUP + pl.program_id(2)] if has_sink else None
    kc = kc_ref[...]
    mxu_sums = not has_sink
    vc = vc_ref[...]
    if mxu_sums:
        vc = jnp.concatenate([vc, jnp.ones((CTX_LEN, HEAD_DIM), BF16)], axis=1)

    def scores(i):
        q0 = pl.multiple_of(i * ATTN_TQ, ATTN_TQ)
        k0 = pl.multiple_of(jnp.clip(q0 - lead, 0, SEQ - n_keys), LANES)
        placement = jnp.where(i == 0, PLACE_FIRST, jnp.where(i == ATTN_BLOCKS - 1, PLACE_LAST, PLACE_INTERIOR))
        q = q_ref[pl.ds(q0, ATTN_TQ), :]
        s = _dot_nt(q, k_ref[pl.ds(k0, n_keys), :]) + bias_ref[placement]
        return q0, k0, s, _dot_nt(q, kc)

    def softmax(q0, k0, s, sc):
        m = jnp.maximum(jnp.max(s, axis=-1, keepdims=True), jnp.max(sc, axis=-1, keepdims=True))
        if has_sink:
            m = jnp.maximum(m, sink)
        p = jnp.exp2(s - m)
        pc = jnp.exp2(sc - m)
        if mxu_sums:
            den = None
        else:
            den = jnp.sum(p, axis=-1, keepdims=True) + jnp.sum(pc, axis=-1, keepdims=True)
            if has_sink:
                den = den + jnp.exp2(sink - m)
        return q0, k0, p.astype(BF16), pc.astype(BF16), den

    def output(q0, k0, p, pc, den):
        v = v_ref[pl.ds(k0, n_keys), :]
        if mxu_sums:
            v = jnp.concatenate([v, jnp.ones((n_keys, HEAD_DIM), BF16)], axis=1)
        o = jnp.dot(p, v, preferred_element_type=F32) + jnp.dot(pc, vc, preferred_element_type=F32)
        if mxu_sums:
            o, den = o[:, :HEAD_DIM], o[:, HEAD_DIM:HEAD_DIM + 1]
        o_ref[pl.ds(q0, ATTN_TQ), :] = (o / den).astype(BF16)

    def body(it, carry):
        scored, weighted = {}, {}
        for step in range(ATTN_UNROLL + 2):
            if step < ATTN_UNROLL:
                scored[step] = scores(it * ATTN_UNROLL + step)
            if 0 <= step - 2 < ATTN_UNROLL:
                output(*weighted.pop(step - 2))
            if 0 <= step - 1 < ATTN_UNROLL:
                weighted[step - 1] = softmax(*scored.pop(step - 1))
        return carry

    lax.fori_loop(0, ATTN_BLOCKS // ATTN_UNROLL, body, 0)


def _local_attn_vmem(n_keys):
    return 2 * 4 * SEQ * HEAD_DIM * 2 + 4 * CTX_LEN * HEAD_DIM * 2 + 2 * N_PLACEMENTS * ATTN_TQ * n_keys * 4


SEQ_BLOCK = (None, SEQ, HEAD_DIM)
CTX_BLOCK = (None, CTX_LEN, HEAD_DIM)


def _a_band_table():
    tables = []
    for blk in PLACEMENT_BLOCKS:
        q0 = blk * ATTN_TQ
        kpos = _window_start(q0, A_LEAD, A_KEYS) + np.arange(A_KEYS)[None, :]
        qpos = q0 + np.arange(ATTN_TQ)[:, None]
        tables.append(np.where(np.abs(kpos - qpos) <= A_WINDOW, 0.0, NEG_INF))
    return jnp.asarray(np.stack(tables), F32)


def _attn_a(qkv, qkv_ctx, sink):
    return pl.pallas_call(
        functools.partial(_local_attn_kernel, A_LEAD, A_KEYS, True),
        grid=(BATCH, A_KV_HEADS, A_GROUP),
        in_specs=[
            pl.BlockSpec(memory_space=pltpu.SMEM),
            pl.BlockSpec(SEQ_BLOCK, lambda b, kv, g: (HEAD_AQ + kv * A_GROUP + g, b, 0)),
            pl.BlockSpec(SEQ_BLOCK, lambda b, kv, g: (HEAD_AK + kv, b, 0)),
            pl.BlockSpec(SEQ_BLOCK, lambda b, kv, g: (HEAD_AV + kv, b, 0)),
            pl.BlockSpec(CTX_BLOCK, lambda b, kv, g: (HEAD_AK + kv, b, 0)),
            pl.BlockSpec(CTX_BLOCK, lambda b, kv, g: (HEAD_AV + kv, b, 0)),
            pl.BlockSpec((N_PLACEMENTS, ATTN_TQ, A_KEYS), lambda b, kv, g: (0, 0, 0)),
        ],
        out_specs=pl.BlockSpec((SEQ, HEAD_DIM), lambda b, kv, g: (b, kv * A_GROUP + g)),
        out_shape=jax.ShapeDtypeStruct((N_LAT, A_Q_HEADS * HEAD_DIM), BF16),
        compiler_params=pltpu.CompilerParams(
            dimension_semantics=("arbitrary", "arbitrary", "arbitrary"),
            vmem_limit_bytes=_vmem_limit(_local_attn_vmem(A_KEYS))),
        name="attn_window",
    )(sink, qkv, qkv, qkv, qkv_ctx, qkv_ctx, _a_band_table())


def _b_bias_tables(rpb):
    j = np.arange(GRID_W)[:, None]
    jk = np.arange(GRID_W)[None, :]
    col_start = np.clip(j - B_WIN_W // 2, 0, GRID_W - B_WIN_W)
    inside = (jk >= col_start) & (jk < col_start + B_WIN_W)
    n_col = 2 * B_WIN_W - 1
    onehot = (inside[:, :, None] & ((jk - j + (B_WIN_W - 1))[:, :, None] == np.arange(n_col))).astype(np.float32)
    by_row = jnp.einsum('lhac,jkc->lhajk', rpb, jnp.asarray(onehot), precision=lax.Precision.HIGHEST)
    by_row = by_row * LOG2_E + jnp.asarray(np.where(inside, 0.0, NEG_INF), F32)
    masked = jnp.full((DEPTH, B_HEADS, GRID_W, GRID_W), NEG_INF, F32)
    query_rows = []
    for blk in PLACEMENT_BLOCKS:
        r0 = blk * B_ROWS_PER_BLOCK
        u0 = _window_start(r0 * GRID_W, B_LEAD, B_KEYS) // GRID_W
        for t in range(B_ROWS_PER_BLOCK):
            r = r0 + t
            rs = int(np.clip(r - B_WIN_H // 2, 0, GRID_ROWS - B_WIN_H))
            before = rs - u0
            assert 0 <= before <= B_UNION_ROWS - B_WIN_H
            bias_row0 = rs - r + (B_WIN_H - 1)
            pieces = [by_row[:, :, bias_row0 + u - before] if before <= u < before + B_WIN_H else masked
                      for u in range(B_UNION_ROWS)]
            query_rows.append(jnp.concatenate(pieces, axis=-1))
    return jnp.stack(query_rows, axis=2).reshape(DEPTH, B_HEADS, N_PLACEMENTS, ATTN_TQ, B_KEYS)


def _attn_b(qkv, qkv_ctx, bias, layer):
    return pl.pallas_call(
        functools.partial(_local_attn_kernel, B_LEAD, B_KEYS, False),
        grid=(BATCH, B_HEADS),
        in_specs=[
            pl.BlockSpec(SEQ_BLOCK, lambda b, h: (HEAD_BQ + h, b, 0)),
            pl.BlockSpec(SEQ_BLOCK, lambda b, h: (HEAD_BK + h, b, 0)),
            pl.BlockSpec(SEQ_BLOCK, lambda b, h: (HEAD_BV + h, b, 0)),
            pl.BlockSpec(CTX_BLOCK, lambda b, h: (HEAD_BK + h, b, 0)),
            pl.BlockSpec(CTX_BLOCK, lambda b, h: (HEAD_BV + h, b, 0)),
            pl.BlockSpec((None, None, N_PLACEMENTS, ATTN_TQ, B_KEYS), lambda b, h: (layer, h, 0, 0, 0)),
        ],
        out_specs=pl.BlockSpec((SEQ, HEAD_DIM), lambda b, h: (b, h)),
        out_shape=jax.ShapeDtypeStruct((N_LAT, B_HEADS * HEAD_DIM), BF16),
        compiler_params=pltpu.CompilerParams(
            dimension_semantics=("arbitrary", "arbitrary"),
            vmem_limit_bytes=_vmem_limit(_local_attn_vmem(B_KEYS))),
        name="attn_neighbourhood",
    )(qkv, qkv, qkv, qkv_ctx, qkv_ctx, bias)


N_CTX_HEADS = A_Q_HEADS + B_HEADS


def _attn_ctx_kernel(sink_ref, q_ref, k_ref, v_ref, o_ref):
    sink = sink_ref[pl.program_id(1)]
    q = q_ref[...]
    s = _dot_nt(q, k_ref[...])
    m = jnp.maximum(jnp.max(s, axis=-1, keepdims=True), sink)
    p = jnp.exp2(s - m)
    den = jnp.sum(p, axis=-1, keepdims=True) + jnp.exp2(sink - m)
    o = jnp.dot(p.astype(BF16), v_ref[...], preferred_element_type=F32)
    o_ref[...] = (o / den).astype(BF16)


def _attn_ctx(qkv_ctx, sinks):
    is_b = lambda h: h >= A_Q_HEADS
    q_head = lambda h: jnp.where(is_b(h), HEAD_BQ + h - A_Q_HEADS, HEAD_AQ + h)
    k_head = lambda h: jnp.where(is_b(h), HEAD_BK + h - A_Q_HEADS, HEAD_AK + h // A_GROUP)
    v_head = lambda h: jnp.where(is_b(h), HEAD_BV + h - A_Q_HEADS, HEAD_AV + h // A_GROUP)
    return pl.pallas_call(
        _attn_ctx_kernel,
        grid=(BATCH, N_CTX_HEADS),
        in_specs=[
            pl.BlockSpec(memory_space=pltpu.SMEM),
            pl.BlockSpec(CTX_BLOCK, lambda b, h: (q_head(h), b, 0)),
            pl.BlockSpec(CTX_BLOCK, lambda b, h: (k_head(h), b, 0)),
            pl.BlockSpec(CTX_BLOCK, lambda b, h: (v_head(h), b, 0)),
        ],
        out_specs=pl.BlockSpec((CTX_LEN, HEAD_DIM), lambda b, h: (b, h)),
        out_shape=jax.ShapeDtypeStruct((N_CTX, N_CTX_HEADS * HEAD_DIM), BF16),
        compiler_params=pltpu.CompilerParams(dimension_semantics=("arbitrary", "arbitrary")),
        name="attn_context",
    )(sinks, qkv_ctx, qkv_ctx, qkv_ctx)


CONV_ROWS = 32
CONV_SH_ROWS = TC + SUBLANES * ((C_CONV_WIDTH - 1) // SUBLANES)
assert SUBLANES - 1 + CONV_HALO - C_PAD + CONV_SH_ROWS <= TC + 2 * CONV_HALO
CONV_CHUNKS = TC // CONV_ROWS


def _glu(u):
    return u[:, :C_CHANNELS] * jax.nn.sigmoid(u[:, C_CHANNELS:])


def _conv_fill(n_rows, first, last, prev_ref, cur_ref, next_ref, ext_ref):
    ext_ref[0:CONV_HALO, :] = jnp.where(first, 0.0, _glu(prev_ref[...]))
    ext_ref[CONV_HALO:CONV_HALO + n_rows, :] = _glu(cur_ref[...])
    ext_ref[CONV_HALO + n_rows:, :] = jnp.where(last, 0.0, _glu(next_ref[...]))


def _conv_shift(row0, ext_ref, sh_ref):
    for b in range(SUBLANES):
        lo = row0 + b + CONV_HALO - C_PAD
        sh_ref[b] = ext_ref[lo:lo + CONV_SH_ROWS, :]


def _conv_chunk(row0, c, sh_ref, w_ref, b_ref, g_ref, beta_ref, out_ref):
    groups = CONV_ROWS // SUBLANES
    r0 = c * CONV_ROWS
    accs = [jnp.zeros((SUBLANES, C_CHANNELS), F32) for _ in range(groups)]
    for k in range(C_CONV_WIDTH):
        a, b = divmod(k, SUBLANES)
        w_k = w_ref[k]
        for g in range(groups):
            lo = r0 + SUBLANES * (a + g)
            accs[g] = accs[g] + sh_ref[b, lo:lo + SUBLANES, :] * w_k
    acc = jnp.concatenate(accs, axis=0) + b_ref[...]
    mu = jnp.mean(acc, axis=-1, keepdims=True)
    xc = acc - mu
    y = xc * lax.rsqrt(jnp.mean(xc * xc, axis=-1, keepdims=True) + NORM_EPS)
    y = y * g_ref[...] + beta_ref[...]
    out_ref[row0 + r0:row0 + r0 + CONV_ROWS, :] = (y * jax.nn.sigmoid(y)).astype(BF16)


def _conv_operands(cu, n_rows, dw_w, dw_b, ln_g, ln_b):
    halo_per_tile = n_rows // CONV_HALO
    n_halo = cu.shape[0] // CONV_HALO
    vec = pl.BlockSpec((1, C_CHANNELS), lambda i: (0, 0))
    specs = [
        pl.BlockSpec((CONV_HALO, 2 * C_CHANNELS), lambda i: (jnp.maximum(i * halo_per_tile - 1, 0), 0)),
        pl.BlockSpec((n_rows, 2 * C_CHANNELS), lambda i: (i, 0)),
        pl.BlockSpec((CONV_HALO, 2 * C_CHANNELS), lambda i: (jnp.minimum((i + 1) * halo_per_tile, n_halo - 1), 0)),
        pl.BlockSpec((C_CONV_WIDTH, SUBLANES, C_CHANNELS), lambda i: (0, 0, 0)),
        vec, vec, vec,
    ]
    operands = [cu, cu, cu, jnp.broadcast_to(dw_w[:, None, :], (C_CONV_WIDTH, SUBLANES, C_CHANNELS)),
                dw_b.reshape(1, -1), ln_g.reshape(1, -1), ln_b.reshape(1, -1)]
    return operands, specs


def _conv_scratch(n_rows):
    return [pltpu.VMEM((n_rows + 2 * CONV_HALO, C_CHANNELS), F32),
            pltpu.VMEM((SUBLANES, CONV_SH_ROWS, C_CHANNELS), F32)]


def _conv_ctx_kernel(prev_ref, cur_ref, next_ref, w_ref, b_ref, g_ref, beta_ref, o_ref, ext_ref, sh_ref):
    _conv_fill(TC, True, True, prev_ref, cur_ref, next_ref, ext_ref)
    _conv_shift(0, ext_ref, sh_ref)
    for c in range(CONV_CHUNKS):
        _conv_chunk(0, c, sh_ref, w_ref, b_ref, g_ref, beta_ref, o_ref)


def _conv_ctx(cu_ctx, dw_w, dw_b, ln_g, ln_b):
    assert CTX_LEN == TC
    operands, specs = _conv_operands(cu_ctx, TC, dw_w, dw_b, ln_g, ln_b)
    return pl.pallas_call(
        _conv_ctx_kernel,
        grid=(N_CTX // TC,),
        in_specs=specs,
        out_specs=pl.BlockSpec((TC, C_CHANNELS), lambda i: (i, 0)),
        out_shape=jax.ShapeDtypeStruct((N_CTX, C_CHANNELS), BF16),
        scratch_shapes=_conv_scratch(TC),
        compiler_params=pltpu.CompilerParams(dimension_semantics=("arbitrary",)),
        name="conv_context",
    )(*operands)


N_CONV_OPERANDS = 7


def _outproj_kernel(widths, fused_conv, x_ref, mod_ref, w_ref, *refs):
    if fused_conv:
        pieces = list(refs[:len(widths) - 1])
        conv_refs = refs[len(widths) - 1:len(widths) - 1 + N_CONV_OPERANDS]
        o_ref, ext_ref, sh_ref, oc_ref = refs[len(widths) - 1 + N_CONV_OPERANDS:]
    else:
        pieces, o_ref = list(refs[:-1]), refs[-1]
    y = None
    k0 = 0
    for piece, width in zip(pieces, widths):
        t = jnp.dot(piece[...], w_ref[k0:k0 + width, :], preferred_element_type=F32)
        y = t if y is None else y + t
        k0 += width
    if fused_conv:
        prev_ref, cur_ref, next_ref, cw_ref, cb_ref, cg_ref, cbeta_ref = conv_refs
        i = pl.program_id(0)
        _conv_fill(TM, i % TILES_PER_BATCH == 0, i % TILES_PER_BATCH == TILES_PER_BATCH - 1,
                   prev_ref, cur_ref, next_ref, ext_ref)
        for row0 in range(0, TM, TC):
            _conv_shift(row0, ext_ref, sh_ref)
            for c in range(CONV_CHUNKS):
                _conv_chunk(row0, c, sh_ref, cw_ref, cb_ref, cg_ref, cbeta_ref, oc_ref)
        y = y + jnp.dot(oc_ref[...], w_ref[k0:, :], preferred_element_type=F32)
    o_ref[...] = x_ref[...] + mod_ref[2:3, :] * y


def _outproj(xs, is_ctx, mod_l, w_out, layer, pieces, conv=None):
    rows = xs.shape[0]
    widths = tuple(int(p.shape[1]) for p in pieces) + ((C_CHANNELS,) if conv is not None else ())
    assert sum(widths) == MIX_WIDTH and all(p.shape[0] == rows for p in pieces)
    blocks = 4 * TM * D_MODEL * 4 + MIX_WIDTH * D_MODEL * 2 + 2 * TM * MIX_WIDTH * 2
    group = _group_of_tile(is_ctx, TM)
    operands = [_in_hbm(xs), mod_l, w_out] + list(pieces)
    in_specs = [
        pl.BlockSpec((TM, D_MODEL), lambda i: (i, 0)),
        pl.BlockSpec((None, None, 3, D_MODEL), lambda i: (group(i), 1, 0, 0)),
        pl.BlockSpec((None, MIX_WIDTH, D_MODEL), lambda i: (layer, 0, 0), pipeline_mode=pl.Buffered(1)),
    ] + [pl.BlockSpec((TM, w), lambda i: (i, 0)) for w in widths[:len(pieces)]]
    scratch = []
    if conv is not None:
        assert not is_ctx and SEQ % TM == 0 and TM % TC == 0 and conv[0].shape[0] == rows
        conv_operands, conv_specs = _conv_operands(conv[0], TM, *conv[1:])
        operands += conv_operands
        in_specs += conv_specs
        scratch = _conv_scratch(TM) + [pltpu.VMEM((TM, C_CHANNELS), BF16)]
        blocks += 4 * TM * C_CHANNELS * 4 + (TM + SUBLANES * CONV_SH_ROWS) * C_CHANNELS * 4
    return pl.pallas_call(
        functools.partial(_outproj_kernel, widths, conv is not None),
        grid=(rows // TM,),
        in_specs=in_specs,
        out_specs=pl.BlockSpec((TM, D_MODEL), lambda i: (i, 0)),
        out_shape=jax.ShapeDtypeStruct((rows, D_MODEL), F32),
        scratch_shapes=scratch,
        input_output_aliases={0: 0},
        compiler_params=pltpu.CompilerParams(
            dimension_semantics=("arbitrary",),
            vmem_limit_bytes=_vmem_limit(blocks)),
        name="outproj",
    )(*operands)


def kernel(x, c, ctx, c_ctx, w_mod, b_mod, norm_ffn1, norm_mix, norm_ffn2, ffn1_w_gate, ffn1_w_up,
           ffn1_w_down, ffn2_w_gate, ffn2_w_up, ffn2_w_down, w_in, w_out, a_q_norm, a_k_norm, a_sink,
           b_q_norm, b_k_norm, b_rpb, c_dw_w, c_dw_b, c_ln_g, c_ln_b):
    mods = _mod_vectors(c, c_ctx, w_mod, b_mod)
    rope = _rope_tables()
    b_bias = _b_bias_tables(b_rpb)
    ffn1_w = [w.astype(BF16) for w in (ffn1_w_gate, ffn1_w_up, ffn1_w_down)]
    ffn2_w = [w.astype(BF16) for w in (ffn2_w_gate, ffn2_w_up, ffn2_w_down)]
    w_in_bf = w_in.astype(BF16)
    w_out_bf = w_out.astype(BF16)
    q_scale = HEAD_DIM ** -0.5 * LOG2_E
    xs = x.reshape(N_LAT, D_MODEL)
    cs = ctx.reshape(N_CTX, D_MODEL)
    for l in range(DEPTH):
        last = l == DEPTH - 1
        mod_l = mods[l]
        owned = l > 0
        xs = _ffn(xs, False, mod_l, 0, norm_ffn1[l], *ffn1_w, l, in_place=owned)
        cs = _ffn(cs, True, mod_l, 0, norm_ffn1[l], *ffn1_w, l, in_place=owned)
        head_gains = jnp.concatenate([
            (a_q_norm[l] * q_scale)[None], a_k_norm[l][None], (b_q_norm[l] * q_scale)[None],
            b_k_norm[l][None], jnp.zeros((SUBLANES - N_HEAD_GAINS, HEAD_DIM), F32)], axis=0)
        qkv, cu = _inproj(xs, False, mod_l, norm_mix[l], w_in_bf, l, head_gains, rope)
        qkv_ctx, cu_ctx = _inproj(cs, True, mod_l, norm_mix[l], w_in_bf, l, head_gains, rope)
        sink_l = a_sink[l] * LOG2_E
        o_a = _attn_a(qkv, qkv_ctx, sink_l)
        o_b = _attn_b(qkv, qkv_ctx, b_bias, l)
        conv_params = (c_dw_w[l], c_dw_b[l], c_ln_g[l], c_ln_b[l])
        xs = _outproj(xs, False, mod_l, w_out_bf, l, [o_a, o_b], conv=(cu,) + conv_params)
        xs = _ffn(xs, False, mod_l, 2, norm_ffn2[l], *ffn2_w, l, in_place=True)
        if not last:
            sinks = jnp.concatenate([sink_l, jnp.full((B_HEADS,), NEG_INF, F32)])
            o_ctx = _attn_ctx(qkv_ctx, sinks)
            o_c_ctx = _conv_ctx(cu_ctx, *conv_params)
            cs = _outproj(cs, True, mod_l, w_out_bf, l, [o_ctx, o_c_ctx])
            cs = _ffn(cs, True, mod_l, 2, norm_ffn2[l], *ffn2_w, l, in_place=True)
    return xs.reshape(BATCH, SEQ, D_MODEL)
```

```python
import functools

import numpy as np
import jax
import jax.numpy as jnp
from jax import lax
from jax.experimental import pallas as pl
from jax.experimental.pallas import tpu as pltpu

D_MODEL = 2048
BATCH = 2
SEQ = 16384
DEPTH = 2
GRID_W = 64
GRID_ROWS = SEQ // GRID_W
CTX_LEN = 256
HEAD_DIM = 128
A_Q_HEADS = 6
A_KV_HEADS = 2
A_GROUP = A_Q_HEADS // A_KV_HEADS
A_WINDOW = 128
B_HEADS = 6
B_WIN_H = 8
B_WIN_W = 16
C_CHANNELS = 512
C_CONV_WIDTH = 31
C_PAD = (C_CONV_WIDTH - 1) // 2
D_FF = 5632
ROPE_THETA = 10000.0
NORM_EPS = 1e-6
NEG_INF = -1e30
N_MOD = 9
IN_COLS = 4608
MIX_WIDTH = 2048

HEAD_AQ = 0
HEAD_AK = HEAD_AQ + A_Q_HEADS
HEAD_AV = HEAD_AK + A_KV_HEADS
HEAD_BQ = HEAD_AV + A_KV_HEADS
HEAD_BK = HEAD_BQ + B_HEADS
HEAD_BV = HEAD_BK + B_HEADS
N_QKV_HEADS = HEAD_BV + B_HEADS
C_COL0 = N_QKV_HEADS * HEAD_DIM

N_LAT = BATCH * SEQ
N_CTX = BATCH * CTX_LEN

V7X_VMEM_BYTES = 64 * 1024 * 1024
SUBLANES = 8
LANES = 128

TM = 512
TILES_PER_BATCH = SEQ // TM
TF = 512
TC = 256
CONV_HALO = 16

F32 = jnp.float32
BF16 = jnp.bfloat16


MIB = 1024 * 1024
VMEM_TEMPORARIES_BYTES = 12 * MIB
VMEM_UNREQUESTED_BYTES = 4 * MIB


def _vmem_limit(block_bytes):
    return int(min(V7X_VMEM_BYTES - VMEM_UNREQUESTED_BYTES, block_bytes + VMEM_TEMPORARIES_BYTES))


def _group_of_tile(is_ctx, tile_rows):
    return (lambda i: BATCH) if is_ctx else (lambda i: i * tile_rows // SEQ)


def _in_hbm(stream):
    return pltpu.with_memory_space_constraint(stream, pltpu.HBM)


def _modulated(x, gain, shift, scale):
    ms = jnp.mean(x * x, axis=-1, keepdims=True)
    return (x * lax.rsqrt(ms + NORM_EPS) * gain) * (1.0 + scale) + shift


MOD_TN = 1024
MOD_GROUPS = BATCH + 1
MOD_UNROLL = 8


def _mod_kernel(c_ref, w_ref, b_ref, o_ref, a_ref):
    @pl.when(jnp.logical_and(pl.program_id(0) == 0, pl.program_id(1) == 0))
    def _():
        c = c_ref[...]
        a_ref[...] = c * jax.nn.sigmoid(c)

    lane_tiles = MOD_TN // LANES

    def body(step, accs):
        accs = list(accs)
        for u in range(MOD_UNROLL):
            r = pl.multiple_of((step * MOD_UNROLL + u) * SUBLANES, SUBLANES)
            for m in range(MOD_GROUPS):
                a = a_ref[m, pl.ds(r, SUBLANES), :]
                for t in range(lane_tiles):
                    idx = m * lane_tiles + t
                    accs[idx] = accs[idx] + w_ref[pl.ds(r, SUBLANES), t * LANES:(t + 1) * LANES] * a
        return tuple(accs)

    zero = jnp.zeros((SUBLANES, LANES), F32)
    accs = lax.fori_loop(0, D_MODEL // (SUBLANES * MOD_UNROLL), body, (zero,) * (MOD_GROUPS * lane_tiles))
    o_ref[...] = jnp.zeros_like(o_ref)
    for m in range(MOD_GROUPS):
        row = jnp.concatenate([jnp.sum(accs[m * lane_tiles + t], axis=0, keepdims=True)
                               for t in range(lane_tiles)], axis=1)
        o_ref[m:m + 1, :] = row + b_ref[...]


def _mod_vectors(c, c_ctx, w_mod, b_mod):
    rows = jnp.concatenate([c, c_ctx[None, :]], axis=0)
    c_cols = jnp.broadcast_to(rows[:, :, None], (MOD_GROUPS, D_MODEL, LANES))
    n = N_MOD * D_MODEL
    out = pl.pallas_call(
        _mod_kernel,
        grid=(DEPTH, n // MOD_TN),
        in_specs=[
            pl.BlockSpec((MOD_GROUPS, D_MODEL, LANES), lambda l, j: (0, 0, 0)),
            pl.BlockSpec((None, D_MODEL, MOD_TN), lambda l, j: (l, 0, j)),
            pl.BlockSpec((None, 1, MOD_TN), lambda l, j: (l, 0, j)),
        ],
        out_specs=pl.BlockSpec((None, SUBLANES, MOD_TN), lambda l, j: (l, 0, j)),
        out_shape=jax.ShapeDtypeStruct((DEPTH, SUBLANES, n), F32),
        scratch_shapes=[pltpu.VMEM((MOD_GROUPS, D_MODEL, LANES), F32)],
        compiler_params=pltpu.CompilerParams(
            dimension_semantics=("arbitrary", "arbitrary"),
            vmem_limit_bytes=_vmem_limit(2 * D_MODEL * MOD_TN * 4 + 3 * MOD_GROUPS * D_MODEL * LANES * 4)),
        name="mod_vectors",
    )(c_cols, w_mod, b_mod.reshape(DEPTH, 1, n))
    return out[:, :MOD_GROUPS].reshape(DEPTH, MOD_GROUPS, 3, 3, D_MODEL)


FFN_SUB = 512
FFN_TM_LAT = 1024


def _ffn_kernel(tm, x_ref, mod_ref, gain_ref, wg_ref, wu_ref, wd_ref, o_ref, h_ref):
    j = pl.program_id(1)
    last = pl.num_programs(1) - 1

    def chunk(first, final):
        for r0 in range(0, tm, FFN_SUB):
            rows = slice(r0, r0 + FFN_SUB)
            if first:
                h = _modulated(x_ref[rows, :], gain_ref[...], mod_ref[0:1, :], mod_ref[1:2, :])
                h_ref[rows, :] = h.astype(BF16)
            h = h_ref[rows, :]
            g = jnp.dot(h, wg_ref[...], preferred_element_type=F32)
            u = jnp.dot(h, wu_ref[...], preferred_element_type=F32)
            a = (g * jax.nn.sigmoid(g)) * u
            d = jnp.dot(a.astype(BF16), wd_ref[...], preferred_element_type=F32)
            if first:
                o_ref[rows, :] = d
            elif final:
                o_ref[rows, :] = x_ref[rows, :] + (0.5 * mod_ref[2:3, :]) * (o_ref[rows, :] + d)
            else:
                o_ref[rows, :] += d

    pl.when(j == 0)(functools.partial(chunk, True, False))
    pl.when(jnp.logical_and(j > 0, j < last))(functools.partial(chunk, False, False))
    pl.when(j == last)(functools.partial(chunk, False, True))


def _ffn(x, is_ctx, mod_l, sub, gain, wg, wu, wd, layer, in_place):
    tm = TM if is_ctx else FFN_TM_LAT
    rows = x.shape[0]
    assert tm % FFN_SUB == 0 and rows % tm == 0 and SEQ % tm == 0 and D_FF // TF >= 2
    blocks = 4 * tm * D_MODEL * 4 + 6 * D_MODEL * TF * 2 + tm * D_MODEL * 2
    group = _group_of_tile(is_ctx, tm)
    return pl.pallas_call(
        functools.partial(_ffn_kernel, tm),
        grid=(rows // tm, D_FF // TF),
        in_specs=[
            pl.BlockSpec((tm, D_MODEL), lambda i, j: (i, 0)),
            pl.BlockSpec((None, None, 3, D_MODEL), lambda i, j: (group(i), sub, 0, 0)),
            pl.BlockSpec((1, D_MODEL), lambda i, j: (0, 0)),
            pl.BlockSpec((None, D_MODEL, TF), lambda i, j: (layer, 0, j)),
            pl.BlockSpec((None, D_MODEL, TF), lambda i, j: (layer, 0, j)),
            pl.BlockSpec((None, TF, D_MODEL), lambda i, j: (layer, j, 0)),
        ],
        out_specs=pl.BlockSpec((tm, D_MODEL), lambda i, j: (i, 0)),
        out_shape=jax.ShapeDtypeStruct((rows, D_MODEL), F32),
        scratch_shapes=[pltpu.VMEM((tm, D_MODEL), BF16)],
        input_output_aliases={0: 0} if in_place else {},
        compiler_params=pltpu.CompilerParams(
            dimension_semantics=("arbitrary", "arbitrary"),
            vmem_limit_bytes=_vmem_limit(blocks)),
        name="ffn",
    )(_in_hbm(x), mod_l, gain.reshape(1, D_MODEL), wg, wu, wd)


GAIN_AQ, GAIN_AK, GAIN_BQ, GAIN_BK = 0, 1, 2, 3
N_HEAD_GAINS = 4


def _head_kind(h):
    if h < HEAD_AK:
        return GAIN_AQ, True
    if h < HEAD_AV:
        return GAIN_AK, True
    if h < HEAD_BQ:
        return None, False
    if h < HEAD_BK:
        return GAIN_BQ, False
    if h < HEAD_BV:
        return GAIN_BK, False
    return None, False


def _inproj_kernel(positioned, x_ref, mod_ref, gain_ref, w_ref, hg_ref, *refs):
    if positioned:
        cos_ref, sin_ref, qkv_ref, cu_ref, h_ref = refs
    else:
        qkv_ref, cu_ref, h_ref = refs
    h = _modulated(x_ref[...], gain_ref[...], mod_ref[0:1, :], mod_ref[1:2, :])
    h_ref[...] = h.astype(BF16)
    lane = lax.broadcasted_iota(jnp.int32, (TM, HEAD_DIM), 1)
    even_quarter = ((lane // (HEAD_DIM // 4)) % 2) == 0
    for pair in range(N_QKV_HEADS // 2):
        c0 = pair * 2 * HEAD_DIM
        y2 = jnp.dot(h_ref[...], w_ref[:, c0:c0 + 2 * HEAD_DIM], preferred_element_type=F32)
        for half in range(2):
            hd = 2 * pair + half
            y = y2[:, half * HEAD_DIM:(half + 1) * HEAD_DIM]
            gain_row, rotary = _head_kind(hd)
            if gain_row is not None:
                ms = jnp.mean(y * y, axis=-1, keepdims=True)
                y = y * lax.rsqrt(ms + NORM_EPS) * hg_ref[gain_row:gain_row + 1, :]
            if rotary and positioned:
                swapped = jnp.where(even_quarter,
                                    pltpu.roll(y, HEAD_DIM - HEAD_DIM // 4, 1),
                                    pltpu.roll(y, HEAD_DIM // 4, 1))
                y = y * cos_ref[...] + swapped * sin_ref[...]
            qkv_ref[hd] = y.astype(BF16)
    for blk in range((IN_COLS - C_COL0) // (2 * HEAD_DIM)):
        c0 = C_COL0 + blk * 2 * HEAD_DIM
        cu_ref[:, blk * 2 * HEAD_DIM:(blk + 1) * 2 * HEAD_DIM] = jnp.dot(
            h_ref[...], w_ref[:, c0:c0 + 2 * HEAD_DIM], preferred_element_type=F32)


def _inproj(xs, is_ctx, mod_l, gain, w_in, layer, head_gains, rope):
    rows = xs.shape[0]
    n_cu = IN_COLS - C_COL0
    blocks = (2 * TM * D_MODEL * 4 + D_MODEL * IN_COLS * 2 + 2 * N_QKV_HEADS * TM * HEAD_DIM * 2
              + 2 * TM * n_cu * 4 + 4 * TM * HEAD_DIM * 4 + TM * D_MODEL * 2)
    group = _group_of_tile(is_ctx, TM)
    operands = [xs, mod_l, gain.reshape(1, D_MODEL), w_in, head_gains]
    in_specs = [
        pl.BlockSpec((TM, D_MODEL), lambda i: (i, 0)),
        pl.BlockSpec((None, None, 3, D_MODEL), lambda i: (group(i), 1, 0, 0)),
        pl.BlockSpec((1, D_MODEL), lambda i: (0, 0)),
        pl.BlockSpec((None, D_MODEL, IN_COLS), lambda i: (layer, 0, 0), pipeline_mode=pl.Buffered(1)),
        pl.BlockSpec((SUBLANES, HEAD_DIM), lambda i: (0, 0)),
    ]
    if not is_ctx:
        operands += list(rope)
        in_specs += [pl.BlockSpec((TM, HEAD_DIM), lambda i: (i % TILES_PER_BATCH, 0))] * 2
    return pl.pallas_call(
        functools.partial(_inproj_kernel, not is_ctx),
        grid=(rows // TM,),
        in_specs=in_specs,
        out_specs=[
            pl.BlockSpec((N_QKV_HEADS, TM, HEAD_DIM), lambda i: (0, i, 0)),
            pl.BlockSpec((TM, n_cu), lambda i: (i, 0)),
        ],
        out_shape=[
            jax.ShapeDtypeStruct((N_QKV_HEADS, rows, HEAD_DIM), BF16),
            jax.ShapeDtypeStruct((rows, n_cu), F32),
        ],
        scratch_shapes=[pltpu.VMEM((TM, D_MODEL), BF16)],
        compiler_params=pltpu.CompilerParams(
            dimension_semantics=("arbitrary",),
            vmem_limit_bytes=_vmem_limit(blocks)),
        name="inproj",
    )(*operands)


def _rope_tables():
    t = np.arange(SEQ)
    n_freq = HEAD_DIM // 4
    inv_freq = ROPE_THETA ** (-np.arange(n_freq, dtype=np.float64) / n_freq)
    ang_r = (t // GRID_W)[:, None] * inv_freq[None, :]
    ang_c = (t % GRID_W)[:, None] * inv_freq[None, :]
    cos_t = np.concatenate([np.cos(ang_r), np.cos(ang_r), np.cos(ang_c), np.cos(ang_c)], axis=-1)
    sin_t = np.concatenate([-np.sin(ang_r), np.sin(ang_r), -np.sin(ang_c), np.sin(ang_c)], axis=-1)
    return jnp.asarray(cos_t, F32), jnp.asarray(sin_t, F32)


LOG2_E = float(np.log2(np.e))
ATTN_TQ = 256
ATTN_BLOCKS = SEQ // ATTN_TQ
ATTN_UNROLL = 16
PLACE_INTERIOR, PLACE_FIRST, PLACE_LAST = 0, 1, 2
N_PLACEMENTS = 3
PLACEMENT_BLOCKS = (1, 0, ATTN_BLOCKS - 1)

A_LEAD = A_WINDOW
A_KEYS = ATTN_TQ + 2 * A_WINDOW
B_ROWS_PER_BLOCK = ATTN_TQ // GRID_W
B_UNION_ROWS = B_ROWS_PER_BLOCK + B_WIN_H
B_LEAD = (B_WIN_H // 2) * GRID_W
B_KEYS = B_UNION_ROWS * GRID_W
assert ATTN_BLOCKS % ATTN_UNROLL == 0 and ATTN_BLOCKS >= 3
assert A_LEAD % LANES == 0 and B_LEAD % LANES == 0 and A_KEYS % LANES == 0 and B_KEYS % LANES == 0


def _dot_nt(a, b):
    return lax.dot_general(a, b, (((1,), (1,)), ((), ())), preferred_element_type=F32)


def _window_start(first_query, lead, n_keys):
    return int(np.clip(first_query - lead, 0, SEQ - n_keys))


def _local_attn_kernel(lead, n_keys, has_sink, *refs):
    if has_sink:
        sink_ref, refs = refs[0], refs[1:]
    q_ref, k_ref, v_ref, kc_ref, vc_ref, bias_ref, o_ref = refs
    sink = sink_ref[pl.program_id(1) * A_GROUP + pl.program_id(2)] if has_sink else None
    kc = kc_ref[...]
    mxu_sums = not has_sink
    vc = vc_ref[...]
    if mxu_sums:
        vc = jnp.concatenate([vc, jnp.ones((CTX_LEN, HEAD_DIM), BF16)], axis=1)

    def scores(i):
        q0 = pl.multiple_of(i * ATTN_TQ, ATTN_TQ)
        k0 = pl.multiple_of(jnp.clip(q0 - lead, 0, SEQ - n_keys), LANES)
        placement = jnp.where(i == 0, PLACE_FIRST, jnp.where(i == ATTN_BLOCKS - 1, PLACE_LAST, PLACE_INTERIOR))
        q = q_ref[pl.ds(q0, ATTN_TQ), :]
        s = _dot_nt(q, k_ref[pl.ds(k0, n_keys), :]) + bias_ref[placement]
        return q0, k0, s, _dot_nt(q, kc)

    def softmax(q0, k0, s, sc):
        m = jnp.maximum(jnp.max(s, axis=-1, keepdims=True), jnp.max(sc, axis=-1, keepdims=True))
        if has_sink:
            m = jnp.maximum(m, sink)
        p = jnp.exp2(s - m)
        pc = jnp.exp2(sc - m)
        if mxu_sums:
            den = None
        else:
            den = jnp.sum(p, axis=-1, keepdims=True) + jnp.sum(pc, axis=-1, keepdims=True)
            if has_sink:
                den = den + jnp.exp2(sink - m)
        return q0, k0, p.astype(BF16), pc.astype(BF16), den

    def output(q0, k0, p, pc, den):
        v = v_ref[pl.ds(k0, n_keys), :]
        if mxu_sums:
            v = jnp.concatenate([v, jnp.ones((n_keys, HEAD_DIM), BF16)], axis=1)
        o = jnp.dot(p, v, preferred_element_type=F32) + jnp.dot(pc, vc, preferred_element_type=F32)
        if mxu_sums:
            o, den = o[:, :HEAD_DIM], o[:, HEAD_DIM:HEAD_DIM + 1]
        o_ref[pl.ds(q0, ATTN_TQ), :] = (o / den).astype(BF16)

    def body(it, carry):
        scored, weighted = {}, {}
        for step in range(ATTN_UNROLL + 2):
            if step < ATTN_UNROLL:
                scored[step] = scores(it * ATTN_UNROLL + step)
            if 0 <= step - 2 < ATTN_UNROLL:
                output(*weighted.pop(step - 2))
            if 0 <= step - 1 < ATTN_UNROLL:
                weighted[step - 1] = softmax(*scored.pop(step - 1))
        return carry

    lax.fori_loop(0, ATTN_BLOCKS // ATTN_UNROLL, body, 0)


def _local_attn_vmem(n_keys):
    return 2 * 4 * SEQ * HEAD_DIM * 2 + 4 * CTX_LEN * HEAD_DIM * 2 + 2 * N_PLACEMENTS * ATTN_TQ * n_keys * 4


SEQ_BLOCK = (None, SEQ, HEAD_DIM)
CTX_BLOCK = (None, CTX_LEN, HEAD_DIM)


def _a_band_table():
    tables = []
    for blk in PLACEMENT_BLOCKS:
        q0 = blk * ATTN_TQ
        kpos = _window_start(q0, A_LEAD, A_KEYS) + np.arange(A_KEYS)[None, :]
        qpos = q0 + np.arange(ATTN_TQ)[:, None]
        tables.append(np.where(np.abs(kpos - qpos) <= A_WINDOW, 0.0, NEG_INF))
    return jnp.asarray(np.stack(tables), F32)


def _attn_a(qkv, qkv_ctx, sink):
    return pl.pallas_call(
        functools.partial(_local_attn_kernel, A_LEAD, A_KEYS, True),
        grid=(BATCH, A_KV_HEADS, A_GROUP),
        in_specs=[
            pl.BlockSpec(memory_space=pltpu.SMEM),
            pl.BlockSpec(SEQ_BLOCK, lambda b, kv, g: (HEAD_AQ + kv * A_GROUP + g, b, 0)),
            pl.BlockSpec(SEQ_BLOCK, lambda b, kv, g: (HEAD_AK + kv, b, 0)),
            pl.BlockSpec(SEQ_BLOCK, lambda b, kv, g: (HEAD_AV + kv, b, 0)),
            pl.BlockSpec(CTX_BLOCK, lambda b, kv, g: (HEAD_AK + kv, b, 0)),
            pl.BlockSpec(CTX_BLOCK, lambda b, kv, g: (HEAD_AV + kv, b, 0)),
            pl.BlockSpec((N_PLACEMENTS, ATTN_TQ, A_KEYS), lambda b, kv, g: (0, 0, 0)),
        ],
        out_specs=pl.BlockSpec((SEQ, HEAD_DIM), lambda b, kv, g: (b, kv * A_GROUP + g)),
        out_shape=jax.ShapeDtypeStruct((N_LAT, A_Q_HEADS * HEAD_DIM), BF16),
        compiler_params=pltpu.CompilerParams(
            dimension_semantics=("arbitrary", "arbitrary", "arbitrary"),
            vmem_limit_bytes=_vmem_limit(_local_attn_vmem(A_KEYS))),
        name="attn_window",
    )(sink, qkv, qkv, qkv, qkv_ctx, qkv_ctx, _a_band_table())


def _b_bias_tables(rpb):
    j = np.arange(GRID_W)[:, None]
    jk = np.arange(GRID_W)[None, :]
    col_start = np.clip(j - B_WIN_W // 2, 0, GRID_W - B_WIN_W)
    inside = (jk >= col_start) & (jk < col_start + B_WIN_W)
    n_col = 2 * B_WIN_W - 1
    onehot = (inside[:, :, None] & ((jk - j + (B_WIN_W - 1))[:, :, None] == np.arange(n_col))).astype(np.float32)
    by_row = jnp.einsum('lhac,jkc->lhajk', rpb, jnp.asarray(onehot), precision=lax.Precision.HIGHEST)
    by_row = by_row * LOG2_E + jnp.asarray(np.where(inside, 0.0, NEG_INF), F32)
    masked = jnp.full((DEPTH, B_HEADS, GRID_W, GRID_W), NEG_INF, F32)
    query_rows = []
    for blk in PLACEMENT_BLOCKS:
        r0 = blk * B_ROWS_PER_BLOCK
        u0 = _window_start(r0 * GRID_W, B_LEAD, B_KEYS) // GRID_W
        for t in range(B_ROWS_PER_BLOCK):
            r = r0 + t
            rs = int(np.clip(r - B_WIN_H // 2, 0, GRID_ROWS - B_WIN_H))
            before = rs - u0
            assert 0 <= before <= B_UNION_ROWS - B_WIN_H
            bias_row0 = rs - r + (B_WIN_H - 1)
            pieces = [by_row[:, :, bias_row0 + u - before] if before <= u < before + B_WIN_H else masked
                      for u in range(B_UNION_ROWS)]
            query_rows.append(jnp.concatenate(pieces, axis=-1))
    return jnp.stack(query_rows, axis=2).reshape(DEPTH, B_HEADS, N_PLACEMENTS, ATTN_TQ, B_KEYS)


def _attn_b(qkv, qkv_ctx, bias, layer):
    return pl.pallas_call(
        functools.partial(_local_attn_kernel, B_LEAD, B_KEYS, False),
        grid=(BATCH, B_HEADS),
        in_specs=[
            pl.BlockSpec(SEQ_BLOCK, lambda b, h: (HEAD_BQ + h, b, 0)),
            pl.BlockSpec(SEQ_BLOCK, lambda b, h: (HEAD_BK + h, b, 0)),
            pl.BlockSpec(SEQ_BLOCK, lambda b, h: (HEAD_BV + h, b, 0)),
            pl.BlockSpec(CTX_BLOCK, lambda b, h: (HEAD_BK + h, b, 0)),
            pl.BlockSpec(CTX_BLOCK, lambda b, h: (HEAD_BV + h, b, 0)),
            pl.BlockSpec((None, None, N_PLACEMENTS, ATTN_TQ, B_KEYS), lambda b, h: (layer, h, 0, 0, 0)),
        ],
        out_specs=pl.BlockSpec((SEQ, HEAD_DIM), lambda b, h: (b, h)),
        out_shape=jax.ShapeDtypeStruct((N_LAT, B_HEADS * HEAD_DIM), BF16),
        compiler_params=pltpu.CompilerParams(
            dimension_semantics=("arbitrary", "arbitrary"),
            vmem_limit_bytes=_vmem_limit(_local_attn_vmem(B_KEYS))),
        name="attn_neighbourhood",
    )(qkv, qkv, qkv, qkv_ctx, qkv_ctx, bias)


N_CTX_HEADS = A_Q_HEADS + B_HEADS


def _attn_ctx_kernel(sink_ref, q_ref, k_ref, v_ref, o_ref):
    sink = sink_ref[pl.program_id(1)]
    q = q_ref[...]
    s = _dot_nt(q, k_ref[...])
    m = jnp.maximum(jnp.max(s, axis=-1, keepdims=True), sink)
    p = jnp.exp2(s - m)
    v_aug = jnp.concatenate([v_ref[...], jnp.ones((CTX_LEN, HEAD_DIM), BF16)], axis=1)
    o = jnp.dot(p.astype(BF16), v_aug, preferred_element_type=F32)
    den = o[:, HEAD_DIM:HEAD_DIM + 1] + jnp.exp2(sink - m)
    o_ref[...] = (o[:, :HEAD_DIM] / den).astype(BF16)


def _attn_ctx(qkv_ctx, sinks):
    is_b = lambda h: h >= A_Q_HEADS
    q_head = lambda h: jnp.where(is_b(h), HEAD_BQ + h - A_Q_HEADS, HEAD_AQ + h)
    k_head = lambda h: jnp.where(is_b(h), HEAD_BK + h - A_Q_HEADS, HEAD_AK + h // A_GROUP)
    v_head = lambda h: jnp.where(is_b(h), HEAD_BV + h - A_Q_HEADS, HEAD_AV + h // A_GROUP)
    return pl.pallas_call(
        _attn_ctx_kernel,
        grid=(BATCH, N_CTX_HEADS),
        in_specs=[
            pl.BlockSpec(memory_space=pltpu.SMEM),
            pl.BlockSpec(CTX_BLOCK, lambda b, h: (q_head(h), b, 0)),
            pl.BlockSpec(CTX_BLOCK, lambda b, h: (k_head(h), b, 0)),
            pl.BlockSpec(CTX_BLOCK, lambda b, h: (v_head(h), b, 0)),
        ],
        out_specs=pl.BlockSpec((CTX_LEN, HEAD_DIM), lambda b, h: (b, h)),
        out_shape=jax.ShapeDtypeStruct((N_CTX, N_CTX_HEADS * HEAD_DIM), BF16),
        compiler_params=pltpu.CompilerParams(dimension_semantics=("arbitrary", "arbitrary")),
        name="attn_context",
    )(sinks, qkv_ctx, qkv_ctx, qkv_ctx)


CONV_ROWS = 32
CONV_SH_ROWS = TC + SUBLANES * ((C_CONV_WIDTH - 1) // SUBLANES)
assert SUBLANES - 1 + CONV_HALO - C_PAD + CONV_SH_ROWS <= TC + 2 * CONV_HALO
CONV_CHUNKS = TC // CONV_ROWS


def _glu(u):
    return u[:, :C_CHANNELS] * jax.nn.sigmoid(u[:, C_CHANNELS:])


def _conv_fill(n_rows, first, last, prev_ref, cur_ref, next_ref, ext_ref):
    ext_ref[0:CONV_HALO, :] = jnp.where(first, 0.0, _glu(prev_ref[...]))
    ext_ref[CONV_HALO:CONV_HALO + n_rows, :] = _glu(cur_ref[...])
    ext_ref[CONV_HALO + n_rows:, :] = jnp.where(last, 0.0, _glu(next_ref[...]))


def _conv_shift(row0, ext_ref, sh_ref):
    for b in range(SUBLANES):
        lo = row0 + b + CONV_HALO - C_PAD
        sh_ref[b] = ext_ref[lo:lo + CONV_SH_ROWS, :]


def _conv_chunk(row0, c, sh_ref, w_ref, b_ref, g_ref, beta_ref, out_ref):
    groups = CONV_ROWS // SUBLANES
    r0 = c * CONV_ROWS
    accs = [jnp.zeros((SUBLANES, C_CHANNELS), F32) for _ in range(groups)]
    for k in range(C_CONV_WIDTH):
        a, b = divmod(k, SUBLANES)
        w_k = w_ref[k]
        for g in range(groups):
            lo = r0 + SUBLANES * (a + g)
            accs[g] = accs[g] + sh_ref[b, lo:lo + SUBLANES, :] * w_k
    acc = jnp.concatenate(accs, axis=0) + b_ref[...]
    mu = jnp.mean(acc, axis=-1, keepdims=True)
    xc = acc - mu
    y = xc * lax.rsqrt(jnp.mean(xc * xc, axis=-1, keepdims=True) + NORM_EPS)
    y = y * g_ref[...] + beta_ref[...]
    out_ref[row0 + r0:row0 + r0 + CONV_ROWS, :] = (y * jax.nn.sigmoid(y)).astype(BF16)


def _conv_operands(cu, n_rows, dw_w, dw_b, ln_g, ln_b):
    halo_per_tile = n_rows // CONV_HALO
    n_halo = cu.shape[0] // CONV_HALO
    vec = pl.BlockSpec((1, C_CHANNELS), lambda i: (0, 0))
    specs = [
        pl.BlockSpec((CONV_HALO, 2 * C_CHANNELS), lambda i: (jnp.maximum(i * halo_per_tile - 1, 0), 0)),
        pl.BlockSpec((n_rows, 2 * C_CHANNELS), lambda i: (i, 0)),
        pl.BlockSpec((CONV_HALO, 2 * C_CHANNELS), lambda i: (jnp.minimum((i + 1) * halo_per_tile, n_halo - 1), 0)),
        pl.BlockSpec((C_CONV_WIDTH, SUBLANES, C_CHANNELS), lambda i: (0, 0, 0)),
        vec, vec, vec,
    ]
    operands = [cu, cu, cu, jnp.broadcast_to(dw_w[:, None, :], (C_CONV_WIDTH, SUBLANES, C_CHANNELS)),
                dw_b.reshape(1, -1), ln_g.reshape(1, -1), ln_b.reshape(1, -1)]
    return operands, specs


def _conv_scratch(n_rows):
    return [pltpu.VMEM((n_rows + 2 * CONV_HALO, C_CHANNELS), F32),
            pltpu.VMEM((SUBLANES, CONV_SH_ROWS, C_CHANNELS), F32)]


def _conv_ctx_kernel(prev_ref, cur_ref, next_ref, w_ref, b_ref, g_ref, beta_ref, o_ref, ext_ref, sh_ref):
    _conv_fill(TC, True, True, prev_ref, cur_ref, next_ref, ext_ref)
    _conv_shift(0, ext_ref, sh_ref)
    for c in range(CONV_CHUNKS):
        _conv_chunk(0, c, sh_ref, w_ref, b_ref, g_ref, beta_ref, o_ref)


def _conv_ctx(cu_ctx, dw_w, dw_b, ln_g, ln_b):
    assert CTX_LEN == TC
    operands, specs = _conv_operands(cu_ctx, TC, dw_w, dw_b, ln_g, ln_b)
    return pl.pallas_call(
        _conv_ctx_kernel,
        grid=(N_CTX // TC,),
        in_specs=specs,
        out_specs=pl.BlockSpec((TC, C_CHANNELS), lambda i: (i, 0)),
        out_shape=jax.ShapeDtypeStruct((N_CTX, C_CHANNELS), BF16),
        scratch_shapes=_conv_scratch(TC),
        compiler_params=pltpu.CompilerParams(dimension_semantics=("arbitrary",)),
        name="conv_context",
    )(*operands)


N_CONV_OPERANDS = 7


def _outproj_kernel(widths, fused_conv, x_ref, mod_ref, w_ref, *refs):
    if fused_conv:
        pieces = list(refs[:len(widths) - 1])
        conv_refs = refs[len(widths) - 1:len(widths) - 1 + N_CONV_OPERANDS]
        o_ref, ext_ref, sh_ref, oc_ref = refs[len(widths) - 1 + N_CONV_OPERANDS:]
    else:
        pieces, o_ref = list(refs[:-1]), refs[-1]
    y = None
    k0 = 0
    for piece, width in zip(pieces, widths):
        t = jnp.dot(piece[...], w_ref[k0:k0 + width, :], preferred_element_type=F32)
        y = t if y is None else y + t
        k0 += width
    if fused_conv:
        prev_ref, cur_ref, next_ref, cw_ref, cb_ref, cg_ref, cbeta_ref = conv_refs
        i = pl.program_id(0)
        _conv_fill(TM, i % TILES_PER_BATCH == 0, i % TILES_PER_BATCH == TILES_PER_BATCH - 1,
                   prev_ref, cur_ref, next_ref, ext_ref)
        for row0 in range(0, TM, TC):
            _conv_shift(row0, ext_ref, sh_ref)
            for c in range(CONV_CHUNKS):
                _conv_chunk(row0, c, sh_ref, cw_ref, cb_ref, cg_ref, cbeta_ref, oc_ref)
        y = y + jnp.dot(oc_ref[...], w_ref[k0:, :], preferred_element_type=F32)
    o_ref[...] = x_ref[...] + mod_ref[2:3, :] * y


def _outproj(xs, is_ctx, mod_l, w_out, layer, pieces, conv=None):
    rows = xs.shape[0]
    widths = tuple(int(p.shape[1]) for p in pieces) + ((C_CHANNELS,) if conv is not None else ())
    assert sum(widths) == MIX_WIDTH and all(p.shape[0] == rows for p in pieces)
    blocks = 4 * TM * D_MODEL * 4 + MIX_WIDTH * D_MODEL * 2 + 2 * TM * MIX_WIDTH * 2
    group = _group_of_tile(is_ctx, TM)
    operands = [_in_hbm(xs), mod_l, w_out] + list(pieces)
    in_specs = [
        pl.BlockSpec((TM, D_MODEL), lambda i: (i, 0)),
        pl.BlockSpec((None, None, 3, D_MODEL), lambda i: (group(i), 1, 0, 0)),
        pl.BlockSpec((None, MIX_WIDTH, D_MODEL), lambda i: (layer, 0, 0), pipeline_mode=pl.Buffered(1)),
    ] + [pl.BlockSpec((TM, w), lambda i: (i, 0)) for w in widths[:len(pieces)]]
    scratch = []
    if conv is not None:
        assert not is_ctx and SEQ % TM == 0 and TM % TC == 0 and conv[0].shape[0] == rows
        conv_operands, conv_specs = _conv_operands(conv[0], TM, *conv[1:])
        operands += conv_operands
        in_specs += conv_specs
        scratch = _conv_scratch(TM) + [pltpu.VMEM((TM, C_CHANNELS), BF16)]
        blocks += 4 * TM * C_CHANNELS * 4 + (TM + SUBLANES * CONV_SH_ROWS) * C_CHANNELS * 4
    return pl.pallas_call(
        functools.partial(_outproj_kernel, widths, conv is not None),
        grid=(rows // TM,),
        in_specs=in_specs,
        out_specs=pl.BlockSpec((TM, D_MODEL), lambda i: (i, 0)),
        out_shape=jax.ShapeDtypeStruct((rows, D_MODEL), F32),
        scratch_shapes=scratch,
        input_output_aliases={0: 0},
        compiler_params=pltpu.CompilerParams(
            dimension_semantics=("arbitrary",),
            vmem_limit_bytes=_vmem_limit(blocks)),
        name="outproj",
    )(*operands)


def kernel(x, c, ctx, c_ctx, w_mod, b_mod, norm_ffn1, norm_mix, norm_ffn2, ffn1_w_gate, ffn1_w_up,
           ffn1_w_down, ffn2_w_gate, ffn2_w_up, ffn2_w_down, w_in, w_out, a_q_norm, a_k_norm, a_sink,
           b_q_norm, b_k_norm, b_rpb, c_dw_w, c_dw_b, c_ln_g, c_ln_b):
    mods = _mod_vectors(c, c_ctx, w_mod, b_mod)
    rope = _rope_tables()
    b_bias = _b_bias_tables(b_rpb)
    ffn1_w = [w.astype(BF16) for w in (ffn1_w_gate, ffn1_w_up, ffn1_w_down)]
    ffn2_w = [w.astype(BF16) for w in (ffn2_w_gate, ffn2_w_up, ffn2_w_down)]
    w_in_bf = w_in.astype(BF16)
    w_out_bf = w_out.astype(BF16)
    q_scale = HEAD_DIM ** -0.5 * LOG2_E
    xs = x.reshape(N_LAT, D_MODEL)
    cs = ctx.reshape(N_CTX, D_MODEL)
    for l in range(DEPTH):
        last = l == DEPTH - 1
        mod_l = mods[l]
        owned = l > 0
        xs = _ffn(xs, False, mod_l, 0, norm_ffn1[l], *ffn1_w, l, in_place=owned)
        cs = _ffn(cs, True, mod_l, 0, norm_ffn1[l], *ffn1_w, l, in_place=owned)
        head_gains = jnp.concatenate([
            (a_q_norm[l] * q_scale)[None], a_k_norm[l][None], (b_q_norm[l] * q_scale)[None],
            b_k_norm[l][None], jnp.zeros((SUBLANES - N_HEAD_GAINS, HEAD_DIM), F32)], axis=0)
        qkv, cu = _inproj(xs, False, mod_l, norm_mix[l], w_in_bf, l, head_gains, rope)
        qkv_ctx, cu_ctx = _inproj(cs, True, mod_l, norm_mix[l], w_in_bf, l, head_gains, rope)
        sink_l = a_sink[l] * LOG2_E
        o_a = _attn_a(qkv, qkv_ctx, sink_l)
        o_b = _attn_b(qkv, qkv_ctx, b_bias, l)
        conv_params = (c_dw_w[l], c_dw_b[l], c_ln_g[l], c_ln_b[l])
        xs = _outproj(xs, False, mod_l, w_out_bf, l, [o_a, o_b], conv=(cu,) + conv_params)
        xs = _ffn(xs, False, mod_l, 2, norm_ffn2[l], *ffn2_w, l, in_place=True)
        if not last:
            sinks = jnp.concatenate([sink_l, jnp.full((B_HEADS,), NEG_INF, F32)])
            o_ctx = _attn_ctx(qkv_ctx, sinks)
            o_c_ctx = _conv_ctx(cu_ctx, *conv_params)
            cs = _outproj(cs, True, mod_l, w_out_bf, l, [o_ctx, o_c_ctx])
            cs = _ffn(cs, True, mod_l, 2, norm_ffn2[l], *ffn2_w, l, in_place=True)
    return xs.reshape(BATCH, SEQ, D_MODEL)
```
